```python
import math
import jax, jax.numpy as jnp
from jax import lax
import numpy as np

D_MODEL = 1024
BATCH = 4
SEQ = 8192
DEPTH = 2

HEAD_DIM = 64
N_HEADS = 16
D_MIX = N_HEADS * HEAD_DIM
N_HEADS_A = 8
N_KV_A = 2
WINDOW = 128
N_HEADS_B = 4
MOBA_BLOCK = 256
MOBA_TOPK = 3
N_HEADS_C = 4
C_KV_LATENT = 128
N_IDX_HEADS = 4
IDX_DIM = 64
DSA_TOPK = 256
D_FF = 2816
N_BUCKETS = 32
MAX_DISTANCE = 128
RMS_EPS = 1e-6
Q_BLOCK = 128

A_Q = N_HEADS_A * HEAD_DIM
A_KV = N_KV_A * HEAD_DIM
B_QKV = N_HEADS_B * HEAD_DIM
C_Q = N_HEADS_C * HEAD_DIM
C_QIDX = N_IDX_HEADS * IDX_DIM
COL_SIZES = (A_Q, A_KV, A_KV, B_QKV, B_QKV, B_QKV, C_Q, C_KV_LATENT, C_QIDX, IDX_DIM, N_IDX_HEADS)
D_IN = A_Q + 2 * A_KV + 3 * B_QKV + C_Q + C_KV_LATENT + C_QIDX + IDX_DIM + N_IDX_HEADS

kernel_name = "hymba_style_swa_moba_dsa_macaron"


def _rmsnorm(x, g):
    x32 = x.astype(jnp.float32)
    y = x32 * lax.rsqrt(jnp.mean(x32 * x32, axis=-1, keepdims=True) + RMS_EPS)
    return (y * g.astype(jnp.float32)).astype(x.dtype)


def _swiglu(x, w_gate, w_up, w_down):
    return (jax.nn.silu(x @ w_gate) * (x @ w_up)) @ w_down


def _t5_bucket(dist):
    n = jnp.maximum(dist, 0)
    max_exact = N_BUCKETS // 2
    nf = jnp.maximum(n, 1).astype(jnp.float32)
    large = max_exact + (jnp.log(nf / max_exact) / math.log(MAX_DISTANCE / max_exact)
                         * (N_BUCKETS - max_exact)).astype(jnp.int32)
    large = jnp.minimum(large, N_BUCKETS - 1)
    return jnp.where(n < max_exact, n, large)


def _split_cols(p):
    cuts = []
    acc = 0
    for sz in COL_SIZES[:-1]:
        acc += sz
        cuts.append(acc)
    return jnp.split(p, cuts, axis=-1)


def _sliding_window_attn(q, k, v, sinks, bias_tab):
    B, T = q.shape[0], q.shape[1]
    nq = T // Q_BLOCK
    G = N_HEADS_A // N_KV_A
    pad = ((0, 0), (WINDOW, 0), (0, 0), (0, 0))
    kp = jnp.pad(k, pad).reshape(B, nq + 1, Q_BLOCK, N_KV_A, HEAD_DIM)
    vp = jnp.pad(v, pad).reshape(B, nq + 1, Q_BLOCK, N_KV_A, HEAD_DIM)
    kb = jnp.concatenate([kp[:, :-1], kp[:, 1:]], axis=2)
    vb = jnp.concatenate([vp[:, :-1], vp[:, 1:]], axis=2)
    qb = q.reshape(B, nq, Q_BLOCK, N_KV_A, G, HEAD_DIM)
    s = jnp.einsum('bnqkgd,bnskd->bnkgqs', qb, kb).astype(jnp.float32) * (HEAD_DIM ** -0.5)
    qi = jnp.arange(Q_BLOCK)[:, None]
    si = jnp.arange(2 * Q_BLOCK)[None, :]
    dist = WINDOW + qi - si
    bias = bias_tab[:, _t5_bucket(dist)].astype(jnp.float32)
    bias = bias.reshape(N_KV_A, G, Q_BLOCK, 2 * Q_BLOCK)
    key_pos = jnp.arange(nq)[:, None] * Q_BLOCK - WINDOW + si
    valid = ((dist >= 0) & (dist < WINDOW))[None] & (key_pos >= 0)[:, None, :]
    s = jnp.where(valid[None, :, None, None], s + bias, -jnp.inf)
    sink = sinks.astype(jnp.float32).reshape(N_KV_A, G)[None, None, :, :, None, None]
    m = jnp.maximum(jnp.max(s, axis=-1, keepdims=True), sink)
    p = jnp.exp(s - m)
    denom = jnp.sum(p, axis=-1, keepdims=True) + jnp.exp(sink - m)
    o = jnp.einsum('bnkgqs,bnskd->bnqkgd', (p / denom).astype(v.dtype), vb)
    return o.reshape(B, T, N_HEADS_A * HEAD_DIM)


def _moba_attn(q, k, v, bias_tab):
    B, T = q.shape[0], q.shape[1]
    nb = -(-T // MOBA_BLOCK)
    Tp = nb * MOBA_BLOCK
    pad = ((0, 0), (0, Tp - T), (0, 0), (0, 0))
    kbh = jnp.pad(k, pad).reshape(B, nb, MOBA_BLOCK, N_HEADS_B, HEAD_DIM).transpose(0, 3, 1, 2, 4)
    vbh = jnp.pad(v, pad).reshape(B, nb, MOBA_BLOCK, N_HEADS_B, HEAD_DIM).transpose(0, 3, 1, 2, 4)
    n_sel = min(MOBA_TOPK, nb - 1)
    scale = HEAD_DIM ** -0.5
    h_idx = jnp.arange(N_HEADS_B)[None, :, None, None]
    b_idx = jnp.arange(B)[:, None, None, None]
    if n_sel > 0:
        k_mean = jnp.mean(kbh.astype(jnp.float32), axis=3)
        gate = jnp.einsum('bthd,bhnd->bhtn', q.astype(jnp.float32), k_mean)
        own_all = jnp.arange(T) // MOBA_BLOCK
        past = jnp.arange(nb)[None, :] < own_all[:, None]
        gate = jnp.where(past[None, None], gate, -jnp.inf)
        _, sel_idx = lax.top_k(gate, n_sel)

    def chunk(i):
        t0 = i * Q_BLOCK
        tq = t0 + jnp.arange(Q_BLOCK)
        qc = lax.dynamic_slice_in_dim(q, t0, Q_BLOCK, axis=1).transpose(0, 2, 1, 3)
        ob = t0 // MOBA_BLOCK
        k_own = lax.dynamic_index_in_dim(kbh, ob, axis=2, keepdims=False)
        v_own = lax.dynamic_index_in_dim(vbh, ob, axis=2, keepdims=False)
        d_own = tq[:, None] - (ob * MOBA_BLOCK + jnp.arange(MOBA_BLOCK))[None, :]
        s_own = (jnp.einsum('bhqd,bhsd->bhqs', qc, k_own).astype(jnp.float32) * scale
                 + bias_tab[:, _t5_bucket(d_own)].astype(jnp.float32)[None])
        s_own = jnp.where((d_own >= 0)[None, None], s_own, -jnp.inf)
        if n_sel == 0:
            p_own = jax.nn.softmax(s_own, axis=-1)
            o = jnp.einsum('bhqs,bhsd->bhqd', p_own.astype(v.dtype), v_own)
            return o.transpose(0, 2, 1, 3)
        idx_c = lax.dynamic_slice_in_dim(sel_idx, t0, Q_BLOCK, axis=2)
        k_sel = kbh[b_idx, h_idx, idx_c]
        v_sel = vbh[b_idx, h_idx, idx_c]
        sel_pos = idx_c[..., None] * MOBA_BLOCK + jnp.arange(MOBA_BLOCK)
        d_sel = tq[None, None, :, None, None] - sel_pos
        b_sel = bias_tab[h_idx[..., None], _t5_bucket(d_sel)].astype(jnp.float32)
        s_sel = jnp.einsum('bhqd,bhqjsd->bhqjs', qc, k_sel).astype(jnp.float32) * scale + b_sel
        ok = (idx_c < (tq // MOBA_BLOCK)[None, None, :, None])[..., None]
        s_sel = jnp.where(ok, s_sel, -jnp.inf).reshape(B, N_HEADS_B, Q_BLOCK, n_sel * MOBA_BLOCK)
        p = jax.nn.softmax(jnp.concatenate([s_sel, s_own], axis=-1), axis=-1).astype(v.dtype)
        p_sel = p[..., :n_sel * MOBA_BLOCK].reshape(B, N_HEADS_B, Q_BLOCK, n_sel, MOBA_BLOCK)
        p_own = p[..., n_sel * MOBA_BLOCK:]
        o = (jnp.einsum('bhqjs,bhqjsd->bhqd', p_sel, v_sel)
             + jnp.einsum('bhqs,bhsd->bhqd', p_own, v_own))
        return o.transpose(0, 2, 1, 3)

    out = lax.map(chunk, jnp.arange(T // Q_BLOCK))
    return out.transpose(1, 0, 2, 3, 4).reshape(B, T, N_HEADS_B * HEAD_DIM)


def _dsa_attn(q, k, v, q_idx, k_idx, w_idx, bias_tab):
    B, T = q.shape[0], q.shape[1]
    n_keep = min(DSA_TOPK, T // 4)
    key_pos = jnp.arange(T)
    b_idx = jnp.arange(B)[:, None, None]
    scale = HEAD_DIM ** -0.5

    def chunk(i):
        t0 = i * Q_BLOCK
        tq = t0 + jnp.arange(Q_BLOCK)
        qi = lax.dynamic_slice_in_dim(q_idx, t0, Q_BLOCK, axis=1)
        wi = lax.dynamic_slice_in_dim(w_idx, t0, Q_BLOCK, axis=1).astype(jnp.float32)
        dots = jnp.einsum('bqhd,bsd->bqhs', qi, k_idx).astype(jnp.float32) * (IDX_DIM ** -0.5)
        score = jnp.einsum('bqhs,bqh->bqs', jax.nn.relu(dots), wi)
        score = jnp.where((key_pos[None, :] <= tq[:, None])[None], score, -jnp.inf)
        _, idx = lax.top_k(score, n_keep)
        k_sel = k[b_idx, idx]
        v_sel = v[b_idx, idx]
        qc = lax.dynamic_slice_in_dim(q, t0, Q_BLOCK, axis=1)
        d = tq[None, :, None] - idx
        bias = bias_tab[:, _t5_bucket(d)].astype(jnp.float32).transpose(1, 0, 2, 3)
        s = jnp.einsum('bqhd,bqkd->bhqk', qc, k_sel).astype(jnp.float32) * scale + bias
        s = jnp.where((d >= 0)[:, None], s, -jnp.inf)
        p = jax.nn.softmax(s, axis=-1).astype(v.dtype)
        return jnp.einsum('bhqk,bqkd->bqhd', p, v_sel)

    out = lax.map(chunk, jnp.arange(T // Q_BLOCK))
    return out.transpose(1, 0, 2, 3, 4).reshape(B, T, N_HEADS_C * HEAD_DIM)


def setup_inputs(seed: int = 0) -> dict:
    key = jax.random.key(seed)
    ks = jax.random.split(key, 18)

    def nrm(k, shape, fan_in):
        return jax.random.normal(k, shape, jnp.float32) * (fan_in ** -0.5)

    def gain(k, shape):
        return 1.0 + 0.05 * jax.random.normal(k, shape, jnp.float32)

    return {
        "x": jax.random.normal(ks[0], (BATCH, SEQ, D_MODEL), jnp.float32),
        "rel_bias_table": 0.5 * jax.random.normal(ks[1], (N_BUCKETS, N_HEADS), jnp.float32),
        "ffn1_norm": gain(ks[2], (DEPTH, D_MODEL)),
        "ffn1_w_gate": nrm(ks[3], (DEPTH, D_MODEL, D_FF), D_MODEL),
        "ffn1_w_up": nrm(ks[4], (DEPTH, D_MODEL, D_FF), D_MODEL),
        "ffn1_w_down": nrm(ks[5], (DEPTH, D_FF, D_MODEL), D_FF),
        "mix_norm": gain(ks[6], (DEPTH, D_MODEL)),
        "w_in": nrm(ks[7], (DEPTH, D_MODEL, D_IN), D_MODEL),
        "attn_sinks": 0.5 * jax.random.normal(ks[8], (DEPTH, N_HEADS_A), jnp.float32),
        "kv_norm_c": gain(ks[9], (DEPTH, C_KV_LATENT)),
        "w_kv_up_c": nrm(ks[10], (DEPTH, C_KV_LATENT, 2 * HEAD_DIM), C_KV_LATENT),
        "w_out": nrm(ks[11], (DEPTH, D_MIX, D_MODEL), D_MIX),
        "ffn2_norm": gain(ks[12], (DEPTH, D_MODEL)),
        "ffn2_w_gate": nrm(ks[13], (DEPTH, D_MODEL, D_FF), D_MODEL),
        "ffn2_w_up": nrm(ks[14], (DEPTH, D_MODEL, D_FF), D_MODEL),
        "ffn2_w_down": nrm(ks[15], (DEPTH, D_FF, D_MODEL), D_FF),
        "final_norm": gain(ks[16], (D_MODEL,)),
    }


def reference(x, rel_bias_table, ffn1_norm, ffn1_w_gate, ffn1_w_up, ffn1_w_down, mix_norm, w_in,
              attn_sinks, kv_norm_c, w_kv_up_c, w_out, ffn2_norm, ffn2_w_gate, ffn2_w_up,
              ffn2_w_down, final_norm):
    B, T = x.shape[0], x.shape[1]
    tab = rel_bias_table.T
    tab_a = tab[:N_HEADS_A]
    tab_b = tab[N_HEADS_A:N_HEADS_A + N_HEADS_B]
    tab_c = tab[N_HEADS_A + N_HEADS_B:]
    for l in range(DEPTH):
        x = x + 0.5 * _swiglu(_rmsnorm(x, ffn1_norm[l]), ffn1_w_gate[l], ffn1_w_up[l], ffn1_w_down[l])
        h = _rmsnorm(x, mix_norm[l])
        (qa, ka, va, qb, kb, vb, qc, ckv, qidx, kidx, widx) = _split_cols(h @ w_in[l])
        out_a = _sliding_window_attn(qa.reshape(B, T, N_HEADS_A, HEAD_DIM),
                                     ka.reshape(B, T, N_KV_A, HEAD_DIM),
                                     va.reshape(B, T, N_KV_A, HEAD_DIM),
                                     attn_sinks[l], tab_a)
        out_b = _moba_attn(qb.reshape(B, T, N_HEADS_B, HEAD_DIM),
                           kb.reshape(B, T, N_HEADS_B, HEAD_DIM),
                           vb.reshape(B, T, N_HEADS_B, HEAD_DIM), tab_b)
        kv_c = _rmsnorm(ckv, kv_norm_c[l]) @ w_kv_up_c[l]
        out_c = _dsa_attn(qc.reshape(B, T, N_HEADS_C, HEAD_DIM),
                          kv_c[..., :HEAD_DIM], kv_c[..., HEAD_DIM:],
                          qidx.reshape(B, T, N_IDX_HEADS, IDX_DIM), kidx,
                          widx * (N_IDX_HEADS ** -0.5), tab_c)
        x = x + jnp.concatenate([out_a, out_b, out_c], axis=-1) @ w_out[l]
        x = x + 0.5 * _swiglu(_rmsnorm(x, ffn2_norm[l]), ffn2_w_gate[l], ffn2_w_up[l], ffn2_w_down[l])
    return _rmsnorm(x, final_norm)
```

```python
import functools
import math

import jax
import jax.numpy as jnp
import numpy as np
from jax import lax
from jax.experimental import pallas as pl
from jax.experimental.pallas import tpu as pltpu

D_MODEL = 1024
HEAD_DIM = 64
N_HEADS = 16
N_HEADS_A = 8
N_KV_A = 2
WINDOW = 128
N_HEADS_B = 4
MOBA_BLOCK = 256
MOBA_TOPK = 3
N_HEADS_C = 4
C_KV_LATENT = 128
N_IDX_HEADS = 4
IDX_DIM = 64
DSA_TOPK = 256
D_FF = 2816
N_BUCKETS = 32
MAX_DISTANCE = 128
RMS_EPS = 1e-6

A_Q = N_HEADS_A * HEAD_DIM
A_KV = N_KV_A * HEAD_DIM
B_QKV = N_HEADS_B * HEAD_DIM
C_Q = N_HEADS_C * HEAD_DIM
C_QIDX = N_IDX_HEADS * IDX_DIM

TQ = 256
LANES = 128
FFN_TM = 1024
FFN_TF = 256
PROJ_TM = 512
VMEM_LIMIT = 56 * 1024 * 1024

F32 = jnp.float32
BF16 = jnp.bfloat16
NEG_INF = float("-inf")

_NT = (((1,), (1,)), ((), ()))


def _cparams(n_axes):
    return pltpu.CompilerParams(dimension_semantics=("arbitrary",) * n_axes,
                                vmem_limit_bytes=VMEM_LIMIT)


def _dot(a, b):
    return jnp.dot(a, b, preferred_element_type=F32)


def _dot_nt(a, b):
    return lax.dot_general(a, b, _NT, preferred_element_type=F32)


def _rms(x, g):
    return x * lax.rsqrt(jnp.mean(x * x, axis=-1, keepdims=True) + RMS_EPS) * g


def _t5_bucket_np(dist):
    n = np.maximum(dist, 0)
    max_exact = N_BUCKETS // 2
    nf = np.maximum(n, 1).astype(np.float32)
    large = max_exact + (np.log(nf / np.float32(max_exact)) / np.float32(math.log(MAX_DISTANCE / max_exact))
                         * np.float32(N_BUCKETS - max_exact)).astype(np.int32)
    large = np.minimum(large, N_BUCKETS - 1)
    return np.where(n < max_exact, n, large).astype(np.int32)


def _bias_index_tiles():
    col = np.arange(TQ)[None, :]
    row = np.arange(TQ)[:, None]
    d_own = col - row
    d_prev = TQ + col - row
    d_prev_w = WINDOW + col - np.arange(WINDOW)[:, None]
    a = np.full((512, TQ), -1, np.int32)
    a[:TQ] = np.where((d_own >= 0) & (d_own < WINDOW), _t5_bucket_np(d_own), -1)
    a[TQ:TQ + WINDOW] = np.where((d_prev_w >= 0) & (d_prev_w < WINDOW), _t5_bucket_np(d_prev_w), -1)
    b = np.full((512, TQ), -1, np.int32)
    b[:TQ] = np.where(d_own >= 0, _t5_bucket_np(d_own), -1)
    b[TQ:] = _t5_bucket_np(d_prev)
    return np.stack([a, b])


def _bias_kernel(tab_ref, idx_ref, o_ref):
    h = pl.program_id(0)
    idx = idx_ref[0]
    out = jnp.full(idx.shape, NEG_INF, F32)
    for b in range(N_BUCKETS):
        out = jnp.where(idx == b, tab_ref[b, h], out)
    o_ref[0] = out


def _bias_tiles(rel_bias_table):
    idx = jnp.asarray(_bias_index_tiles())
    return pl.pallas_call(
        _bias_kernel,
        grid=(N_HEADS,),
        in_specs=[pl.BlockSpec(memory_space=pltpu.SMEM),
                  pl.BlockSpec((1, 512, TQ), lambda h: (h // N_HEADS_A, 0, 0))],
        out_specs=pl.BlockSpec((1, 512, TQ), lambda h: (h, 0, 0)),
        out_shape=jax.ShapeDtypeStruct((N_HEADS, 512, TQ), F32),
        compiler_params=_cparams(1),
        name="t5_bias_tiles",
    )(rel_bias_table, idx)


def _ffn_kernel(x_ref, g_ref, wg_ref, wu_ref, wd_ref, gf_ref, o_ref, h_ref, acc_ref, *, final_norm):
    k = pl.program_id(1)

    @pl.when(k == 0)
    def _():
        h_ref[...] = _rms(x_ref[...], g_ref[...]).astype(BF16)
        acc_ref[...] = jnp.zeros_like(acc_ref)

    h = h_ref[...]
    gate = _dot(h, wg_ref[...])
    up = _dot(h, wu_ref[...])
    act = (gate * jax.nn.sigmoid(gate) * up).astype(BF16)
    acc_ref[...] += _dot(act, wd_ref[...])

    @pl.when(k == pl.num_programs(1) - 1)
    def _():
        y = x_ref[...] + 0.5 * acc_ref[...]
        if final_norm:
            y = _rms(y, gf_ref[...])
        o_ref[...] = y


def _ffn(x, g, wg, wu, wd, gf, final_norm):
    n = x.shape[0]
    tm = min(FFN_TM, n)
    assert n % tm == 0 and D_FF % FFN_TF == 0
    return pl.pallas_call(
        functools.partial(_ffn_kernel, final_norm=final_norm),
        grid=(n // tm, D_FF // FFN_TF),
        in_specs=[pl.BlockSpec((tm, D_MODEL), lambda i, k: (i, 0)),
                  pl.BlockSpec((1, D_MODEL), lambda i, k: (0, 0)),
                  pl.BlockSpec((D_MODEL, FFN_TF), lambda i, k: (0, k)),
                  pl.BlockSpec((D_MODEL, FFN_TF), lambda i, k: (0, k)),
                  pl.BlockSpec((FFN_TF, D_MODEL), lambda i, k: (k, 0)),
                  pl.BlockSpec((1, D_MODEL), lambda i, k: (0, 0))],
        out_specs=pl.BlockSpec((tm, D_MODEL), lambda i, k: (i, 0)),
        out_shape=jax.ShapeDtypeStruct((n, D_MODEL), F32),
        scratch_shapes=[pltpu.VMEM((tm, D_MODEL), BF16), pltpu.VMEM((tm, D_MODEL), F32)],
        compiler_params=_cparams(2),
        name="swiglu_ffn",
    )(x, g, wg, wu, wd, gf)


_FM_QA, _FM_VA, _FM_QB, _FM_VB, _FM_QC, _FM_QI, _FM_W = 0, 512, 640, 896, 1152, 1408, 1664
_FM_ROWS = 1680
_TM_KA, _TM_KB, _TM_CKV, _TM_KI = 0, 256, 768, 896
_TM_COLS = 1024


def _proj_kernel(x_ref, g_ref, wfm_ref, wtm_ref, gc_ref, wkv_ref, wvT_ref,
                 qa_ref, va_ref, qb_ref, vb_ref, qc_ref, qi_ref, w_ref,
                 ka_ref, kb_ref, kmean_ref, kc_ref, vc_ref, ki_ref):
    tm = x_ref.shape[0]
    nsub = tm // TQ
    h = _rms(x_ref[...], g_ref[...]).astype(BF16)

    def fm(lo, hi, scale=None):
        y = _dot_nt(wfm_ref[lo:hi, :], h)
        return y if scale is None else y * scale

    def put_fm(ref, y):
        for r in range(nsub):
            ref[0, r] = y[:, r * TQ:(r + 1) * TQ].astype(ref.dtype)

    qk_scale = HEAD_DIM ** -0.5
    put_fm(qa_ref, fm(_FM_QA, _FM_VA, qk_scale))
    put_fm(va_ref, fm(_FM_VA, _FM_QB))
    put_fm(qb_ref, fm(_FM_QB, _FM_VB, qk_scale))
    put_fm(vb_ref, fm(_FM_VB, _FM_QC))
    put_fm(qc_ref, fm(_FM_QC, _FM_QI, qk_scale))
    put_fm(qi_ref, fm(_FM_QI, _FM_W))
    put_fm(w_ref, fm(_FM_W, _FM_ROWS, (N_IDX_HEADS ** -0.5) * (IDX_DIM ** -0.5)))

    ptm = _dot(h, wtm_ref[...])
    for kv in range(N_KV_A):
        lo = _TM_KA + kv * LANES
        ka_ref[0, kv] = ptm[:, lo:lo + HEAD_DIM].astype(BF16)
    for hb in range(N_HEADS_B):
        lo = _TM_KB + hb * LANES
        kb_ref[0, hb] = ptm[:, lo:lo + HEAD_DIM].astype(BF16)
    for r in range(nsub):
        kmean_ref[0, r] = jnp.mean(ptm[r * TQ:(r + 1) * TQ, _TM_KB:_TM_CKV], axis=0, keepdims=True)
    ki_ref[0] = ptm[:, _TM_KI:_TM_KI + IDX_DIM].astype(BF16)

    ckv = _rms(ptm[:, _TM_CKV:_TM_KI], gc_ref[...]).astype(BF16)
    kc_ref[0] = _dot(ckv, wkv_ref[...])[:, :HEAD_DIM].astype(BF16)
    put_fm(vc_ref, _dot_nt(wvT_ref[...], ckv))


def _proj(x, g, wfm, wtm, gc, wkv, wvT, batch, seq):
    tm = min(PROJ_TM, seq)
    assert seq % tm == 0 and tm % TQ == 0
    tpb = seq // tm
    nsub = tm // TQ
    nb = seq // TQ

    def fm_spec(rows):
        return pl.BlockSpec((1, nsub, rows, TQ), lambda i: (i // tpb, i % tpb, 0, 0))

    def fm_shape(rows, dtype=BF16):
        return jax.ShapeDtypeStruct((batch, nb, rows, TQ), dtype)

    def full(a):
        return pl.BlockSpec(a.shape, lambda i: (0,) * a.ndim)

    out_specs = [fm_spec(A_Q), fm_spec(A_KV), fm_spec(B_QKV), fm_spec(B_QKV), fm_spec(C_Q), fm_spec(C_QIDX),
                 fm_spec(16),
                 pl.BlockSpec((1, N_KV_A, tm, HEAD_DIM), lambda i: (i // tpb, 0, i % tpb, 0)),
                 pl.BlockSpec((1, N_HEADS_B, tm, HEAD_DIM), lambda i: (i // tpb, 0, i % tpb, 0)),
                 pl.BlockSpec((1, nsub, 1, N_HEADS_B * LANES), lambda i: (i // tpb, i % tpb, 0, 0)),
                 pl.BlockSpec((1, tm, HEAD_DIM), lambda i: (i // tpb, i % tpb, 0)),
                 fm_spec(HEAD_DIM),
                 pl.BlockSpec((1, tm, IDX_DIM), lambda i: (i // tpb, i % tpb, 0))]
    out_shape = [fm_shape(A_Q), fm_shape(A_KV), fm_shape(B_QKV), fm_shape(B_QKV), fm_shape(C_Q), fm_shape(C_QIDX),
                 fm_shape(16, F32),
                 jax.ShapeDtypeStruct((batch, N_KV_A, seq, HEAD_DIM), BF16),
                 jax.ShapeDtypeStruct((batch, N_HEADS_B, seq, HEAD_DIM), BF16),
                 jax.ShapeDtypeStruct((batch, nb, 1, N_HEADS_B * LANES), F32),
                 jax.ShapeDtypeStruct((batch, seq, HEAD_DIM), BF16),
                 fm_shape(HEAD_DIM),
                 jax.ShapeDtypeStruct((batch, seq, IDX_DIM), BF16)]
    return pl.pallas_call(
        _proj_kernel,
        grid=(batch * tpb,),
        in_specs=[pl.BlockSpec((tm, D_MODEL), lambda i: (i, 0)),
                  full(g), full(wfm), full(wtm), full(gc), full(wkv), full(wvT)],
        out_specs=out_specs,
        out_shape=out_shape,
        compiler_params=_cparams(1),
        name="mix_in_proj",
    )(x, g, wfm, wtm, gc, wkv, wvT)


def _online_update(s, vT, m_ref, l_ref, acc_ref, hh):
    m_prev = m_ref[hh]
    m_new = jnp.maximum(m_prev, jnp.max(s, axis=0, keepdims=True))
    m_safe = jnp.where(m_new == NEG_INF, 0.0, m_new)
    p = jnp.exp(s - m_safe)
    alpha = jnp.exp(m_prev - m_safe)
    l_ref[hh] = alpha * l_ref[hh] + jnp.sum(p, axis=0, keepdims=True)
    acc_ref[hh] = alpha * acc_ref[hh] + _dot(vT, p.astype(BF16))
    m_ref[hh] = m_new


def _init_stats(m_ref, l_ref, acc_ref):
    m_ref[...] = jnp.full(m_ref.shape, NEG_INF, F32)
    l_ref[...] = jnp.zeros(l_ref.shape, F32)
    acc_ref[...] = jnp.zeros(acc_ref.shape, F32)


def _finish(l_ref, acc_ref, n_heads):
    oT = jnp.concatenate([acc_ref[hh] * (1.0 / l_ref[hh]) for hh in range(n_heads)], axis=0)
    return oT.T


def _swa_kernel(sink_ref, q_ref, k_ref, v_ref, bias_ref, o_ref):
    g = pl.program_id(1)
    i = pl.program_id(2)
    g_heads = N_HEADS_A // N_KV_A
    start = pl.multiple_of(i * TQ, TQ)
    prev_start = pl.multiple_of(jnp.maximum(i * TQ - WINDOW, 0), WINDOW)
    k_main = k_ref[0, 0, pl.ds(start, TQ), :]
    k_prev = k_ref[0, 0, pl.ds(prev_start, WINDOW), :]
    v_main = v_ref[0, i]
    v_prev = v_ref[0, jnp.maximum(i - 1, 0)][:, TQ - WINDOW:]
    prev_off = jnp.where(i > 0, 0.0, NEG_INF)
    outs = []
    for hh in range(g_heads):
        qT = q_ref[0, 0, hh * HEAD_DIM:(hh + 1) * HEAD_DIM, :]
        s_main = _dot(k_main, qT) + bias_ref[hh, 0:TQ, :]
        s_prev = _dot(k_prev, qT) + bias_ref[hh, TQ:TQ + WINDOW, :] + prev_off
        sink = sink_ref[g * g_heads + hh]
        m = jnp.maximum(jnp.maximum(jnp.max(s_main, axis=0, keepdims=True),
                                    jnp.max(s_prev, axis=0, keepdims=True)), sink)
        p_main = jnp.exp(s_main - m)
        p_prev = jnp.exp(s_prev - m)
        denom = (jnp.sum(p_main, axis=0, keepdims=True) + jnp.sum(p_prev, axis=0, keepdims=True)
                 + jnp.exp(sink - m))
        oT = _dot(v_main, p_main.astype(BF16)) + _dot(v_prev, p_prev.astype(BF16))
        outs.append(oT * (1.0 / denom))
    o_ref[0] = jnp.concatenate(outs, axis=0).T.astype(o_ref.dtype)


def _swa(sinks, qaT, ka, vaT, bias, batch, seq):
    nb = seq // TQ
    g_heads = N_HEADS_A // N_KV_A
    return pl.pallas_call(
        _swa_kernel,
        grid=(batch, N_KV_A, nb),
        in_specs=[pl.BlockSpec(memory_space=pltpu.SMEM),
                  pl.BlockSpec((1, 1, g_heads * HEAD_DIM, TQ), lambda b, g, i: (b, i, g, 0)),
                  pl.BlockSpec((1, 1, seq, HEAD_DIM), lambda b, g, i: (b, g, 0, 0)),
                  pl.BlockSpec((1, nb, HEAD_DIM, TQ), lambda b, g, i: (b, 0, g, 0)),
                  pl.BlockSpec((g_heads, 512, TQ), lambda b, g, i: (g, 0, 0))],
        out_specs=pl.BlockSpec((1, TQ, g_heads * HEAD_DIM), lambda b, g, i: (b, i, g)),
        out_shape=jax.ShapeDtypeStruct((batch, seq, A_Q), BF16),
        compiler_params=_cparams(3),
        name="swa_attn",
    )(sinks, qaT, ka, vaT, bias)


_MOBA_HPS = 2


def _moba_kernel(tab_ref, q_ref, k_ref, v_ref, kmean_ref, bias_ref, o_ref,
                 sel_ref, m_ref, l_ref, acc_ref, *, n_sel):
    hp = pl.program_id(1)
    i = pl.program_id(2)
    nb = kmean_ref.shape[2]
    _init_stats(m_ref, l_ref, acc_ref)
    row = lax.broadcasted_iota(jnp.int32, (nb, TQ), 0)

    for hh in range(_MOBA_HPS):
        head = N_HEADS_A + hp * _MOBA_HPS + hh
        c_far = tab_ref[N_BUCKETS - 1, head]
        qT = q_ref[0, 0, hh * HEAD_DIM:(hh + 1) * HEAD_DIM, :]

        gate = _dot(kmean_ref[0, hh][:, :HEAD_DIM], qT.astype(F32))
        gate = jnp.where(row < i, gate, NEG_INF)
        sel = jnp.zeros((nb, TQ), jnp.bool_)
        for _ in range(n_sel):
            best = jnp.max(gate, axis=0, keepdims=True)
            first = jnp.min(jnp.where(gate == best, row, nb), axis=0, keepdims=True)
            pick = (row == first) & (best > NEG_INF)
            sel = sel | pick
            gate = jnp.where(pick, NEG_INF, gate)
        sel_ref[hh] = jnp.where(sel, 0.0, NEG_INF)

        def k_blk(j):
            return k_ref[0, hh, pl.ds(pl.multiple_of(j * TQ, TQ), TQ), :]

        def v_blk(j):
            return v_ref[0, j, hh * HEAD_DIM:(hh + 1) * HEAD_DIM, :]

        def far_body(j, carry):
            s = _dot(k_blk(j), qT) + (sel_ref[hh, pl.ds(j, 1), :] + c_far)
            _online_update(s, v_blk(j), m_ref, l_ref, acc_ref, hh)
            return carry

        lax.fori_loop(0, i - 1, far_body, 0)

        @pl.when(i > 0)
        def _():
            j = i - 1
            s = _dot(k_blk(j), qT) + bias_ref[hh, TQ:2 * TQ, :] + sel_ref[hh, pl.ds(j, 1), :]
            _online_update(s, v_blk(j), m_ref, l_ref, acc_ref, hh)

        s = _dot(k_blk(i), qT) + bias_ref[hh, 0:TQ, :]
        _online_update(s, v_blk(i), m_ref, l_ref, acc_ref, hh)

    o_ref[0] = _finish(l_ref, acc_ref, _MOBA_HPS).astype(o_ref.dtype)


def _moba(tab, qbT, kb, vbT, kmean, bias, batch, seq):
    nb = seq // TQ
    hps = _MOBA_HPS
    n_sel = min(MOBA_TOPK, nb - 1)
    bias_blk0 = N_HEADS_A // hps
    return pl.pallas_call(
        functools.partial(_moba_kernel, n_sel=n_sel),
        grid=(batch, N_HEADS_B // hps, nb),
        in_specs=[pl.BlockSpec(memory_space=pltpu.SMEM),
                  pl.BlockSpec((1, 1, hps * HEAD_DIM, TQ), lambda b, hp, i: (b, i, hp, 0)),
                  pl.BlockSpec((1, hps, seq, HEAD_DIM), lambda b, hp, i: (b, hp, 0, 0)),
                  pl.BlockSpec((1, nb, hps * HEAD_DIM, TQ), lambda b, hp, i: (b, 0, hp, 0)),
                  pl.BlockSpec((1, hps, nb, LANES), lambda b, hp, i: (b, hp, 0, 0)),
                  pl.BlockSpec((hps, 512, TQ), lambda b, hp, i: (bias_blk0 + hp, 0, 0))],
        out_specs=pl.BlockSpec((1, TQ, hps * HEAD_DIM), lambda b, hp, i: (b, i, hp)),
        out_shape=jax.ShapeDtypeStruct((batch, seq, B_QKV), BF16),
        scratch_shapes=[pltpu.VMEM((hps, nb, TQ), F32),
                        pltpu.VMEM((hps, 1, TQ), F32),
                        pltpu.VMEM((hps, 1, TQ), F32),
                        pltpu.VMEM((hps, HEAD_DIM, TQ), F32)],
        compiler_params=_cparams(3),
        name="moba_attn",
    )(tab, qbT, kb, vbT, kmean, bias)


_SIGN = -2 ** 31
_NEG_INF_ORD = 0x007FFFFF


def _ord_to_f32(u):
    skey = u ^ jnp.int32(_SIGN)
    bits = skey ^ ((skey >> 31) & jnp.int32(0x7FFFFFFF))
    return lax.bitcast_convert_type(bits, F32)


def _dsa_kernel(tab_ref, qi_ref, w_ref, ki_ref, q_ref, k_ref, v_ref, bias_ref, tri_ref, o_ref,
                score_ref, m_ref, l_ref, acc_ref, ties_ref, *, n_keep):
    i = pl.program_id(1)
    head0 = N_HEADS_A + N_HEADS_B
    nblk = i + 1

    def blk_start(j):
        return pl.multiple_of(j * TQ, TQ)

    w = w_ref[0, 0]
    krow = lax.broadcasted_iota(jnp.int32, (TQ, TQ), 0)
    qcol = lax.broadcasted_iota(jnp.int32, (TQ, TQ), 1)

    def score_body(j, carry):
        ki = ki_ref[0, pl.ds(blk_start(j), TQ), :]
        sc = jnp.zeros((TQ, TQ), F32)
        for hi in range(N_IDX_HEADS):
            d = _dot(ki, qi_ref[0, 0, hi * IDX_DIM:(hi + 1) * IDX_DIM, :])
            sc = sc + jnp.maximum(d, 0.0) * w[hi:hi + 1, :]
        sc = jnp.where(krow + j * TQ <= qcol + i * TQ, sc, NEG_INF)
        score_ref[pl.ds(blk_start(j), TQ), :] = sc
        return carry

    lax.fori_loop(0, nblk, score_body, 0)

    def count_ge(cand):
        def body(j, cnt):
            blk = score_ref[pl.ds(blk_start(j), TQ), :]
            return cnt + jnp.sum(jnp.where(blk >= cand, 1.0, 0.0), axis=0, keepdims=True)
        return lax.fori_loop(0, nblk, body, jnp.zeros((1, TQ), F32))

    def bit_body(b, u):
        u_try = u | lax.shift_left(jnp.int32(1), 31 - b)
        return jnp.where(count_ge(_ord_to_f32(u_try)) >= n_keep, u_try, u)

    u = lax.fori_loop(0, 32, bit_body, jnp.zeros((1, TQ), jnp.int32))
    u = jnp.where((u >= 0) & (u < _NEG_INF_ORD), jnp.int32(_NEG_INF_ORD), u)
    thr = _ord_to_f32(u)
    thr_next = _ord_to_f32(u + 1)
    tie_budget = n_keep - count_ge(thr_next)

    _init_stats(m_ref, l_ref, acc_ref)
    ties_ref[...] = jnp.zeros(ties_ref.shape, F32)

    def attend(j, kind):
        blk = score_ref[pl.ds(blk_start(j), TQ), :]
        above = blk >= thr_next
        tie = (blk >= thr) & jnp.logical_not(above)
        tie_f = jnp.where(tie, 1.0, 0.0)
        before = ties_ref[...] + _dot(tri_ref[...], tie_f.astype(BF16))
        keep = above | (tie & (before < tie_budget))
        ties_ref[...] += jnp.sum(tie_f, axis=0, keepdims=True)
        kc = k_ref[0, pl.ds(blk_start(j), TQ), :]
        vT = v_ref[0, j]
        for hh in range(N_HEADS_C):
            s = _dot(kc, q_ref[0, 0, hh * HEAD_DIM:(hh + 1) * HEAD_DIM, :])
            if kind == "far":
                s = s + tab_ref[N_BUCKETS - 1, head0 + hh]
            elif kind == "prev":
                s = s + bias_ref[hh, TQ:2 * TQ, :]
            else:
                s = s + bias_ref[hh, 0:TQ, :]
            _online_update(jnp.where(keep, s, NEG_INF), vT, m_ref, l_ref, acc_ref, hh)

    def far_body(j, carry):
        attend(j, "far")
        return carry

    lax.fori_loop(0, i - 1, far_body, 0)

    @pl.when(i > 0)
    def _():
        attend(i - 1, "prev")

    attend(i, "own")
    o_ref[0] = _finish(l_ref, acc_ref, N_HEADS_C).astype(o_ref.dtype)


def _dsa(tab, qiT, wT, ki, qcT, kc, vcT, bias, batch, seq):
    nb = seq // TQ
    n_keep = min(DSA_TOPK, seq // 4)
    tri = jnp.asarray(np.tril(np.ones((TQ, TQ), np.float32), -1), BF16)
    bias_blk = (N_HEADS_A + N_HEADS_B) // N_HEADS_C
    return pl.pallas_call(
        functools.partial(_dsa_kernel, n_keep=n_keep),
        grid=(batch, nb),
        in_specs=[pl.BlockSpec(memory_space=pltpu.SMEM),
                  pl.BlockSpec((1, 1, C_QIDX, TQ), lambda b, i: (b, i, 0, 0)),
                  pl.BlockSpec((1, 1, 16, TQ), lambda b, i: (b, i, 0, 0)),
                  pl.BlockSpec((1, seq, IDX_DIM), lambda b, i: (b, 0, 0)),
                  pl.BlockSpec((1, 1, C_Q, TQ), lambda b, i: (b, i, 0, 0)),
                  pl.BlockSpec((1, seq, HEAD_DIM), lambda b, i: (b, 0, 0)),
                  pl.BlockSpec((1, nb, HEAD_DIM, TQ), lambda b, i: (b, 0, 0, 0)),
                  pl.BlockSpec((N_HEADS_C, 512, TQ), lambda b, i: (bias_blk, 0, 0)),
                  pl.BlockSpec((TQ, TQ), lambda b, i: (0, 0))],
        out_specs=pl.BlockSpec((1, TQ, C_Q), lambda b, i: (b, i, 0)),
        out_shape=jax.ShapeDtypeStruct((batch, seq, C_Q), BF16),
        scratch_shapes=[pltpu.VMEM((seq, TQ), F32),
                        pltpu.VMEM((N_HEADS_C, 1, TQ), F32),
                        pltpu.VMEM((N_HEADS_C, 1, TQ), F32),
                        pltpu.VMEM((N_HEADS_C, HEAD_DIM, TQ), F32),
                        pltpu.VMEM((1, TQ), F32)],
        compiler_params=_cparams(2),
        name="dsa_attn",
    )(tab, qiT, wT, ki, qcT, kc, vcT, bias, tri)


def _oproj_kernel(x_ref, a_ref, b_ref, c_ref, wa_ref, wb_ref, wc_ref, o_ref):
    o_ref[...] = (x_ref[...] + _dot(a_ref[...], wa_ref[...]) + _dot(b_ref[...], wb_ref[...])
                  + _dot(c_ref[...], wc_ref[...]))


def _oproj(x, a, b, c, wa, wb, wc):
    n = x.shape[0]
    tm = min(PROJ_TM, n)
    assert n % tm == 0

    def rows(cols):
        return pl.BlockSpec((tm, cols), lambda i: (i, 0))

    def full(w):
        return pl.BlockSpec(w.shape, lambda i: (0, 0))

    return pl.pallas_call(
        _oproj_kernel,
        grid=(n // tm,),
        in_specs=[rows(D_MODEL), rows(A_Q), rows(B_QKV), rows(C_Q), full(wa), full(wb), full(wc)],
        out_specs=rows(D_MODEL),
        out_shape=jax.ShapeDtypeStruct((n, D_MODEL), F32),
        compiler_params=_cparams(1),
        name="mix_out_proj",
    )(x, a, b, c, wa, wb, wc)


def _pad_heads(w, n_heads):
    w = w.reshape(D_MODEL, n_heads, HEAD_DIM)
    return jnp.pad(w, ((0, 0), (0, 0), (0, LANES - HEAD_DIM))).reshape(D_MODEL, n_heads * LANES)


def _split_w_in(w_in):
    cuts = np.cumsum([A_Q, A_KV, A_KV, B_QKV, B_QKV, B_QKV, C_Q, C_KV_LATENT, C_QIDX, IDX_DIM])
    qa, ka, va, qb, kb, vb, qc, ckv, qidx, kidx, widx = jnp.split(w_in, cuts, axis=1)
    wfm = jnp.concatenate([qa, va, qb, vb, qc, qidx, jnp.pad(widx, ((0, 0), (0, 16 - N_IDX_HEADS)))], axis=1)
    wtm = jnp.concatenate([_pad_heads(ka, N_KV_A), _pad_heads(kb, N_HEADS_B), ckv, _pad_heads(kidx, 1)], axis=1)
    assert wfm.shape[1] == _FM_ROWS and wtm.shape[1] == _TM_COLS
    return wfm.T.astype(BF16), wtm.astype(BF16)


def kernel(x, rel_bias_table, ffn1_norm, ffn1_w_gate, ffn1_w_up, ffn1_w_down, mix_norm, w_in, attn_sinks, kv_norm_c, w_kv_up_c, w_out, ffn2_norm, ffn2_w_gate, ffn2_w_up, ffn2_w_down, final_norm):
    batch, seq = x.shape[0], x.shape[1]
    depth = w_in.shape[0]
    assert seq % TQ == 0 and x.shape[2] == D_MODEL
    nb = seq // TQ
    bias = _bias_tiles(rel_bias_table)
    gf = final_norm.reshape(1, D_MODEL)
    xf = x.reshape(batch * seq, D_MODEL)
    for l in range(depth):
        xf = _ffn(xf, ffn1_norm[l].reshape(1, D_MODEL), ffn1_w_gate[l].astype(BF16), ffn1_w_up[l].astype(BF16),
                  ffn1_w_down[l].astype(BF16), gf, False)
        wfm, wtm = _split_w_in(w_in[l])
        wkv = w_kv_up_c[l].astype(BF16)
        (qaT, vaT, qbT, vbT, qcT, qiT, wT, ka, kb, kmean, kc, vcT, ki) = _proj(
            xf, mix_norm[l].reshape(1, D_MODEL), wfm, wtm, kv_norm_c[l].reshape(1, C_KV_LATENT),
            wkv, wkv[:, HEAD_DIM:].T, batch, seq)
        kmean = kmean.reshape(batch, nb, N_HEADS_B, LANES).transpose(0, 2, 1, 3)
        out_a = _swa(attn_sinks[l], qaT, ka, vaT, bias, batch, seq)
        out_b = _moba(rel_bias_table, qbT, kb, vbT, kmean, bias, batch, seq)
        out_c = _dsa(rel_bias_table, qiT, wT, ki, qcT, kc, vcT, bias, batch, seq)
        wo = w_out[l].astype(BF16)
        xf = _oproj(xf, out_a.reshape(batch * seq, A_Q), out_b.reshape(batch * seq, B_QKV),
                    out_c.reshape(batch * seq, C_Q), wo[:A_Q], wo[A_Q:A_Q + B_QKV], wo[A_Q + B_QKV:])
        xf = _ffn(xf, ffn2_norm[l].reshape(1, D_MODEL), ffn2_w_gate[l].astype(BF16), ffn2_w_up[l].astype(BF16),
                  ffn2_w_down[l].astype(BF16), gf, l == depth - 1)
    return xf.reshape(batch, seq, D_MODEL)
```

```python
import functools
import math

import jax
import jax.numpy as jnp
import numpy as np
from jax import lax
from jax.experimental import pallas as pl
from jax.experimental.pallas import tpu as pltpu

D_MODEL = 1024
HEAD_DIM = 64
N_HEADS = 16
N_HEADS_A = 8
N_KV_A = 2
WINDOW = 128
N_HEADS_B = 4
MOBA_BLOCK = 256
MOBA_TOPK = 3
N_HEADS_C = 4
C_KV_LATENT = 128
N_IDX_HEADS = 4
IDX_DIM = 64
DSA_TOPK = 256
D_FF = 2816
N_BUCKETS = 32
MAX_DISTANCE = 128
RMS_EPS = 1e-6

A_Q = N_HEADS_A * HEAD_DIM
A_KV = N_KV_A * HEAD_DIM
B_QKV = N_HEADS_B * HEAD_DIM
C_Q = N_HEADS_C * HEAD_DIM
C_QIDX = N_IDX_HEADS * IDX_DIM

TQ = 256
GB = 4
GK = GB * TQ
SUBLANES = 8
LANES = 128
FFN_TM = 1024
FFN_TF = 256
PROJ_TM = GK
VMEM_LIMIT = 56 * 1024 * 1024

F32 = jnp.float32
BF16 = jnp.bfloat16
NEG_INF = float("-inf")
LOG2E = math.log2(math.e)
QK_SCALE = HEAD_DIM ** -0.5 * LOG2E

BIAS_ROWS = 4 * TQ

_NT = (((1,), (1,)), ((), ()))


def _cparams(n_axes):
    return pltpu.CompilerParams(dimension_semantics=("arbitrary",) * n_axes,
                                vmem_limit_bytes=VMEM_LIMIT)


def _dot(a, b):
    return jnp.dot(a, b, preferred_element_type=F32)


def _dot_nt(a, b):
    return lax.dot_general(a, b, _NT, preferred_element_type=F32)


def _rms(x, g):
    return x * lax.rsqrt(jnp.mean(x * x, axis=-1, keepdims=True) + RMS_EPS) * g


def _t5_bucket_np(dist):
    n = np.maximum(dist, 0)
    max_exact = N_BUCKETS // 2
    nf = np.maximum(n, 1).astype(np.float32)
    large = max_exact + (np.log(nf / np.float32(max_exact)) / np.float32(math.log(MAX_DISTANCE / max_exact))
                         * np.float32(N_BUCKETS - max_exact)).astype(np.int32)
    large = np.minimum(large, N_BUCKETS - 1)
    return np.where(n < max_exact, n, large).astype(np.int32)


_IDX_MASKED = -1
_IDX_ZERO = -2


def _bias_index_tiles():
    col = np.arange(TQ)[None, :]
    row = np.arange(TQ)[:, None]
    d_own = col - row
    d_prev = TQ + col - row
    d_prev_w = WINDOW + col - np.arange(WINDOW)[:, None]
    a = np.full((BIAS_ROWS, TQ), _IDX_MASKED, np.int32)
    a[:TQ] = np.where((d_own >= 0) & (d_own < WINDOW), _t5_bucket_np(d_own), _IDX_MASKED)
    a[TQ:TQ + WINDOW] = np.where((d_prev_w >= 0) & (d_prev_w < WINDOW), _t5_bucket_np(d_prev_w), _IDX_MASKED)
    b = np.full((BIAS_ROWS, TQ), _IDX_MASKED, np.int32)
    b[:TQ] = _IDX_ZERO
    b[TQ:2 * TQ] = _t5_bucket_np(d_prev)
    b[2 * TQ:3 * TQ] = np.where(d_own >= 0, _t5_bucket_np(d_own), _IDX_MASKED)
    assert (_t5_bucket_np(np.arange(2 * TQ, 64 * TQ)) == N_BUCKETS - 1).all()
    return np.stack([a, b])


def _bias_kernel(tab_ref, idx_ref, o_ref):
    h = pl.program_id(0)
    idx = idx_ref[0]
    shift = jnp.where(h >= N_HEADS_A, tab_ref[N_BUCKETS - 1, h], 0.0)
    out = jnp.where(idx == _IDX_ZERO, 0.0, NEG_INF)
    for b in range(N_BUCKETS):
        out = jnp.where(idx == b, (tab_ref[b, h] - shift) * LOG2E, out)
    o_ref[0] = out


def _bias_tiles(rel_bias_table):
    idx = jnp.asarray(_bias_index_tiles())
    return pl.pallas_call(
        _bias_kernel,
        grid=(N_HEADS,),
        in_specs=[pl.BlockSpec(memory_space=pltpu.SMEM),
                  pl.BlockSpec((1, BIAS_ROWS, TQ), lambda h: (h // N_HEADS_A, 0, 0))],
        out_specs=pl.BlockSpec((1, BIAS_ROWS, TQ), lambda h: (h, 0, 0)),
        out_shape=jax.ShapeDtypeStruct((N_HEADS, BIAS_ROWS, TQ), F32),
        compiler_params=_cparams(1),
        name="t5_bias_tiles",
    )(rel_bias_table, idx)


def _ffn_kernel(x_ref, g_ref, wg_ref, wu_ref, wd_ref, gf_ref, o_ref, h_ref, acc_ref, *, final_norm):
    k = pl.program_id(1)

    @pl.when(k == 0)
    def _():
        h_ref[...] = _rms(x_ref[...], g_ref[...]).astype(BF16)
        acc_ref[...] = jnp.zeros_like(acc_ref)

    h = h_ref[...]
    gate = _dot(h, wg_ref[...])
    up = _dot(h, wu_ref[...])
    act = (gate * jax.nn.sigmoid(gate) * up).astype(BF16)
    acc_ref[...] += _dot(act, wd_ref[...])

    @pl.when(k == pl.num_programs(1) - 1)
    def _():
        y = x_ref[...] + 0.5 * acc_ref[...]
        if final_norm:
            y = _rms(y, gf_ref[...])
        o_ref[...] = y


def _ffn(x, g, wg, wu, wd, gf, final_norm):
    n = x.shape[0]
    tm = min(FFN_TM, n)
    assert n % tm == 0 and D_FF % FFN_TF == 0
    return pl.pallas_call(
        functools.partial(_ffn_kernel, final_norm=final_norm),
        grid=(n // tm, D_FF // FFN_TF),
        in_specs=[pl.BlockSpec((tm, D_MODEL), lambda i, k: (i, 0)),
                  pl.BlockSpec((1, D_MODEL), lambda i, k: (0, 0)),
                  pl.BlockSpec((D_MODEL, FFN_TF), lambda i, k: (0, k)),
                  pl.BlockSpec((D_MODEL, FFN_TF), lambda i, k: (0, k)),
                  pl.BlockSpec((FFN_TF, D_MODEL), lambda i, k: (k, 0)),
                  pl.BlockSpec((1, D_MODEL), lambda i, k: (0, 0))],
        out_specs=pl.BlockSpec((tm, D_MODEL), lambda i, k: (i, 0)),
        out_shape=jax.ShapeDtypeStruct((n, D_MODEL), F32),
        scratch_shapes=[pltpu.VMEM((tm, D_MODEL), BF16), pltpu.VMEM((tm, D_MODEL), F32)],
        compiler_params=_cparams(2),
        name="swiglu_ffn",
    )(x, g, wg, wu, wd, gf)


_FM_QA, _FM_VA, _FM_QB, _FM_VB, _FM_QC, _FM_QI, _FM_W = 0, 512, 640, 896, 1152, 1408, 1664
_FM_ROWS = 1680
_TM_KA, _TM_KB, _TM_CKV, _TM_KI = 0, 256, 768, 896
_TM_COLS = 1024


def _proj_kernel(x_ref, g_ref, wfm_ref, wtm_ref, gc_ref, wkv_ref, wvT_ref,
                 qa_ref, va_ref, qb_ref, vb_ref, qc_ref, qi_ref, w_ref,
                 ka_ref, kb_ref, kmean_ref, kc_ref, vc_ref, ki_ref):
    tm = x_ref.shape[0]
    nsub = tm // TQ
    h = _rms(x_ref[...], g_ref[...]).astype(BF16)

    def fm(lo, hi, scale=None):
        y = _dot_nt(wfm_ref[lo:hi, :], h)
        return y if scale is None else y * scale

    def put_fm(ref, y):
        for r in range(nsub):
            ref[0, r] = y[:, r * TQ:(r + 1) * TQ].astype(ref.dtype)

    put_fm(qa_ref, fm(_FM_QA, _FM_VA, QK_SCALE))
    put_fm(va_ref, fm(_FM_VA, _FM_QB))
    put_fm(qb_ref, fm(_FM_QB, _FM_VB, QK_SCALE))
    vb_ref[0, 0] = fm(_FM_VB, _FM_QC).astype(BF16)
    put_fm(qc_ref, fm(_FM_QC, _FM_QI, QK_SCALE))
    put_fm(qi_ref, fm(_FM_QI, _FM_W))
    put_fm(w_ref, fm(_FM_W, _FM_ROWS, (N_IDX_HEADS ** -0.5) * (IDX_DIM ** -0.5)))

    ptm = _dot(h, wtm_ref[...])
    for kv in range(N_KV_A):
        lo = _TM_KA + kv * LANES
        ka_ref[0, kv] = ptm[:, lo:lo + HEAD_DIM].astype(BF16)
    for hb in range(N_HEADS_B):
        lo = _TM_KB + hb * LANES
        kb_ref[0, hb] = ptm[:, lo:lo + HEAD_DIM].astype(BF16)
    for r in range(nsub):
        kmean_ref[0, r] = jnp.mean(ptm[r * TQ:(r + 1) * TQ, _TM_KB:_TM_CKV], axis=0, keepdims=True)
    ki_ref[0] = ptm[:, _TM_KI:_TM_KI + IDX_DIM].astype(BF16)

    ckv = _rms(ptm[:, _TM_CKV:_TM_KI], gc_ref[...]).astype(BF16)
    kc_ref[0] = _dot(ckv, wkv_ref[...])[:, :HEAD_DIM].astype(BF16)
    vc_ref[0, 0] = _dot_nt(wvT_ref[...], ckv).astype(BF16)


def _proj(x, g, wfm, wtm, gc, wkv, wvT, batch, seq):
    tm = PROJ_TM
    assert seq % tm == 0 and tm == GK
    tpb = seq // tm
    nsub = tm // TQ
    nb = seq // TQ

    def fm_spec(rows):
        return pl.BlockSpec((1, nsub, rows, TQ), lambda i: (i // tpb, i % tpb, 0, 0))

    def fm_shape(rows, dtype=BF16):
        return jax.ShapeDtypeStruct((batch, nb, rows, TQ), dtype)

    def grp_spec(rows):
        return pl.BlockSpec((1, 1, rows, GK), lambda i: (i // tpb, i % tpb, 0, 0))

    def grp_shape(rows):
        return jax.ShapeDtypeStruct((batch, tpb, rows, GK), BF16)

    def full(a):
        return pl.BlockSpec(a.shape, lambda i: (0,) * a.ndim)

    out_specs = [fm_spec(A_Q), fm_spec(A_KV), fm_spec(B_QKV), grp_spec(B_QKV), fm_spec(C_Q), fm_spec(C_QIDX),
                 fm_spec(16),
                 pl.BlockSpec((1, N_KV_A, tm, HEAD_DIM), lambda i: (i // tpb, 0, i % tpb, 0)),
                 pl.BlockSpec((1, N_HEADS_B, tm, HEAD_DIM), lambda i: (i // tpb, 0, i % tpb, 0)),
                 pl.BlockSpec((1, nsub, 1, N_HEADS_B * LANES), lambda i: (i // tpb, i % tpb, 0, 0)),
                 pl.BlockSpec((1, tm, HEAD_DIM), lambda i: (i // tpb, i % tpb, 0)),
                 grp_spec(HEAD_DIM),
                 pl.BlockSpec((1, tm, IDX_DIM), lambda i: (i // tpb, i % tpb, 0))]
    out_shape = [fm_shape(A_Q), fm_shape(A_KV), fm_shape(B_QKV), grp_shape(B_QKV), fm_shape(C_Q), fm_shape(C_QIDX),
                 fm_shape(16, F32),
                 jax.ShapeDtypeStruct((batch, N_KV_A, seq, HEAD_DIM), BF16),
                 jax.ShapeDtypeStruct((batch, N_HEADS_B, seq, HEAD_DIM), BF16),
                 jax.ShapeDtypeStruct((batch, nb, 1, N_HEADS_B * LANES), F32),
                 jax.ShapeDtypeStruct((batch, seq, HEAD_DIM), BF16),
                 grp_shape(HEAD_DIM),
                 jax.ShapeDtypeStruct((batch, seq, IDX_DIM), BF16)]
    return pl.pallas_call(
        _proj_kernel,
        grid=(batch * tpb,),
        in_specs=[pl.BlockSpec((tm, D_MODEL), lambda i: (i, 0)),
                  full(g), full(wfm), full(wtm), full(gc), full(wkv), full(wvT)],
        out_specs=out_specs,
        out_shape=out_shape,
        compiler_params=_cparams(1),
        name="mix_in_proj",
    )(x, g, wfm, wtm, gc, wkv, wvT)


def _online_update(parts, v_parts, m_ref, l_ref, acc_ref, hh):
    m_prev = m_ref[hh]
    m_new = jnp.maximum(m_prev, jnp.max(functools.reduce(jnp.maximum, parts), axis=0, keepdims=True))
    m_safe = jnp.where(m_new == NEG_INF, 0.0, m_new)
    alpha = jnp.exp2(m_prev - m_safe)
    ps = [jnp.exp2(s - m_safe) for s in parts]
    pv = functools.reduce(lambda a, b: a + b, [_dot(v, p.astype(BF16)) for v, p in zip(v_parts, ps)])
    l_ref[hh] = alpha * l_ref[hh] + jnp.sum(functools.reduce(lambda a, b: a + b, ps), axis=0, keepdims=True)
    acc_ref[hh] = alpha * acc_ref[hh] + pv
    m_ref[hh] = m_new


def _init_stats(m_ref, l_ref, acc_ref):
    m_ref[...] = jnp.full(m_ref.shape, NEG_INF, F32)
    l_ref[...] = jnp.zeros(l_ref.shape, F32)
    acc_ref[...] = jnp.zeros(acc_ref.shape, F32)


def _finish(l_ref, acc_ref, n_heads):
    oT = jnp.concatenate([acc_ref[hh] * (1.0 / l_ref[hh]) for hh in range(n_heads)], axis=0)
    return oT.T


def _bias_tile(bias_ref, hh, j, i):
    kind = jnp.clip(j - i + 2, 0, 3)
    return bias_ref[hh, pl.ds(pl.multiple_of(kind * TQ, TQ), TQ), :]


def _group_bounds(i):
    n_groups = i // GB + 1
    n_far = jnp.maximum((i - 1) // GB, 0)
    return n_far, n_groups


def _swa_kernel(sink_ref, q_ref, k_ref, v_ref, bias_ref, o_ref):
    g = pl.program_id(1)
    i = pl.program_id(2)
    g_heads = N_HEADS_A // N_KV_A
    start = pl.multiple_of(i * TQ, TQ)
    prev_start = pl.multiple_of(jnp.maximum(i * TQ - WINDOW, 0), WINDOW)
    k_main = k_ref[0, 0, pl.ds(start, TQ), :]
    k_prev = k_ref[0, 0, pl.ds(prev_start, WINDOW), :]
    v_main = v_ref[0, i]
    v_prev = v_ref[0, jnp.maximum(i - 1, 0)][:, TQ - WINDOW:]
    prev_off = jnp.where(i > 0, 0.0, NEG_INF)
    qs = [q_ref[0, 0, hh * HEAD_DIM:(hh + 1) * HEAD_DIM, :] for hh in range(g_heads)]
    s_mains = [_dot(k_main, q) for q in qs]
    s_prevs = [_dot(k_prev, q) for q in qs]
    outs = []
    for hh in range(g_heads):
        s_main = s_mains[hh] + bias_ref[hh, 0:TQ, :]
        s_prev = s_prevs[hh] + bias_ref[hh, TQ:TQ + WINDOW, :] + prev_off
        sink = sink_ref[g * g_heads + hh] * LOG2E
        m = jnp.maximum(jnp.maximum(jnp.max(s_main, axis=0, keepdims=True),
                                    jnp.max(s_prev, axis=0, keepdims=True)), sink)
        p_main = jnp.exp2(s_main - m)
        p_prev = jnp.exp2(s_prev - m)
        denom = (jnp.sum(p_main, axis=0, keepdims=True) + jnp.sum(p_prev, axis=0, keepdims=True)
                 + jnp.exp2(sink - m))
        oT = _dot(v_main, p_main.astype(BF16)) + _dot(v_prev, p_prev.astype(BF16))
        outs.append(oT * (1.0 / denom))
    o_ref[0] = jnp.concatenate(outs, axis=0).T.astype(o_ref.dtype)


def _swa(sinks, qaT, ka, vaT, bias, batch, seq):
    nb = seq // TQ
    g_heads = N_HEADS_A // N_KV_A
    return pl.pallas_call(
        _swa_kernel,
        grid=(batch, N_KV_A, nb),
        in_specs=[pl.BlockSpec(memory_space=pltpu.SMEM),
                  pl.BlockSpec((1, 1, g_heads * HEAD_DIM, TQ), lambda b, g, i: (b, i, g, 0)),
                  pl.BlockSpec((1, 1, seq, HEAD_DIM), lambda b, g, i: (b, g, 0, 0)),
                  pl.BlockSpec((1, nb, HEAD_DIM, TQ), lambda b, g, i: (b, 0, g, 0)),
                  pl.BlockSpec((g_heads, BIAS_ROWS, TQ), lambda b, g, i: (g, 0, 0))],
        out_specs=pl.BlockSpec((1, TQ, g_heads * HEAD_DIM), lambda b, g, i: (b, i, g)),
        out_shape=jax.ShapeDtypeStruct((batch, seq, A_Q), BF16),
        compiler_params=_cparams(3),
        name="swa_attn",
    )(sinks, qaT, ka, vaT, bias)


_MOBA_HPS = 2


def _moba_kernel(q_ref, k_ref, v_ref, kmean_ref, bias_ref, o_ref,
                 sel_ref, m_ref, l_ref, acc_ref, *, n_sel):
    i = pl.program_id(2)
    nb = kmean_ref.shape[2]
    _init_stats(m_ref, l_ref, acc_ref)
    row = lax.broadcasted_iota(jnp.int32, (nb, TQ), 0)
    qs = [q_ref[0, 0, hh * HEAD_DIM:(hh + 1) * HEAD_DIM, :] for hh in range(_MOBA_HPS)]

    for hh in range(_MOBA_HPS):
        gate = _dot(kmean_ref[0, hh][:, :HEAD_DIM], qs[hh].astype(F32))
        gate = jnp.where(row < i, gate, NEG_INF)
        sel = row == i
        for _ in range(n_sel):
            best = jnp.max(gate, axis=0, keepdims=True)
            first = jnp.min(jnp.where(gate == best, row, nb), axis=0, keepdims=True)
            pick = (row == first) & (best > NEG_INF)
            sel = sel | pick
            gate = jnp.where(pick, NEG_INF, gate)
        sel_ref[hh] = jnp.where(sel, 0.0, NEG_INF)

    def group(g, mixed):
        kbase = pl.multiple_of(g * GK, GK)
        s_all = [_dot(k_ref[0, hh, pl.ds(kbase, GK), :], qs[hh]) for hh in range(_MOBA_HPS)]
        for hh in range(_MOBA_HPS):
            parts = []
            for b in range(GB):
                j = g * GB + b
                sb = s_all[hh][b * TQ:(b + 1) * TQ] + sel_ref[hh, pl.ds(j, 1), :]
                if mixed:
                    sb = sb + _bias_tile(bias_ref, hh, j, i)
                parts.append(sb)
            v_parts = [v_ref[0, g, hh * HEAD_DIM:(hh + 1) * HEAD_DIM, b * TQ:(b + 1) * TQ] for b in range(GB)]
            _online_update(parts, v_parts, m_ref, l_ref, acc_ref, hh)

    n_far, n_groups = _group_bounds(i)

    def far_body(g, carry):
        group(g, False)
        return carry

    def mixed_body(g, carry):
        group(g, True)
        return carry

    lax.fori_loop(0, n_far, far_body, 0)
    lax.fori_loop(n_far, n_groups, mixed_body, 0)
    o_ref[0] = _finish(l_ref, acc_ref, _MOBA_HPS).astype(o_ref.dtype)


def _moba(qbT, kb, vbT, kmean, bias, batch, seq):
    nb = seq // TQ
    ng = seq // GK
    hps = _MOBA_HPS
    n_sel = min(MOBA_TOPK, nb - 1)
    bias_blk0 = N_HEADS_A // hps
    return pl.pallas_call(
        functools.partial(_moba_kernel, n_sel=n_sel),
        grid=(batch, N_HEADS_B // hps, nb),
        in_specs=[pl.BlockSpec((1, 1, hps * HEAD_DIM, TQ), lambda b, hp, i: (b, i, hp, 0)),
                  pl.BlockSpec((1, hps, seq, HEAD_DIM), lambda b, hp, i: (b, hp, 0, 0)),
                  pl.BlockSpec((1, ng, hps * HEAD_DIM, GK), lambda b, hp, i: (b, 0, hp, 0)),
                  pl.BlockSpec((1, hps, nb, LANES), lambda b, hp, i: (b, hp, 0, 0)),
                  pl.BlockSpec((hps, BIAS_ROWS, TQ), lambda b, hp, i: (bias_blk0 + hp, 0, 0))],
        out_specs=pl.BlockSpec((1, TQ, hps * HEAD_DIM), lambda b, hp, i: (b, i, hp)),
        out_shape=jax.ShapeDtypeStruct((batch, seq, B_QKV), BF16),
        scratch_shapes=[pltpu.VMEM((hps, nb, TQ), F32),
                        pltpu.VMEM((hps, 1, TQ), F32),
                        pltpu.VMEM((hps, 1, TQ), F32),
                        pltpu.VMEM((hps, HEAD_DIM, TQ), F32)],
        compiler_params=_cparams(3),
        name="moba_attn",
    )(qbT, kb, vbT, kmean, bias)


_SIGN = -2 ** 31
_NEG_INF_ORD = 0x007FFFFF
_N_COUNT_ACC = 4


def _ord_to_f32(u):
    skey = u ^ jnp.int32(_SIGN)
    bits = skey ^ ((skey >> 31) & jnp.int32(0x7FFFFFFF))
    return lax.bitcast_convert_type(bits, F32)


def _dsa_kernel(qi_ref, w_ref, ki_ref, q_ref, k_ref, v_ref, bias_ref, tri_ref, o_ref,
                score_ref, m_ref, l_ref, acc_ref, ties_ref, *, n_keep):
    i = pl.program_id(1)
    n_far, n_groups = _group_bounds(i)

    def rows(g, b=0):
        return pl.ds(pl.multiple_of(g * GK + b * TQ, TQ), TQ)

    w = w_ref[0, 0]
    qis = [qi_ref[0, 0, hi * IDX_DIM:(hi + 1) * IDX_DIM, :] for hi in range(N_IDX_HEADS)]
    krow = lax.broadcasted_iota(jnp.int32, (TQ, TQ), 0)
    qcol = lax.broadcasted_iota(jnp.int32, (TQ, TQ), 1)

    def score_group(g, mixed):
        for b in range(GB):
            ki = ki_ref[0, rows(g, b), :]
            ds = [_dot(ki, qi) for qi in qis]
            sc = functools.reduce(lambda x, y: x + y,
                                  [jnp.maximum(d, 0.0) * w[hi:hi + 1, :] for hi, d in enumerate(ds)])
            if mixed:
                sc = jnp.where(krow + (g * GB + b) * TQ <= qcol + i * TQ, sc, NEG_INF)
            score_ref[rows(g, b), :] = sc

    def score_far(g, carry):
        score_group(g, False)
        return carry

    def score_mixed(g, carry):
        score_group(g, True)
        return carry

    lax.fori_loop(0, n_far, score_far, 0)
    lax.fori_loop(n_far, n_groups, score_mixed, 0)

    def count_ge(cand):
        cand8 = jnp.broadcast_to(cand, (SUBLANES, TQ))

        def body(g, accs):
            accs = list(accs)
            grp = score_ref[pl.ds(pl.multiple_of(g * GK, GK), GK), :]
            for r in range(GK // SUBLANES):
                a = r % _N_COUNT_ACC
                accs[a] = accs[a] + jnp.where(grp[r * SUBLANES:(r + 1) * SUBLANES] >= cand8, 1.0, 0.0)
            return tuple(accs)

        zero = jnp.zeros((SUBLANES, TQ), F32)
        accs = lax.fori_loop(0, n_groups, body, (zero,) * _N_COUNT_ACC)
        return jnp.sum(functools.reduce(lambda x, y: x + y, accs), axis=0, keepdims=True)

    def bit_body(b, u):
        u_try = u | lax.shift_left(jnp.int32(1), 31 - b)
        return jnp.where(count_ge(_ord_to_f32(u_try)) >= n_keep, u_try, u)

    u = lax.fori_loop(0, 32, bit_body, jnp.zeros((1, TQ), jnp.int32))
    u = jnp.where((u >= 0) & (u < _NEG_INF_ORD), jnp.int32(_NEG_INF_ORD), u)
    thr = _ord_to_f32(u)
    thr_next = _ord_to_f32(u + 1)
    n_ge = count_ge(thr)
    tie_budget = n_keep - count_ge(thr_next)
    excess_ties = jnp.max(jnp.where((n_ge > n_keep) & (thr > NEG_INF), 1.0, 0.0)) > 0.5

    _init_stats(m_ref, l_ref, acc_ref)
    qs = [q_ref[0, 0, hh * HEAD_DIM:(hh + 1) * HEAD_DIM, :] for hh in range(N_HEADS_C)]

    def attend_group(g, mixed):
        kc = k_ref[0, pl.ds(pl.multiple_of(g * GK, GK), GK), :]
        s_all = [_dot(kc, q) for q in qs]
        masks = [jnp.where(score_ref[rows(g, b), :] >= thr, 0.0, NEG_INF) for b in range(GB)]
        v_parts = [v_ref[0, g, :, b * TQ:(b + 1) * TQ] for b in range(GB)]
        for hh in range(N_HEADS_C):
            parts = []
            for b in range(GB):
                sb = s_all[hh][b * TQ:(b + 1) * TQ] + masks[b]
                if mixed:
                    sb = sb + _bias_tile(bias_ref, hh, g * GB + b, i)
                parts.append(sb)
            _online_update(parts, v_parts, m_ref, l_ref, acc_ref, hh)

    def attend_group_exact_ties(g, carry):
        for b in range(GB):
            blk = score_ref[rows(g, b), :]
            above = blk >= thr_next
            tie = (blk >= thr) & jnp.logical_not(above)
            tie_f = jnp.where(tie, 1.0, 0.0)
            before = ties_ref[...] + _dot(tri_ref[...], tie_f.astype(BF16))
            keep = above | (tie & (before < tie_budget))
            ties_ref[...] += jnp.sum(tie_f, axis=0, keepdims=True)
            kc = k_ref[0, rows(g, b), :]
            vT = v_ref[0, g, :, b * TQ:(b + 1) * TQ]
            for hh in range(N_HEADS_C):
                s = _dot(kc, qs[hh]) + _bias_tile(bias_ref, hh, g * GB + b, i)
                _online_update([jnp.where(keep, s, NEG_INF)], [vT], m_ref, l_ref, acc_ref, hh)
        return carry

    @pl.when(jnp.logical_not(excess_ties))
    def _():
        def far_body(g, carry):
            attend_group(g, False)
            return carry

        def mixed_body(g, carry):
            attend_group(g, True)
            return carry

        lax.fori_loop(0, n_far, far_body, 0)
        lax.fori_loop(n_far, n_groups, mixed_body, 0)

    @pl.when(excess_ties)
    def _():
        ties_ref[...] = jnp.zeros(ties_ref.shape, F32)
        lax.fori_loop(0, n_groups, attend_group_exact_ties, 0)

    o_ref[0] = _finish(l_ref, acc_ref, N_HEADS_C).astype(o_ref.dtype)


def _dsa(qiT, wT, ki, qcT, kc, vcT, bias, batch, seq):
    nb = seq // TQ
    ng = seq // GK
    n_keep = min(DSA_TOPK, seq // 4)
    tri = jnp.asarray(np.tril(np.ones((TQ, TQ), np.float32), -1), BF16)
    bias_blk = (N_HEADS_A + N_HEADS_B) // N_HEADS_C
    return pl.pallas_call(
        functools.partial(_dsa_kernel, n_keep=n_keep),
        grid=(batch, nb),
        in_specs=[pl.BlockSpec((1, 1, C_QIDX, TQ), lambda b, i: (b, i, 0, 0)),
                  pl.BlockSpec((1, 1, 16, TQ), lambda b, i: (b, i, 0, 0)),
                  pl.BlockSpec((1, seq, IDX_DIM), lambda b, i: (b, 0, 0)),
                  pl.BlockSpec((1, 1, C_Q, TQ), lambda b, i: (b, i, 0, 0)),
                  pl.BlockSpec((1, seq, HEAD_DIM), lambda b, i: (b, 0, 0)),
                  pl.BlockSpec((1, ng, HEAD_DIM, GK), lambda b, i: (b, 0, 0, 0)),
                  pl.BlockSpec((N_HEADS_C, BIAS_ROWS, TQ), lambda b, i: (bias_blk, 0, 0)),
                  pl.BlockSpec((TQ, TQ), lambda b, i: (0, 0))],
        out_specs=pl.BlockSpec((1, TQ, C_Q), lambda b, i: (b, i, 0)),
        out_shape=jax.ShapeDtypeStruct((batch, seq, C_Q), BF16),
        scratch_shapes=[pltpu.VMEM((seq, TQ), F32),
                        pltpu.VMEM((N_HEADS_C, 1, TQ), F32),
                        pltpu.VMEM((N_HEADS_C, 1, TQ), F32),
                        pltpu.VMEM((N_HEADS_C, HEAD_DIM, TQ), F32),
                        pltpu.VMEM((1, TQ), F32)],
        compiler_params=_cparams(2),
        name="dsa_attn",
    )(qiT, wT, ki, qcT, kc, vcT, bias, tri)


def _oproj_kernel(x_ref, a_ref, b_ref, c_ref, wa_ref, wb_ref, wc_ref, o_ref):
    o_ref[...] = (x_ref[...] + _dot(a_ref[...], wa_ref[...]) + _dot(b_ref[...], wb_ref[...])
                  + _dot(c_ref[...], wc_ref[...]))


def _oproj(x, a, b, c, wa, wb, wc):
    n = x.shape[0]
    tm = min(512, n)
    assert n % tm == 0

    def rows(cols):
        return pl.BlockSpec((tm, cols), lambda i: (i, 0))

    def full(w):
        return pl.BlockSpec(w.shape, lambda i: (0, 0))

    return pl.pallas_call(
        _oproj_kernel,
        grid=(n // tm,),
        in_specs=[rows(D_MODEL), rows(A_Q), rows(B_QKV), rows(C_Q), full(wa), full(wb), full(wc)],
        out_specs=rows(D_MODEL),
        out_shape=jax.ShapeDtypeStruct((n, D_MODEL), F32),
        compiler_params=_cparams(1),
        name="mix_out_proj",
    )(x, a, b, c, wa, wb, wc)


def _pad_heads(w, n_heads):
    w = w.reshape(D_MODEL, n_heads, HEAD_DIM)
    return jnp.pad(w, ((0, 0), (0, 0), (0, LANES - HEAD_DIM))).reshape(D_MODEL, n_heads * LANES)


def _split_w_in(w_in):
    cuts = np.cumsum([A_Q, A_KV, A_KV, B_QKV, B_QKV, B_QKV, C_Q, C_KV_LATENT, C_QIDX, IDX_DIM])
    qa, ka, va, qb, kb, vb, qc, ckv, qidx, kidx, widx = jnp.split(w_in, cuts, axis=1)
    wfm = jnp.concatenate([qa, va, qb, vb, qc, qidx, jnp.pad(widx, ((0, 0), (0, 16 - N_IDX_HEADS)))], axis=1)
    wtm = jnp.concatenate([_pad_heads(ka, N_KV_A), _pad_heads(kb, N_HEADS_B), ckv, _pad_heads(kidx, 1)], axis=1)
    assert wfm.shape[1] == _FM_ROWS and wtm.shape[1] == _TM_COLS
    return wfm.T.astype(BF16), wtm.astype(BF16)


def kernel(x, rel_bias_table, ffn1_norm, ffn1_w_gate, ffn1_w_up, ffn1_w_down, mix_norm, w_in, attn_sinks, kv_norm_c, w_kv_up_c, w_out, ffn2_norm, ffn2_w_gate, ffn2_w_up, ffn2_w_down, final_norm):
    batch, seq = x.shape[0], x.shape[1]
    depth = w_in.shape[0]
    assert seq % GK == 0 and x.shape[2] == D_MODEL
    nb = seq // TQ
    bias = _bias_tiles(rel_bias_table)
    gf = final_norm.reshape(1, D_MODEL)
    xf = x.reshape(batch * seq, D_MODEL)
    for l in range(depth):
        xf = _ffn(xf, ffn1_norm[l].reshape(1, D_MODEL), ffn1_w_gate[l].astype(BF16), ffn1_w_up[l].astype(BF16),
                  ffn1_w_down[l].astype(BF16), gf, False)
        wfm, wtm = _split_w_in(w_in[l])
        wkv = w_kv_up_c[l].astype(BF16)
        (qaT, vaT, qbT, vbT, qcT, qiT, wT, ka, kb, kmean, kc, vcT, ki) = _proj(
            xf, mix_norm[l].reshape(1, D_MODEL), wfm, wtm, kv_norm_c[l].reshape(1, C_KV_LATENT),
            wkv, wkv[:, HEAD_DIM:].T, batch, seq)
        kmean = kmean.reshape(batch, nb, N_HEADS_B, LANES).transpose(0, 2, 1, 3)
        out_a = _swa(attn_sinks[l], qaT, ka, vaT, bias, batch, seq)
        out_b = _moba(qbT, kb, vbT, kmean, bias, batch, seq)
        out_c = _dsa(qiT, wT, ki, qcT, kc, vcT, bias, batch, seq)
        wo = w_out[l].astype(BF16)
        xf = _oproj(xf, out_a.reshape(batch * seq, A_Q), out_b.reshape(batch * seq, B_QKV),
                    out_c.reshape(batch * seq, C_Q), wo[:A_Q], wo[A_Q:A_Q + B_QKV], wo[A_Q + B_QKV:])
        xf = _ffn(xf, ffn2_norm[l].reshape(1, D_MODEL), ffn2_w_gate[l].astype(BF16), ffn2_w_up[l].astype(BF16),
                  ffn2_w_down[l].astype(BF16), gf, l == depth - 1)
    return xf.reshape(batch, seq, D_MODEL)
```

```python
import functools
import math

import jax
import jax.numpy as jnp
import numpy as np
from jax import lax
from jax.experimental import pallas as pl
from jax.experimental.pallas import tpu as pltpu

D_MODEL = 1024
HEAD_DIM = 64
N_HEADS = 16
N_HEADS_A = 8
N_KV_A = 2
WINDOW = 128
N_HEADS_B = 4
MOBA_BLOCK = 256
MOBA_TOPK = 3
N_HEADS_C = 4
C_KV_LATENT = 128
N_IDX_HEADS = 4
IDX_DIM = 64
DSA_TOPK = 256
D_FF = 2816
N_BUCKETS = 32
MAX_DISTANCE = 128
RMS_EPS = 1e-6

A_Q = N_HEADS_A * HEAD_DIM
A_KV = N_KV_A * HEAD_DIM
B_QKV = N_HEADS_B * HEAD_DIM
C_Q = N_HEADS_C * HEAD_DIM
C_QIDX = N_IDX_HEADS * IDX_DIM

TQ = 256
GB = 4
GK = GB * TQ
SUBLANES = 8
LANES = 128
FFN_TM = 1024
FFN_TF = 256
PROJ_TM = GK
VMEM_LIMIT = 56 * 1024 * 1024

F32 = jnp.float32
BF16 = jnp.bfloat16
NEG_INF = float("-inf")
LOG2E = math.log2(math.e)
QK_SCALE = HEAD_DIM ** -0.5 * LOG2E

BIAS_ROWS = 4 * TQ

_NT = (((1,), (1,)), ((), ()))


def _cparams(n_axes):
    return pltpu.CompilerParams(dimension_semantics=("arbitrary",) * n_axes,
                                vmem_limit_bytes=VMEM_LIMIT)


def _dot(a, b):
    return jnp.dot(a, b, preferred_element_type=F32)


def _dot_nt(a, b):
    return lax.dot_general(a, b, _NT, preferred_element_type=F32)


def _rms(x, g):
    return x * lax.rsqrt(jnp.mean(x * x, axis=-1, keepdims=True) + RMS_EPS) * g


def _t5_bucket_np(dist):
    n = np.maximum(dist, 0)
    max_exact = N_BUCKETS // 2
    nf = np.maximum(n, 1).astype(np.float32)
    large = max_exact + (np.log(nf / np.float32(max_exact)) / np.float32(math.log(MAX_DISTANCE / max_exact))
                         * np.float32(N_BUCKETS - max_exact)).astype(np.int32)
    large = np.minimum(large, N_BUCKETS - 1)
    return np.where(n < max_exact, n, large).astype(np.int32)


_IDX_MASKED = -1
_IDX_ZERO = -2


def _bias_index_tiles():
    col = np.arange(TQ)[None, :]
    row = np.arange(TQ)[:, None]
    d_own = col - row
    d_prev = TQ + col - row
    d_prev_w = WINDOW + col - np.arange(WINDOW)[:, None]
    a = np.full((BIAS_ROWS, TQ), _IDX_MASKED, np.int32)
    a[:TQ] = np.where((d_own >= 0) & (d_own < WINDOW), _t5_bucket_np(d_own), _IDX_MASKED)
    a[TQ:TQ + WINDOW] = np.where((d_prev_w >= 0) & (d_prev_w < WINDOW), _t5_bucket_np(d_prev_w), _IDX_MASKED)
    b = np.full((BIAS_ROWS, TQ), _IDX_MASKED, np.int32)
    b[:TQ] = _IDX_ZERO
    b[TQ:2 * TQ] = _t5_bucket_np(d_prev)
    b[2 * TQ:3 * TQ] = np.where(d_own >= 0, _t5_bucket_np(d_own), _IDX_MASKED)
    assert (_t5_bucket_np(np.arange(2 * TQ, 64 * TQ)) == N_BUCKETS - 1).all()
    return np.stack([a, b])


def _bias_kernel(tab_ref, idx_ref, o_ref):
    h = pl.program_id(0)
    idx = idx_ref[0]
    shift = jnp.where(h >= N_HEADS_A, tab_ref[N_BUCKETS - 1, h], 0.0)
    out = jnp.where(idx == _IDX_ZERO, 0.0, NEG_INF)
    for b in range(N_BUCKETS):
        out = jnp.where(idx == b, (tab_ref[b, h] - shift) * LOG2E, out)
    o_ref[0] = out


def _bias_tiles(rel_bias_table):
    idx = jnp.asarray(_bias_index_tiles())
    return pl.pallas_call(
        _bias_kernel,
        grid=(N_HEADS,),
        in_specs=[pl.BlockSpec(memory_space=pltpu.SMEM),
                  pl.BlockSpec((1, BIAS_ROWS, TQ), lambda h: (h // N_HEADS_A, 0, 0))],
        out_specs=pl.BlockSpec((1, BIAS_ROWS, TQ), lambda h: (h, 0, 0)),
        out_shape=jax.ShapeDtypeStruct((N_HEADS, BIAS_ROWS, TQ), F32),
        compiler_params=_cparams(1),
        name="t5_bias_tiles",
    )(rel_bias_table, idx)


def _ffn_kernel(x_ref, g_ref, wg_ref, wu_ref, wd_ref, gf_ref, o_ref, h_ref, acc_ref, *, final_norm):
    k = pl.program_id(1)

    @pl.when(k == 0)
    def _():
        h_ref[...] = _rms(x_ref[...], g_ref[...]).astype(BF16)
        acc_ref[...] = jnp.zeros_like(acc_ref)

    h = h_ref[...]
    gate = _dot(h, wg_ref[...])
    up = _dot(h, wu_ref[...])
    act = (gate * jax.nn.sigmoid(gate) * up).astype(BF16)
    acc_ref[...] += _dot(act, wd_ref[...])

    @pl.when(k == pl.num_programs(1) - 1)
    def _():
        y = x_ref[...] + 0.5 * acc_ref[...]
        if final_norm:
            y = _rms(y, gf_ref[...])
        o_ref[...] = y


def _ffn(x, g, wg, wu, wd, gf, final_norm):
    n = x.shape[0]
    tm = min(FFN_TM, n)
    assert n % tm == 0 and D_FF % FFN_TF == 0
    return pl.pallas_call(
        functools.partial(_ffn_kernel, final_norm=final_norm),
        grid=(n // tm, D_FF // FFN_TF),
        in_specs=[pl.BlockSpec((tm, D_MODEL), lambda i, k: (i, 0)),
                  pl.BlockSpec((1, D_MODEL), lambda i, k: (0, 0)),
                  pl.BlockSpec((D_MODEL, FFN_TF), lambda i, k: (0, k)),
                  pl.BlockSpec((D_MODEL, FFN_TF), lambda i, k: (0, k)),
                  pl.BlockSpec((FFN_TF, D_MODEL), lambda i, k: (k, 0)),
                  pl.BlockSpec((1, D_MODEL), lambda i, k: (0, 0))],
        out_specs=pl.BlockSpec((tm, D_MODEL), lambda i, k: (i, 0)),
        out_shape=jax.ShapeDtypeStruct((n, D_MODEL), F32),
        scratch_shapes=[pltpu.VMEM((tm, D_MODEL), BF16), pltpu.VMEM((tm, D_MODEL), F32)],
        compiler_params=_cparams(2),
        name="swiglu_ffn",
    )(x, g, wg, wu, wd, gf)


_FM_QA, _FM_VA, _FM_QB, _FM_VB, _FM_QC, _FM_QI, _FM_W = 0, 512, 640, 896, 1152, 1408, 1664
_FM_ROWS = 1680
_TM_KA, _TM_KB, _TM_CKV, _TM_KI = 0, 256, 768, 896
_TM_COLS = 1024


def _proj_kernel(x_ref, g_ref, wfm_ref, wtm_ref, gc_ref, wkv_ref, wvT_ref,
                 qa_ref, va_ref, qb_ref, vb_ref, qc_ref, qi_ref, w_ref,
                 ka_ref, kb_ref, kmean_ref, kc_ref, vc_ref, ki_ref):
    tm = x_ref.shape[0]
    nsub = tm // TQ
    h = _rms(x_ref[...], g_ref[...]).astype(BF16)

    def fm(lo, hi, scale=None):
        y = _dot_nt(wfm_ref[lo:hi, :], h)
        return y if scale is None else y * scale

    def put_fm(ref, y):
        for r in range(nsub):
            ref[0, r] = y[:, r * TQ:(r + 1) * TQ].astype(ref.dtype)

    put_fm(qa_ref, fm(_FM_QA, _FM_VA, QK_SCALE))
    put_fm(va_ref, fm(_FM_VA, _FM_QB))
    put_fm(qb_ref, fm(_FM_QB, _FM_VB, QK_SCALE))
    vb_ref[0, 0] = fm(_FM_VB, _FM_QC).astype(BF16)
    put_fm(qc_ref, fm(_FM_QC, _FM_QI, QK_SCALE))
    put_fm(qi_ref, fm(_FM_QI, _FM_W))
    put_fm(w_ref, fm(_FM_W, _FM_ROWS, (N_IDX_HEADS ** -0.5) * (IDX_DIM ** -0.5)))

    ptm = _dot(h, wtm_ref[...])
    for kv in range(N_KV_A):
        lo = _TM_KA + kv * LANES
        ka_ref[0, kv] = ptm[:, lo:lo + HEAD_DIM].astype(BF16)
    for hb in range(N_HEADS_B):
        lo = _TM_KB + hb * LANES
        kb_ref[0, hb] = ptm[:, lo:lo + HEAD_DIM].astype(BF16)
    for r in range(nsub):
        kmean_ref[0, r] = jnp.mean(ptm[r * TQ:(r + 1) * TQ, _TM_KB:_TM_CKV], axis=0, keepdims=True)
    ki_ref[0] = ptm[:, _TM_KI:_TM_KI + IDX_DIM].astype(BF16)

    ckv = _rms(ptm[:, _TM_CKV:_TM_KI], gc_ref[...]).astype(BF16)
    kc_ref[0] = _dot(ckv, wkv_ref[...])[:, :HEAD_DIM].astype(BF16)
    vc_ref[0, 0] = _dot_nt(wvT_ref[...], ckv).astype(BF16)


def _proj(x, g, wfm, wtm, gc, wkv, wvT, batch, seq):
    tm = PROJ_TM
    assert seq % tm == 0 and tm == GK
    tpb = seq // tm
    nsub = tm // TQ
    nb = seq // TQ

    def fm_spec(rows):
        return pl.BlockSpec((1, nsub, rows, TQ), lambda i: (i // tpb, i % tpb, 0, 0))

    def fm_shape(rows, dtype=BF16):
        return jax.ShapeDtypeStruct((batch, nb, rows, TQ), dtype)

    def grp_spec(rows):
        return pl.BlockSpec((1, 1, rows, GK), lambda i: (i // tpb, i % tpb, 0, 0))

    def grp_shape(rows):
        return jax.ShapeDtypeStruct((batch, tpb, rows, GK), BF16)

    def full(a):
        return pl.BlockSpec(a.shape, lambda i: (0,) * a.ndim)

    out_specs = [fm_spec(A_Q), fm_spec(A_KV), fm_spec(B_QKV), grp_spec(B_QKV), fm_spec(C_Q), fm_spec(C_QIDX),
                 fm_spec(16),
                 pl.BlockSpec((1, N_KV_A, tm, HEAD_DIM), lambda i: (i // tpb, 0, i % tpb, 0)),
                 pl.BlockSpec((1, N_HEADS_B, tm, HEAD_DIM), lambda i: (i // tpb, 0, i % tpb, 0)),
                 pl.BlockSpec((1, nsub, 1, N_HEADS_B * LANES), lambda i: (i // tpb, i % tpb, 0, 0)),
                 pl.BlockSpec((1, tm, HEAD_DIM), lambda i: (i // tpb, i % tpb, 0)),
                 grp_spec(HEAD_DIM),
                 pl.BlockSpec((1, tm, IDX_DIM), lambda i: (i // tpb, i % tpb, 0))]
    out_shape = [fm_shape(A_Q), fm_shape(A_KV), fm_shape(B_QKV), grp_shape(B_QKV), fm_shape(C_Q), fm_shape(C_QIDX),
                 fm_shape(16, F32),
                 jax.ShapeDtypeStruct((batch, N_KV_A, seq, HEAD_DIM), BF16),
                 jax.ShapeDtypeStruct((batch, N_HEADS_B, seq, HEAD_DIM), BF16),
                 jax.ShapeDtypeStruct((batch, nb, 1, N_HEADS_B * LANES), F32),
                 jax.ShapeDtypeStruct((batch, seq, HEAD_DIM), BF16),
                 grp_shape(HEAD_DIM),
                 jax.ShapeDtypeStruct((batch, seq, IDX_DIM), BF16)]
    return pl.pallas_call(
        _proj_kernel,
        grid=(batch * tpb,),
        in_specs=[pl.BlockSpec((tm, D_MODEL), lambda i: (i, 0)),
                  full(g), full(wfm), full(wtm), full(gc), full(wkv), full(wvT)],
        out_specs=out_specs,
        out_shape=out_shape,
        compiler_params=_cparams(1),
        name="mix_in_proj",
    )(x, g, wfm, wtm, gc, wkv, wvT)


def _online_update(parts, v_parts, m_ref, l_ref, acc_ref, hh):
    m_prev = m_ref[hh]
    m_new = jnp.maximum(m_prev, jnp.max(functools.reduce(jnp.maximum, parts), axis=0, keepdims=True))
    m_safe = jnp.where(m_new == NEG_INF, 0.0, m_new)
    alpha = jnp.exp2(m_prev - m_safe)
    ps = [jnp.exp2(s - m_safe) for s in parts]
    pv = functools.reduce(lambda a, b: a + b, [_dot(v, p.astype(BF16)) for v, p in zip(v_parts, ps)])
    l_ref[hh] = alpha * l_ref[hh] + jnp.sum(functools.reduce(lambda a, b: a + b, ps), axis=0, keepdims=True)
    acc_ref[hh] = alpha * acc_ref[hh] + pv
    m_ref[hh] = m_new


def _init_stats(m_ref, l_ref, acc_ref):
    m_ref[...] = jnp.full(m_ref.shape, NEG_INF, F32)
    l_ref[...] = jnp.zeros(l_ref.shape, F32)
    acc_ref[...] = jnp.zeros(acc_ref.shape, F32)


def _finish(l_ref, acc_ref, n_heads):
    oT = jnp.concatenate([acc_ref[hh] * (1.0 / l_ref[hh]) for hh in range(n_heads)], axis=0)
    return oT.T


def _bias_tile(bias_ref, hh, j, i):
    kind = jnp.clip(j - i + 2, 0, 3)
    return bias_ref[hh, pl.ds(pl.multiple_of(kind * TQ, TQ), TQ), :]


def _group_bounds(i):
    n_groups = i // GB + 1
    n_far = jnp.maximum((i - 1) // GB, 0)
    return n_far, n_groups


def _swa_kernel(sink_ref, q_ref, k_ref, v_ref, bias_ref, o_ref):
    g = pl.program_id(1)
    i = pl.program_id(2)
    g_heads = N_HEADS_A // N_KV_A
    start = pl.multiple_of(i * TQ, TQ)
    prev_start = pl.multiple_of(jnp.maximum(i * TQ - WINDOW, 0), WINDOW)
    k_main = k_ref[0, 0, pl.ds(start, TQ), :]
    k_prev = k_ref[0, 0, pl.ds(prev_start, WINDOW), :]
    v_main = v_ref[0, i]
    v_prev = v_ref[0, jnp.maximum(i - 1, 0)][:, TQ - WINDOW:]
    prev_off = jnp.where(i > 0, 0.0, NEG_INF)
    qs = [q_ref[0, 0, hh * HEAD_DIM:(hh + 1) * HEAD_DIM, :] for hh in range(g_heads)]
    s_mains = [_dot(k_main, q) for q in qs]
    s_prevs = [_dot(k_prev, q) for q in qs]
    outs = []
    for hh in range(g_heads):
        s_main = s_mains[hh] + bias_ref[hh, 0:TQ, :]
        s_prev = s_prevs[hh] + bias_ref[hh, TQ:TQ + WINDOW, :] + prev_off
        sink = sink_ref[g * g_heads + hh] * LOG2E
        m = jnp.maximum(jnp.maximum(jnp.max(s_main, axis=0, keepdims=True),
                                    jnp.max(s_prev, axis=0, keepdims=True)), sink)
        p_main = jnp.exp2(s_main - m)
        p_prev = jnp.exp2(s_prev - m)
        denom = (jnp.sum(p_main, axis=0, keepdims=True) + jnp.sum(p_prev, axis=0, keepdims=True)
                 + jnp.exp2(sink - m))
        oT = _dot(v_main, p_main.astype(BF16)) + _dot(v_prev, p_prev.astype(BF16))
        outs.append(oT * (1.0 / denom))
    o_ref[0] = jnp.concatenate(outs, axis=0).T.astype(o_ref.dtype)


def _swa(sinks, qaT, ka, vaT, bias, batch, seq):
    nb = seq // TQ
    g_heads = N_HEADS_A // N_KV_A
    return pl.pallas_call(
        _swa_kernel,
        grid=(batch, N_KV_A, nb),
        in_specs=[pl.BlockSpec(memory_space=pltpu.SMEM),
                  pl.BlockSpec((1, 1, g_heads * HEAD_DIM, TQ), lambda b, g, i: (b, i, g, 0)),
                  pl.BlockSpec((1, 1, seq, HEAD_DIM), lambda b, g, i: (b, g, 0, 0)),
                  pl.BlockSpec((1, nb, HEAD_DIM, TQ), lambda b, g, i: (b, 0, g, 0)),
                  pl.BlockSpec((g_heads, BIAS_ROWS, TQ), lambda b, g, i: (g, 0, 0))],
        out_specs=pl.BlockSpec((1, TQ, g_heads * HEAD_DIM), lambda b, g, i: (b, i, g)),
        out_shape=jax.ShapeDtypeStruct((batch, seq, A_Q), BF16),
        compiler_params=_cparams(3),
        name="swa_attn",
    )(sinks, qaT, ka, vaT, bias)


_MOBA_HPS = 2


def _moba_kernel(q_ref, k_ref, v_ref, kmean_ref, bias_ref, o_ref,
                 sel_ref, m_ref, l_ref, acc_ref, *, n_sel):
    i = pl.program_id(2)
    nb = kmean_ref.shape[2]
    _init_stats(m_ref, l_ref, acc_ref)
    row = lax.broadcasted_iota(jnp.int32, (nb, TQ), 0)
    qs = [q_ref[0, 0, hh * HEAD_DIM:(hh + 1) * HEAD_DIM, :] for hh in range(_MOBA_HPS)]

    for hh in range(_MOBA_HPS):
        gate = _dot(kmean_ref[0, hh][:, :HEAD_DIM], qs[hh].astype(F32))
        gate = jnp.where(row < i, gate, NEG_INF)
        sel = row == i
        for _ in range(n_sel):
            best = jnp.max(gate, axis=0, keepdims=True)
            first = jnp.min(jnp.where(gate == best, row, nb), axis=0, keepdims=True)
            pick = (row == first) & (best > NEG_INF)
            sel = sel | pick
            gate = jnp.where(pick, NEG_INF, gate)
        sel_ref[hh] = jnp.where(sel, 0.0, NEG_INF)

    def group(g, mixed):
        kbase = pl.multiple_of(g * GK, GK)
        s_all = [_dot(k_ref[0, hh, pl.ds(kbase, GK), :], qs[hh]) for hh in range(_MOBA_HPS)]
        for hh in range(_MOBA_HPS):
            parts = []
            for b in range(GB):
                j = g * GB + b
                sb = s_all[hh][b * TQ:(b + 1) * TQ] + sel_ref[hh, pl.ds(j, 1), :]
                if mixed:
                    sb = sb + _bias_tile(bias_ref, hh, j, i)
                parts.append(sb)
            v_parts = [v_ref[0, g, hh * HEAD_DIM:(hh + 1) * HEAD_DIM, b * TQ:(b + 1) * TQ] for b in range(GB)]
            _online_update(parts, v_parts, m_ref, l_ref, acc_ref, hh)

    n_far, n_groups = _group_bounds(i)

    def far_body(g, carry):
        group(g, False)
        return carry

    def mixed_body(g, carry):
        group(g, True)
        return carry

    lax.fori_loop(0, n_far, far_body, 0)
    lax.fori_loop(n_far, n_groups, mixed_body, 0)
    o_ref[0] = _finish(l_ref, acc_ref, _MOBA_HPS).astype(o_ref.dtype)


def _moba(qbT, kb, vbT, kmean, bias, batch, seq):
    nb = seq // TQ
    ng = seq // GK
    hps = _MOBA_HPS
    n_sel = min(MOBA_TOPK, nb - 1)
    bias_blk0 = N_HEADS_A // hps
    return pl.pallas_call(
        functools.partial(_moba_kernel, n_sel=n_sel),
        grid=(batch, N_HEADS_B // hps, nb),
        in_specs=[pl.BlockSpec((1, 1, hps * HEAD_DIM, TQ), lambda b, hp, i: (b, i, hp, 0)),
                  pl.BlockSpec((1, hps, seq, HEAD_DIM), lambda b, hp, i: (b, hp, 0, 0)),
                  pl.BlockSpec((1, ng, hps * HEAD_DIM, GK), lambda b, hp, i: (b, 0, hp, 0)),
                  pl.BlockSpec((1, hps, nb, LANES), lambda b, hp, i: (b, hp, 0, 0)),
                  pl.BlockSpec((hps, BIAS_ROWS, TQ), lambda b, hp, i: (bias_blk0 + hp, 0, 0))],
        out_specs=pl.BlockSpec((1, TQ, hps * HEAD_DIM), lambda b, hp, i: (b, i, hp)),
        out_shape=jax.ShapeDtypeStruct((batch, seq, B_QKV), BF16),
        scratch_shapes=[pltpu.VMEM((hps, nb, TQ), F32),
                        pltpu.VMEM((hps, 1, TQ), F32),
                        pltpu.VMEM((hps, 1, TQ), F32),
                        pltpu.VMEM((hps, HEAD_DIM, TQ), F32)],
        compiler_params=_cparams(3),
        name="moba_attn",
    )(qbT, kb, vbT, kmean, bias)


_PACK16 = 16
_I16_MIN, _I16_MAX = -32768, 32767
_N_COUNT_ACC = 4


def _f32_to_key(x):
    bits = lax.bitcast_convert_type(x, jnp.int32)
    key = bits ^ ((bits >> 31) & jnp.int32(0x7FFFFFFF))
    return jnp.where(key == -1, 0, key)


def _key_to_f32(key):
    bits = key ^ ((key >> 31) & jnp.int32(0x7FFFFFFF))
    return lax.bitcast_convert_type(bits, F32)


_MIN_NORMAL_KEY = 0x00800000


def _next_key(key):
    nxt = key + 1
    nxt = jnp.where((nxt >= -_MIN_NORMAL_KEY) & (nxt < 0), 0, nxt)
    return jnp.where((nxt > 0) & (nxt < _MIN_NORMAL_KEY), _MIN_NORMAL_KEY, nxt)


def _dsa_kernel(qi_ref, w_ref, ki_ref, q_ref, k_ref, v_ref, bias_ref, tri_ref, o_ref,
                score_ref, hi_ref, lo_ref, m_ref, l_ref, acc_ref, ties_ref, *, n_keep):
    i = pl.program_id(1)
    n_far, n_groups = _group_bounds(i)

    def rows(g, b=0):
        return pl.ds(pl.multiple_of(g * GK + b * TQ, TQ), TQ)

    w = w_ref[0, 0]
    qis = [qi_ref[0, 0, hi * IDX_DIM:(hi + 1) * IDX_DIM, :] for hi in range(N_IDX_HEADS)]
    krow = lax.broadcasted_iota(jnp.int32, (TQ, TQ), 0)
    qcol = lax.broadcasted_iota(jnp.int32, (TQ, TQ), 1)

    def score_group(g, mixed):
        for b in range(GB):
            ki = ki_ref[0, rows(g, b), :]
            ds = [_dot(ki, qi) for qi in qis]
            sc = functools.reduce(lambda x, y: x + y,
                                  [jnp.maximum(d, 0.0) * w[hi:hi + 1, :] for hi, d in enumerate(ds)])
            if mixed:
                sc = jnp.where(krow + (g * GB + b) * TQ <= qcol + i * TQ, sc, NEG_INF)
            score_ref[rows(g, b), :] = sc
            key = _f32_to_key(sc)
            hi_ref[rows(g, b), :] = (key >> 16).astype(jnp.int16)
            lo_ref[rows(g, b), :] = ((key & 0xFFFF) + _I16_MIN).astype(jnp.int16)

    def score_far(g, carry):
        score_group(g, False)
        return carry

    def score_mixed(g, carry):
        score_group(g, True)
        return carry

    lax.fori_loop(0, n_far, score_far, 0)
    lax.fori_loop(n_far, n_groups, score_mixed, 0)

    def group16(ref, g):
        return ref[pl.ds(pl.multiple_of(g * GK, GK), GK), :]

    def count_ge16(ref, cand):
        cand16 = jnp.broadcast_to(cand.astype(jnp.int16), (_PACK16, TQ))

        def body(g, accs):
            accs = list(accs)
            grp = group16(ref, g)
            for r in range(GK // _PACK16):
                a = r % _N_COUNT_ACC
                hit = grp[r * _PACK16:(r + 1) * _PACK16] >= cand16
                accs[a] = accs[a] + jnp.where(hit, jnp.int16(1), jnp.int16(0))
            return tuple(accs)

        zero = jnp.zeros((_PACK16, TQ), jnp.int16)
        accs = lax.fori_loop(0, n_groups, body, (zero,) * _N_COUNT_ACC)
        total = functools.reduce(lambda x, y: x + y, accs).astype(jnp.int32)
        return jnp.sum(total, axis=0, keepdims=True)

    def count_gt16(ref, cand):
        return jnp.where(cand >= _I16_MAX, 0, count_ge16(ref, jnp.minimum(cand + 1, _I16_MAX)))

    def kth_largest16(ref, k):
        def bit_body(b, c):
            c_try = c + lax.shift_left(jnp.int32(1), 15 - b)
            return jnp.where(count_ge16(ref, c_try) >= k, c_try, c)
        return lax.fori_loop(0, 16, bit_body, jnp.full((1, TQ), _I16_MIN, jnp.int32))

    c_hi = kth_largest16(hi_ref, n_keep)
    n_gt_hi = count_gt16(hi_ref, c_hi)
    c_hi16 = c_hi.astype(jnp.int16)

    def mask_lo(g, carry):
        sl = pl.ds(pl.multiple_of(g * GK, GK), GK)
        lo_ref[sl, :] = jnp.where(hi_ref[sl, :] == c_hi16, lo_ref[sl, :], jnp.int16(_I16_MIN))
        return carry

    lax.fori_loop(0, n_groups, mask_lo, 0)
    c_lo = kth_largest16(lo_ref, n_keep - n_gt_hi)
    key_thr = lax.shift_left(c_hi, 16) | (c_lo - _I16_MIN)
    thr = _key_to_f32(key_thr)
    thr_next = _key_to_f32(_next_key(key_thr))
    n_ge = n_gt_hi + count_ge16(lo_ref, c_lo)
    tie_budget = (n_keep - (n_gt_hi + count_gt16(lo_ref, c_lo))).astype(F32)
    excess_ties = jnp.max(jnp.where((n_ge > n_keep) & (thr > NEG_INF), 1.0, 0.0)) > 0.5

    _init_stats(m_ref, l_ref, acc_ref)
    qs = [q_ref[0, 0, hh * HEAD_DIM:(hh + 1) * HEAD_DIM, :] for hh in range(N_HEADS_C)]

    def attend_group(g, mixed):
        kc = k_ref[0, pl.ds(pl.multiple_of(g * GK, GK), GK), :]
        s_all = [_dot(kc, q) for q in qs]
        masks = [jnp.where(score_ref[rows(g, b), :] >= thr, 0.0, NEG_INF) for b in range(GB)]
        v_parts = [v_ref[0, g, :, b * TQ:(b + 1) * TQ] for b in range(GB)]
        for hh in range(N_HEADS_C):
            parts = []
            for b in range(GB):
                sb = s_all[hh][b * TQ:(b + 1) * TQ] + masks[b]
                if mixed:
                    sb = sb + _bias_tile(bias_ref, hh, g * GB + b, i)
                parts.append(sb)
            _online_update(parts, v_parts, m_ref, l_ref, acc_ref, hh)

    def attend_group_exact_ties(g, carry):
        for b in range(GB):
            blk = score_ref[rows(g, b), :]
            above = blk >= thr_next
            tie = (blk >= thr) & jnp.logical_not(above)
            tie_f = jnp.where(tie, 1.0, 0.0)
            before = ties_ref[...] + _dot(tri_ref[...], tie_f.astype(BF16))
            keep = above | (tie & (before < tie_budget))
            ties_ref[...] += jnp.sum(tie_f, axis=0, keepdims=True)
            kc = k_ref[0, rows(g, b), :]
            vT = v_ref[0, g, :, b * TQ:(b + 1) * TQ]
            for hh in range(N_HEADS_C):
                s = _dot(kc, qs[hh]) + _bias_tile(bias_ref, hh, g * GB + b, i)
                _online_update([jnp.where(keep, s, NEG_INF)], [vT], m_ref, l_ref, acc_ref, hh)
        return carry

    @pl.when(jnp.logical_not(excess_ties))
    def _():
        def far_body(g, carry):
            attend_group(g, False)
            return carry

        def mixed_body(g, carry):
            attend_group(g, True)
            return carry

        lax.fori_loop(0, n_far, far_body, 0)
        lax.fori_loop(n_far, n_groups, mixed_body, 0)

    @pl.when(excess_ties)
    def _():
        ties_ref[...] = jnp.zeros(ties_ref.shape, F32)
        lax.fori_loop(0, n_groups, attend_group_exact_ties, 0)

    o_ref[0] = _finish(l_ref, acc_ref, N_HEADS_C).astype(o_ref.dtype)


def _dsa(qiT, wT, ki, qcT, kc, vcT, bias, batch, seq):
    nb = seq // TQ
    ng = seq // GK
    n_keep = min(DSA_TOPK, seq // 4)
    tri = jnp.asarray(np.tril(np.ones((TQ, TQ), np.float32), -1), BF16)
    bias_blk = (N_HEADS_A + N_HEADS_B) // N_HEADS_C
    return pl.pallas_call(
        functools.partial(_dsa_kernel, n_keep=n_keep),
        grid=(batch, nb),
        in_specs=[pl.BlockSpec((1, 1, C_QIDX, TQ), lambda b, i: (b, i, 0, 0)),
                  pl.BlockSpec((1, 1, 16, TQ), lambda b, i: (b, i, 0, 0)),
                  pl.BlockSpec((1, seq, IDX_DIM), lambda b, i: (b, 0, 0)),
                  pl.BlockSpec((1, 1, C_Q, TQ), lambda b, i: (b, i, 0, 0)),
                  pl.BlockSpec((1, seq, HEAD_DIM), lambda b, i: (b, 0, 0)),
                  pl.BlockSpec((1, ng, HEAD_DIM, GK), lambda b, i: (b, 0, 0, 0)),
                  pl.BlockSpec((N_HEADS_C, BIAS_ROWS, TQ), lambda b, i: (bias_blk, 0, 0)),
                  pl.BlockSpec((TQ, TQ), lambda b, i: (0, 0))],
        out_specs=pl.BlockSpec((1, TQ, C_Q), lambda b, i: (b, i, 0)),
        out_shape=jax.ShapeDtypeStruct((batch, seq, C_Q), BF16),
        scratch_shapes=[pltpu.VMEM((seq, TQ), F32),
                        pltpu.VMEM((seq, TQ), jnp.int16),
                        pltpu.VMEM((seq, TQ), jnp.int16),
                        pltpu.VMEM((N_HEADS_C, 1, TQ), F32),
                        pltpu.VMEM((N_HEADS_C, 1, TQ), F32),
                        pltpu.VMEM((N_HEADS_C, HEAD_DIM, TQ), F32),
                        pltpu.VMEM((1, TQ), F32)],
        compiler_params=_cparams(2),
        name="dsa_attn",
    )(qiT, wT, ki, qcT, kc, vcT, bias, tri)


def _oproj_kernel(x_ref, a_ref, b_ref, c_ref, wa_ref, wb_ref, wc_ref, o_ref):
    o_ref[...] = (x_ref[...] + _dot(a_ref[...], wa_ref[...]) + _dot(b_ref[...], wb_ref[...])
                  + _dot(c_ref[...], wc_ref[...]))


def _oproj(x, a, b, c, wa, wb, wc):
    n = x.shape[0]
    tm = min(512, n)
    assert n % tm == 0

    def rows(cols):
        return pl.BlockSpec((tm, cols), lambda i: (i, 0))

    def full(w):
        return pl.BlockSpec(w.shape, lambda i: (0, 0))

    return pl.pallas_call(
        _oproj_kernel,
        grid=(n // tm,),
        in_specs=[rows(D_MODEL), rows(A_Q), rows(B_QKV), rows(C_Q), full(wa), full(wb), full(wc)],
        out_specs=rows(D_MODEL),
        out_shape=jax.ShapeDtypeStruct((n, D_MODEL), F32),
        compiler_params=_cparams(1),
        name="mix_out_proj",
    )(x, a, b, c, wa, wb, wc)


def _pad_heads(w, n_heads):
    w = w.reshape(D_MODEL, n_heads, HEAD_DIM)
    return jnp.pad(w, ((0, 0), (0, 0), (0, LANES - HEAD_DIM))).reshape(D_MODEL, n_heads * LANES)


def _split_w_in(w_in):
    cuts = np.cumsum([A_Q, A_KV, A_KV, B_QKV, B_QKV, B_QKV, C_Q, C_KV_LATENT, C_QIDX, IDX_DIM])
    qa, ka, va, qb, kb, vb, qc, ckv, qidx, kidx, widx = jnp.split(w_in, cuts, axis=1)
    wfm = jnp.concatenate([qa, va, qb, vb, qc, qidx, jnp.pad(widx, ((0, 0), (0, 16 - N_IDX_HEADS)))], axis=1)
    wtm = jnp.concatenate([_pad_heads(ka, N_KV_A), _pad_heads(kb, N_HEADS_B), ckv, _pad_heads(kidx, 1)], axis=1)
    assert wfm.shape[1] == _FM_ROWS and wtm.shape[1] == _TM_COLS
    return wfm.T.astype(BF16), wtm.astype(BF16)


def kernel(x, rel_bias_table, ffn1_norm, ffn1_w_gate, ffn1_w_up, ffn1_w_down, mix_norm, w_in, attn_sinks, kv_norm_c, w_kv_up_c, w_out, ffn2_norm, ffn2_w_gate, ffn2_w_up, ffn2_w_down, final_norm):
    batch, seq = x.shape[0], x.shape[1]
    depth = w_in.shape[0]
    assert seq % GK == 0 and x.shape[2] == D_MODEL
    nb = seq // TQ
    bias = _bias_tiles(rel_bias_table)
    gf = final_norm.reshape(1, D_MODEL)
    xf = x.reshape(batch * seq, D_MODEL)
    for l in range(depth):
        xf = _ffn(xf, ffn1_norm[l].reshape(1, D_MODEL), ffn1_w_gate[l].astype(BF16), ffn1_w_up[l].astype(BF16),
                  ffn1_w_down[l].astype(BF16), gf, False)
        wfm, wtm = _split_w_in(w_in[l])
        wkv = w_kv_up_c[l].astype(BF16)
        (qaT, vaT, qbT, vbT, qcT, qiT, wT, ka, kb, kmean, kc, vcT, ki) = _proj(
            xf, mix_norm[l].reshape(1, D_MODEL), wfm, wtm, kv_norm_c[l].reshape(1, C_KV_LATENT),
            wkv, wkv[:, HEAD_DIM:].T, batch, seq)
        kmean = kmean.reshape(batch, nb, N_HEADS_B, LANES).transpose(0, 2, 1, 3)
        out_a = _swa(attn_sinks[l], qaT, ka, vaT, bias, batch, seq)
        out_b = _moba(qbT, kb, vbT, kmean, bias, batch, seq)
        out_c = _dsa(qiT, wT, ki, qcT, kc, vcT, bias, batch, seq)
        wo = w_out[l].astype(BF16)
        xf = _oproj(xf, out_a.reshape(batch * seq, A_Q), out_b.reshape(batch * seq, B_QKV),
                    out_c.reshape(batch * seq, C_Q), wo[:A_Q], wo[A_Q:A_Q + B_QKV], wo[A_Q + B_QKV:])
        xf = _ffn(xf, ffn2_norm[l].reshape(1, D_MODEL), ffn2_w_gate[l].astype(BF16), ffn2_w_up[l].astype(BF16),
                  ffn2_w_down[l].astype(BF16), gf, l == depth - 1)
    return xf.reshape(batch, seq, D_MODEL)
```

```python
import functools
import math

import jax
import jax.numpy as jnp
import numpy as np
from jax import lax
from jax.experimental import pallas as pl
from jax.experimental.pallas import tpu as pltpu

D_MODEL = 1024
HEAD_DIM = 64
N_HEADS = 16
N_HEADS_A = 8
N_KV_A = 2
WINDOW = 128
N_HEADS_B = 4
MOBA_BLOCK = 256
MOBA_TOPK = 3
N_HEADS_C = 4
C_KV_LATENT = 128
N_IDX_HEADS = 4
IDX_DIM = 64
DSA_TOPK = 256
D_FF = 2816
N_BUCKETS = 32
MAX_DISTANCE = 128
RMS_EPS = 1e-6

A_Q = N_HEADS_A * HEAD_DIM
A_KV = N_KV_A * HEAD_DIM
B_QKV = N_HEADS_B * HEAD_DIM
C_Q = N_HEADS_C * HEAD_DIM
C_QIDX = N_IDX_HEADS * IDX_DIM

TQ = 256
GB = 4
GK = GB * TQ
SUBLANES = 8
LANES = 128
FFN_TM = 1024
FFN_TF = 256
PROJ_TM = GK
VMEM_LIMIT = 56 * 1024 * 1024

F32 = jnp.float32
BF16 = jnp.bfloat16
NEG_INF = float("-inf")
LOG2E = math.log2(math.e)
QK_SCALE = HEAD_DIM ** -0.5 * LOG2E

BIAS_ROWS = 4 * TQ

_NT = (((1,), (1,)), ((), ()))


def _cparams(n_axes):
    return pltpu.CompilerParams(dimension_semantics=("arbitrary",) * n_axes,
                                vmem_limit_bytes=VMEM_LIMIT)


def _dot(a, b):
    return jnp.dot(a, b, preferred_element_type=F32)


def _dot_nt(a, b):
    return lax.dot_general(a, b, _NT, preferred_element_type=F32)


def _rms(x, g):
    return x * lax.rsqrt(jnp.mean(x * x, axis=-1, keepdims=True) + RMS_EPS) * g


def _t5_bucket_np(dist):
    n = np.maximum(dist, 0)
    max_exact = N_BUCKETS // 2
    nf = np.maximum(n, 1).astype(np.float32)
    large = max_exact + (np.log(nf / np.float32(max_exact)) / np.float32(math.log(MAX_DISTANCE / max_exact))
                         * np.float32(N_BUCKETS - max_exact)).astype(np.int32)
    large = np.minimum(large, N_BUCKETS - 1)
    return np.where(n < max_exact, n, large).astype(np.int32)


_IDX_MASKED = -1
_IDX_ZERO = -2


def _bias_index_tiles():
    col = np.arange(TQ)[None, :]
    row = np.arange(TQ)[:, None]
    d_own = col - row
    d_prev = TQ + col - row
    d_prev_w = WINDOW + col - np.arange(WINDOW)[:, None]
    a = np.full((BIAS_ROWS, TQ), _IDX_MASKED, np.int32)
    a[:TQ] = np.where((d_own >= 0) & (d_own < WINDOW), _t5_bucket_np(d_own), _IDX_MASKED)
    a[TQ:TQ + WINDOW] = np.where((d_prev_w >= 0) & (d_prev_w < WINDOW), _t5_bucket_np(d_prev_w), _IDX_MASKED)
    b = np.full((BIAS_ROWS, TQ), _IDX_MASKED, np.int32)
    b[:TQ] = _IDX_ZERO
    b[TQ:2 * TQ] = _t5_bucket_np(d_prev)
    b[2 * TQ:3 * TQ] = np.where(d_own >= 0, _t5_bucket_np(d_own), _IDX_MASKED)
    assert (_t5_bucket_np(np.arange(2 * TQ, 64 * TQ)) == N_BUCKETS - 1).all()
    return np.stack([a, b])


def _bias_kernel(tab_ref, idx_ref, o_ref):
    h = pl.program_id(0)
    idx = idx_ref[0]
    shift = jnp.where(h >= N_HEADS_A, tab_ref[N_BUCKETS - 1, h], 0.0)
    out = jnp.where(idx == _IDX_ZERO, 0.0, NEG_INF)
    for b in range(N_BUCKETS):
        out = jnp.where(idx == b, (tab_ref[b, h] - shift) * LOG2E, out)
    o_ref[0] = out


def _bias_tiles(rel_bias_table):
    idx = jnp.asarray(_bias_index_tiles())
    return pl.pallas_call(
        _bias_kernel,
        grid=(N_HEADS,),
        in_specs=[pl.BlockSpec(memory_space=pltpu.SMEM),
                  pl.BlockSpec((1, BIAS_ROWS, TQ), lambda h: (h // N_HEADS_A, 0, 0))],
        out_specs=pl.BlockSpec((1, BIAS_ROWS, TQ), lambda h: (h, 0, 0)),
        out_shape=jax.ShapeDtypeStruct((N_HEADS, BIAS_ROWS, TQ), F32),
        compiler_params=_cparams(1),
        name="t5_bias_tiles",
    )(rel_bias_table, idx)


def _ffn_kernel(x_ref, g_ref, wg_ref, wu_ref, wd_ref, gf_ref, o_ref, h_ref, acc_ref, *, final_norm):
    k = pl.program_id(1)

    @pl.when(k == 0)
    def _():
        h_ref[...] = _rms(x_ref[...], g_ref[...]).astype(BF16)
        acc_ref[...] = jnp.zeros_like(acc_ref)

    h = h_ref[...]
    gate = _dot(h, wg_ref[...])
    up = _dot(h, wu_ref[...])
    act = (gate * jax.nn.sigmoid(gate) * up).astype(BF16)
    acc_ref[...] += _dot(act, wd_ref[...])

    @pl.when(k == pl.num_programs(1) - 1)
    def _():
        y = x_ref[...] + 0.5 * acc_ref[...]
        if final_norm:
            y = _rms(y, gf_ref[...])
        o_ref[...] = y


def _ffn(x, g, wg, wu, wd, gf, final_norm):
    n = x.shape[0]
    tm = min(FFN_TM, n)
    assert n % tm == 0 and D_FF % FFN_TF == 0
    return pl.pallas_call(
        functools.partial(_ffn_kernel, final_norm=final_norm),
        grid=(n // tm, D_FF // FFN_TF),
        in_specs=[pl.BlockSpec((tm, D_MODEL), lambda i, k: (i, 0)),
                  pl.BlockSpec((1, D_MODEL), lambda i, k: (0, 0)),
                  pl.BlockSpec((D_MODEL, FFN_TF), lambda i, k: (0, k)),
                  pl.BlockSpec((D_MODEL, FFN_TF), lambda i, k: (0, k)),
                  pl.BlockSpec((FFN_TF, D_MODEL), lambda i, k: (k, 0)),
                  pl.BlockSpec((1, D_MODEL), lambda i, k: (0, 0))],
        out_specs=pl.BlockSpec((tm, D_MODEL), lambda i, k: (i, 0)),
        out_shape=jax.ShapeDtypeStruct((n, D_MODEL), F32),
        scratch_shapes=[pltpu.VMEM((tm, D_MODEL), BF16), pltpu.VMEM((tm, D_MODEL), F32)],
        compiler_params=_cparams(2),
        name="swiglu_ffn",
    )(x, g, wg, wu, wd, gf)


_FM_QA, _FM_VA, _FM_QB, _FM_VB, _FM_QC, _FM_QI, _FM_W = 0, 512, 640, 896, 1152, 1408, 1664
_FM_ROWS = 1680
_TM_KA, _TM_KB, _TM_CKV, _TM_KI = 0, 256, 768, 896
_TM_COLS = 1024


def _proj_kernel(x_ref, g_ref, wfm_ref, wtm_ref, gc_ref, wkv_ref, wvT_ref,
                 qa_ref, va_ref, qb_ref, vb_ref, qc_ref, qi_ref, w_ref,
                 ka_ref, kb_ref, kmean_ref, kc_ref, vc_ref, ki_ref):
    tm = x_ref.shape[0]
    nsub = tm // TQ
    h = _rms(x_ref[...], g_ref[...]).astype(BF16)

    def fm(lo, hi, scale=None):
        y = _dot_nt(wfm_ref[lo:hi, :], h)
        return y if scale is None else y * scale

    def put_fm(ref, y):
        for r in range(nsub):
            ref[0, r] = y[:, r * TQ:(r + 1) * TQ].astype(ref.dtype)

    put_fm(qa_ref, fm(_FM_QA, _FM_VA, QK_SCALE))
    put_fm(va_ref, fm(_FM_VA, _FM_QB))
    put_fm(qb_ref, fm(_FM_QB, _FM_VB, QK_SCALE))
    vb_ref[0, 0] = fm(_FM_VB, _FM_QC).astype(BF16)
    put_fm(qc_ref, fm(_FM_QC, _FM_QI, QK_SCALE))
    put_fm(qi_ref, fm(_FM_QI, _FM_W))
    put_fm(w_ref, fm(_FM_W, _FM_ROWS, (N_IDX_HEADS ** -0.5) * (IDX_DIM ** -0.5)))

    ptm = _dot(h, wtm_ref[...])
    for kv in range(N_KV_A):
        lo = _TM_KA + kv * LANES
        ka_ref[0, kv] = ptm[:, lo:lo + HEAD_DIM].astype(BF16)
    for hb in range(N_HEADS_B):
        lo = _TM_KB + hb * LANES
        kb_ref[0, hb] = ptm[:, lo:lo + HEAD_DIM].astype(BF16)
    for r in range(nsub):
        kmean_ref[0, r] = jnp.mean(ptm[r * TQ:(r + 1) * TQ, _TM_KB:_TM_CKV], axis=0, keepdims=True)
    ki_ref[0] = ptm[:, _TM_KI:_TM_KI + IDX_DIM].astype(BF16)

    ckv = _rms(ptm[:, _TM_CKV:_TM_KI], gc_ref[...]).astype(BF16)
    kc_ref[0] = _dot(ckv, wkv_ref[...])[:, :HEAD_DIM].astype(BF16)
    vc_ref[0, 0] = _dot_nt(wvT_ref[...], ckv).astype(BF16)


def _proj(x, g, wfm, wtm, gc, wkv, wvT, batch, seq):
    tm = PROJ_TM
    assert seq % tm == 0 and tm == GK
    tpb = seq // tm
    nsub = tm // TQ
    nb = seq // TQ

    def fm_spec(rows):
        return pl.BlockSpec((1, nsub, rows, TQ), lambda i: (i // tpb, i % tpb, 0, 0))

    def fm_shape(rows, dtype=BF16):
        return jax.ShapeDtypeStruct((batch, nb, rows, TQ), dtype)

    def grp_spec(rows):
        return pl.BlockSpec((1, 1, rows, GK), lambda i: (i // tpb, i % tpb, 0, 0))

    def grp_shape(rows):
        return jax.ShapeDtypeStruct((batch, tpb, rows, GK), BF16)

    def full(a):
        return pl.BlockSpec(a.shape, lambda i: (0,) * a.ndim)

    out_specs = [fm_spec(A_Q), fm_spec(A_KV), fm_spec(B_QKV), grp_spec(B_QKV), fm_spec(C_Q), fm_spec(C_QIDX),
                 fm_spec(16),
                 pl.BlockSpec((1, N_KV_A, tm, HEAD_DIM), lambda i: (i // tpb, 0, i % tpb, 0)),
                 pl.BlockSpec((1, N_HEADS_B, tm, HEAD_DIM), lambda i: (i // tpb, 0, i % tpb, 0)),
                 pl.BlockSpec((1, nsub, 1, N_HEADS_B * LANES), lambda i: (i // tpb, i % tpb, 0, 0)),
                 pl.BlockSpec((1, tm, HEAD_DIM), lambda i: (i // tpb, i % tpb, 0)),
                 grp_spec(HEAD_DIM),
                 pl.BlockSpec((1, tm, IDX_DIM), lambda i: (i // tpb, i % tpb, 0))]
    out_shape = [fm_shape(A_Q), fm_shape(A_KV), fm_shape(B_QKV), grp_shape(B_QKV), fm_shape(C_Q), fm_shape(C_QIDX),
                 fm_shape(16, F32),
                 jax.ShapeDtypeStruct((batch, N_KV_A, seq, HEAD_DIM), BF16),
                 jax.ShapeDtypeStruct((batch, N_HEADS_B, seq, HEAD_DIM), BF16),
                 jax.ShapeDtypeStruct((batch, nb, 1, N_HEADS_B * LANES), F32),
                 jax.ShapeDtypeStruct((batch, seq, HEAD_DIM), BF16),
                 grp_shape(HEAD_DIM),
                 jax.ShapeDtypeStruct((batch, seq, IDX_DIM), BF16)]
    return pl.pallas_call(
        _proj_kernel,
        grid=(batch * tpb,),
        in_specs=[pl.BlockSpec((tm, D_MODEL), lambda i: (i, 0)),
                  full(g), full(wfm), full(wtm), full(gc), full(wkv), full(wvT)],
        out_specs=out_specs,
        out_shape=out_shape,
        compiler_params=_cparams(1),
        name="mix_in_proj",
    )(x, g, wfm, wtm, gc, wkv, wvT)


def _online_update(parts, v_parts, m_ref, l_ref, acc_ref, hh):
    m_prev = m_ref[hh]
    m_new = jnp.maximum(m_prev, jnp.max(functools.reduce(jnp.maximum, parts), axis=0, keepdims=True))
    m_safe = jnp.where(m_new == NEG_INF, 0.0, m_new)
    alpha = jnp.exp2(m_prev - m_safe)
    ps = [jnp.exp2(s - m_safe) for s in parts]
    pv = functools.reduce(lambda a, b: a + b, [_dot(v, p.astype(BF16)) for v, p in zip(v_parts, ps)])
    l_ref[hh] = alpha * l_ref[hh] + jnp.sum(functools.reduce(lambda a, b: a + b, ps), axis=0, keepdims=True)
    acc_ref[hh] = alpha * acc_ref[hh] + pv
    m_ref[hh] = m_new


def _init_stats(m_ref, l_ref, acc_ref, m_init=NEG_INF):
    m_ref[...] = jnp.full(m_ref.shape, m_init, F32)
    l_ref[...] = jnp.zeros(l_ref.shape, F32)
    acc_ref[...] = jnp.zeros(acc_ref.shape, F32)


LAZY_GUARD = 64.0


def _lazy_update(parts, v_parts, m_ref, l_ref, acc_ref, top_ref, bad_ref, hh):
    m_stab = m_ref[hh]
    ps = [jnp.exp2(s - m_stab) for s in parts]
    top = jnp.max(functools.reduce(jnp.maximum, parts), axis=0, keepdims=True)
    pv = functools.reduce(lambda a, b: a + b, [_dot(v, p.astype(BF16)) for v, p in zip(v_parts, ps)])
    m_new = jnp.maximum(m_stab, top)
    beta = jnp.exp2(m_stab - m_new)
    l_ref[hh] = (l_ref[hh] + jnp.sum(functools.reduce(lambda a, b: a + b, ps), axis=0, keepdims=True)) * beta
    acc_ref[hh] = (acc_ref[hh] + pv) * beta
    m_ref[hh] = m_new
    top_ref[hh] = jnp.maximum(top_ref[hh], top)
    bad_ref[...] = jnp.maximum(bad_ref[...], jnp.where(top - m_stab > LAZY_GUARD, 1.0, 0.0))


def _init_lazy(m_ref, l_ref, acc_ref, top_ref, bad_ref):
    _init_stats(m_ref, l_ref, acc_ref, 0.0)
    top_ref[...] = jnp.full(top_ref.shape, NEG_INF, F32)
    bad_ref[...] = jnp.zeros(bad_ref.shape, F32)


def _lazy_failed(top_ref, bad_ref, n_heads):
    low = functools.reduce(jnp.maximum, [jnp.where(top_ref[hh] < -LAZY_GUARD, 1.0, 0.0) for hh in range(n_heads)])
    return jnp.max(jnp.maximum(bad_ref[...], low)) > 0.5


def _finish(l_ref, acc_ref, n_heads):
    oT = jnp.concatenate([acc_ref[hh] * (1.0 / l_ref[hh]) for hh in range(n_heads)], axis=0)
    return oT.T


def _bias_tile(bias_ref, hh, j, i):
    kind = jnp.clip(j - i + 2, 0, 3)
    return bias_ref[hh, pl.ds(pl.multiple_of(kind * TQ, TQ), TQ), :]


def _group_bounds(i):
    n_groups = i // GB + 1
    n_far = jnp.maximum((i - 1) // GB, 0)
    return n_far, n_groups


def _swa_kernel(sink_ref, q_ref, k_ref, v_ref, bias_ref, o_ref):
    g = pl.program_id(1)
    i = pl.program_id(2)
    g_heads = N_HEADS_A // N_KV_A
    start = pl.multiple_of(i * TQ, TQ)
    prev_start = pl.multiple_of(jnp.maximum(i * TQ - WINDOW, 0), WINDOW)
    k_main = k_ref[0, 0, pl.ds(start, TQ), :]
    k_prev = k_ref[0, 0, pl.ds(prev_start, WINDOW), :]
    v_main = v_ref[0, i]
    v_prev = v_ref[0, jnp.maximum(i - 1, 0)][:, TQ - WINDOW:]
    prev_off = jnp.where(i > 0, 0.0, NEG_INF)
    qs = [q_ref[0, 0, hh * HEAD_DIM:(hh + 1) * HEAD_DIM, :] for hh in range(g_heads)]
    s_mains = [_dot(k_main, q) for q in qs]
    s_prevs = [_dot(k_prev, q) for q in qs]
    outs = []
    for hh in range(g_heads):
        s_main = s_mains[hh] + bias_ref[hh, 0:TQ, :]
        s_prev = s_prevs[hh] + bias_ref[hh, TQ:TQ + WINDOW, :] + prev_off
        sink = sink_ref[g * g_heads + hh] * LOG2E
        m = jnp.maximum(jnp.maximum(jnp.max(s_main, axis=0, keepdims=True),
                                    jnp.max(s_prev, axis=0, keepdims=True)), sink)
        p_main = jnp.exp2(s_main - m)
        p_prev = jnp.exp2(s_prev - m)
        denom = (jnp.sum(p_main, axis=0, keepdims=True) + jnp.sum(p_prev, axis=0, keepdims=True)
                 + jnp.exp2(sink - m))
        oT = _dot(v_main, p_main.astype(BF16)) + _dot(v_prev, p_prev.astype(BF16))
        outs.append(oT * (1.0 / denom))
    o_ref[0] = jnp.concatenate(outs, axis=0).T.astype(o_ref.dtype)


def _swa(sinks, qaT, ka, vaT, bias, batch, seq):
    nb = seq // TQ
    g_heads = N_HEADS_A // N_KV_A
    return pl.pallas_call(
        _swa_kernel,
        grid=(batch, N_KV_A, nb),
        in_specs=[pl.BlockSpec(memory_space=pltpu.SMEM),
                  pl.BlockSpec((1, 1, g_heads * HEAD_DIM, TQ), lambda b, g, i: (b, i, g, 0)),
                  pl.BlockSpec((1, 1, seq, HEAD_DIM), lambda b, g, i: (b, g, 0, 0)),
                  pl.BlockSpec((1, nb, HEAD_DIM, TQ), lambda b, g, i: (b, 0, g, 0)),
                  pl.BlockSpec((g_heads, BIAS_ROWS, TQ), lambda b, g, i: (g, 0, 0))],
        out_specs=pl.BlockSpec((1, TQ, g_heads * HEAD_DIM), lambda b, g, i: (b, i, g)),
        out_shape=jax.ShapeDtypeStruct((batch, seq, A_Q), BF16),
        compiler_params=_cparams(3),
        name="swa_attn",
    )(sinks, qaT, ka, vaT, bias)


_MOBA_HPS = 2


def _moba_kernel(q_ref, k_ref, v_ref, kmean_ref, bias_ref, o_ref,
                 sel_ref, m_ref, l_ref, acc_ref, top_ref, bad_ref, *, n_sel):
    i = pl.program_id(2)
    nb = kmean_ref.shape[2]
    row = lax.broadcasted_iota(jnp.int32, (nb, TQ), 0)
    qs = [q_ref[0, 0, hh * HEAD_DIM:(hh + 1) * HEAD_DIM, :] for hh in range(_MOBA_HPS)]

    for hh in range(_MOBA_HPS):
        gate = _dot(kmean_ref[0, hh][:, :HEAD_DIM], qs[hh].astype(F32))
        gate = jnp.where(row < i, gate, NEG_INF)
        sel = row == i
        for _ in range(n_sel):
            best = jnp.max(gate, axis=0, keepdims=True)
            first = jnp.min(jnp.where(gate == best, row, nb), axis=0, keepdims=True)
            pick = (row == first) & (best > NEG_INF)
            sel = sel | pick
            gate = jnp.where(pick, NEG_INF, gate)
        sel_ref[hh] = jnp.where(sel, 0.0, NEG_INF)

    def group(g, mixed, lazy):
        kbase = pl.multiple_of(g * GK, GK)
        s_all = [_dot(k_ref[0, hh, pl.ds(kbase, GK), :], qs[hh]) for hh in range(_MOBA_HPS)]
        for hh in range(_MOBA_HPS):
            parts = []
            for b in range(GB):
                j = g * GB + b
                sb = s_all[hh][b * TQ:(b + 1) * TQ] + sel_ref[hh, pl.ds(j, 1), :]
                if mixed:
                    sb = sb + _bias_tile(bias_ref, hh, j, i)
                parts.append(sb)
            v_parts = [v_ref[0, g, hh * HEAD_DIM:(hh + 1) * HEAD_DIM, b * TQ:(b + 1) * TQ] for b in range(GB)]
            if lazy:
                _lazy_update(parts, v_parts, m_ref, l_ref, acc_ref, top_ref, bad_ref, hh)
            else:
                _online_update(parts, v_parts, m_ref, l_ref, acc_ref, hh)

    n_far, n_groups = _group_bounds(i)

    def all_groups(lazy):
        def far_body(g, carry):
            group(g, False, lazy)
            return carry

        def mixed_body(g, carry):
            group(g, True, lazy)
            return carry

        lax.fori_loop(0, n_far, far_body, 0)
        lax.fori_loop(n_far, n_groups, mixed_body, 0)

    _init_lazy(m_ref, l_ref, acc_ref, top_ref, bad_ref)
    all_groups(True)

    @pl.when(_lazy_failed(top_ref, bad_ref, _MOBA_HPS))
    def _():
        _init_stats(m_ref, l_ref, acc_ref)
        all_groups(False)

    o_ref[0] = _finish(l_ref, acc_ref, _MOBA_HPS).astype(o_ref.dtype)


def _moba(qbT, kb, vbT, kmean, bias, batch, seq):
    nb = seq // TQ
    ng = seq // GK
    hps = _MOBA_HPS
    n_sel = min(MOBA_TOPK, nb - 1)
    bias_blk0 = N_HEADS_A // hps
    return pl.pallas_call(
        functools.partial(_moba_kernel, n_sel=n_sel),
        grid=(batch, N_HEADS_B // hps, nb),
        in_specs=[pl.BlockSpec((1, 1, hps * HEAD_DIM, TQ), lambda b, hp, i: (b, i, hp, 0)),
                  pl.BlockSpec((1, hps, seq, HEAD_DIM), lambda b, hp, i: (b, hp, 0, 0)),
                  pl.BlockSpec((1, ng, hps * HEAD_DIM, GK), lambda b, hp, i: (b, 0, hp, 0)),
                  pl.BlockSpec((1, hps, nb, LANES), lambda b, hp, i: (b, hp, 0, 0)),
                  pl.BlockSpec((hps, BIAS_ROWS, TQ), lambda b, hp, i: (bias_blk0 + hp, 0, 0))],
        out_specs=pl.BlockSpec((1, TQ, hps * HEAD_DIM), lambda b, hp, i: (b, i, hp)),
        out_shape=jax.ShapeDtypeStruct((batch, seq, B_QKV), BF16),
        scratch_shapes=[pltpu.VMEM((hps, nb, TQ), F32),
                        pltpu.VMEM((hps, 1, TQ), F32),
                        pltpu.VMEM((hps, 1, TQ), F32),
                        pltpu.VMEM((hps, HEAD_DIM, TQ), F32),
                        pltpu.VMEM((hps, 1, TQ), F32),
                        pltpu.VMEM((1, TQ), F32)],
        compiler_params=_cparams(3),
        name="moba_attn",
    )(qbT, kb, vbT, kmean, bias)


_PACK16 = 16
_I16_MIN, _I16_MAX = -32768, 32767
_N_COUNT_ACC = 4


def _f32_to_key(x):
    bits = lax.bitcast_convert_type(x, jnp.int32)
    key = bits ^ ((bits >> 31) & jnp.int32(0x7FFFFFFF))
    return jnp.where(key == -1, 0, key)


def _key_to_f32(key):
    bits = key ^ ((key >> 31) & jnp.int32(0x7FFFFFFF))
    return lax.bitcast_convert_type(bits, F32)


_MIN_NORMAL_KEY = 0x00800000


def _next_key(key):
    nxt = key + 1
    nxt = jnp.where((nxt >= -_MIN_NORMAL_KEY) & (nxt < 0), 0, nxt)
    return jnp.where((nxt > 0) & (nxt < _MIN_NORMAL_KEY), _MIN_NORMAL_KEY, nxt)


def _dsa_kernel(qi_ref, w_ref, ki_ref, q_ref, k_ref, v_ref, bias_ref, tri_ref, o_ref,
                score_ref, hi_ref, lo_ref, m_ref, l_ref, acc_ref, ties_ref, top_ref, bad_ref, *, n_keep):
    i = pl.program_id(1)
    n_far, n_groups = _group_bounds(i)

    def rows(g, b=0):
        return pl.ds(pl.multiple_of(g * GK + b * TQ, TQ), TQ)

    w = w_ref[0, 0]
    qis = [qi_ref[0, 0, hi * IDX_DIM:(hi + 1) * IDX_DIM, :] for hi in range(N_IDX_HEADS)]
    krow = lax.broadcasted_iota(jnp.int32, (TQ, TQ), 0)
    qcol = lax.broadcasted_iota(jnp.int32, (TQ, TQ), 1)

    def score_group(g, mixed):
        for b in range(GB):
            ki = ki_ref[0, rows(g, b), :]
            ds = [_dot(ki, qi) for qi in qis]
            sc = functools.reduce(lambda x, y: x + y,
                                  [jnp.maximum(d, 0.0) * w[hi:hi + 1, :] for hi, d in enumerate(ds)])
            if mixed:
                sc = jnp.where(krow + (g * GB + b) * TQ <= qcol + i * TQ, sc, NEG_INF)
            score_ref[rows(g, b), :] = sc
            key = _f32_to_key(sc)
            hi_ref[rows(g, b), :] = (key >> 16).astype(jnp.int16)
            lo_ref[rows(g, b), :] = ((key & 0xFFFF) + _I16_MIN).astype(jnp.int16)

    def score_far(g, carry):
        score_group(g, False)
        return carry

    def score_mixed(g, carry):
        score_group(g, True)
        return carry

    lax.fori_loop(0, n_far, score_far, 0)
    lax.fori_loop(n_far, n_groups, score_mixed, 0)

    def group16(ref, g):
        return ref[pl.ds(pl.multiple_of(g * GK, GK), GK), :]

    def count_ge16(ref, cand):
        cand16 = jnp.broadcast_to(cand.astype(jnp.int16), (_PACK16, TQ))

        def body(g, accs):
            accs = list(accs)
            grp = group16(ref, g)
            for r in range(GK // _PACK16):
                a = r % _N_COUNT_ACC
                hit = grp[r * _PACK16:(r + 1) * _PACK16] >= cand16
                accs[a] = accs[a] + jnp.where(hit, jnp.int16(1), jnp.int16(0))
            return tuple(accs)

        zero = jnp.zeros((_PACK16, TQ), jnp.int16)
        accs = lax.fori_loop(0, n_groups, body, (zero,) * _N_COUNT_ACC)
        total = functools.reduce(lambda x, y: x + y, accs).astype(jnp.int32)
        return jnp.sum(total, axis=0, keepdims=True)

    def count_gt16(ref, cand):
        return jnp.where(cand >= _I16_MAX, 0, count_ge16(ref, jnp.minimum(cand + 1, _I16_MAX)))

    def kth_largest16(ref, k):
        def bit_body(b, c):
            c_try = c + lax.shift_left(jnp.int32(1), 15 - b)
            return jnp.where(count_ge16(ref, c_try) >= k, c_try, c)
        return lax.fori_loop(0, 16, bit_body, jnp.full((1, TQ), _I16_MIN, jnp.int32))

    c_hi = kth_largest16(hi_ref, n_keep)
    n_gt_hi = count_gt16(hi_ref, c_hi)
    c_hi16 = c_hi.astype(jnp.int16)

    def mask_lo(g, carry):
        sl = pl.ds(pl.multiple_of(g * GK, GK), GK)
        lo_ref[sl, :] = jnp.where(hi_ref[sl, :] == c_hi16, lo_ref[sl, :], jnp.int16(_I16_MIN))
        return carry

    lax.fori_loop(0, n_groups, mask_lo, 0)
    c_lo = kth_largest16(lo_ref, n_keep - n_gt_hi)
    key_thr = lax.shift_left(c_hi, 16) | (c_lo - _I16_MIN)
    thr = _key_to_f32(key_thr)
    thr_next = _key_to_f32(_next_key(key_thr))
    n_ge = n_gt_hi + count_ge16(lo_ref, c_lo)
    tie_budget = (n_keep - (n_gt_hi + count_gt16(lo_ref, c_lo))).astype(F32)
    excess_ties = jnp.max(jnp.where((n_ge > n_keep) & (thr > NEG_INF), 1.0, 0.0)) > 0.5

    qs = [q_ref[0, 0, hh * HEAD_DIM:(hh + 1) * HEAD_DIM, :] for hh in range(N_HEADS_C)]

    def attend_group(g, mixed, lazy):
        kc = k_ref[0, pl.ds(pl.multiple_of(g * GK, GK), GK), :]
        s_all = [_dot(kc, q) for q in qs]
        masks = [jnp.where(score_ref[rows(g, b), :] >= thr, 0.0, NEG_INF) for b in range(GB)]
        v_parts = [v_ref[0, g, :, b * TQ:(b + 1) * TQ] for b in range(GB)]
        for hh in range(N_HEADS_C):
            parts = []
            for b in range(GB):
                sb = s_all[hh][b * TQ:(b + 1) * TQ] + masks[b]
                if mixed:
                    sb = sb + _bias_tile(bias_ref, hh, g * GB + b, i)
                parts.append(sb)
            if lazy:
                _lazy_update(parts, v_parts, m_ref, l_ref, acc_ref, top_ref, bad_ref, hh)
            else:
                _online_update(parts, v_parts, m_ref, l_ref, acc_ref, hh)

    def attend_all_groups(lazy):
        def far_body(g, carry):
            attend_group(g, False, lazy)
            return carry

        def mixed_body(g, carry):
            attend_group(g, True, lazy)
            return carry

        lax.fori_loop(0, n_far, far_body, 0)
        lax.fori_loop(n_far, n_groups, mixed_body, 0)

    def attend_group_exact_ties(g, carry):
        for b in range(GB):
            blk = score_ref[rows(g, b), :]
            above = blk >= thr_next
            tie = (blk >= thr) & jnp.logical_not(above)
            tie_f = jnp.where(tie, 1.0, 0.0)
            before = ties_ref[...] + _dot(tri_ref[...], tie_f.astype(BF16))
            keep = above | (tie & (before < tie_budget))
            ties_ref[...] += jnp.sum(tie_f, axis=0, keepdims=True)
            kc = k_ref[0, rows(g, b), :]
            vT = v_ref[0, g, :, b * TQ:(b + 1) * TQ]
            for hh in range(N_HEADS_C):
                s = _dot(kc, qs[hh]) + _bias_tile(bias_ref, hh, g * GB + b, i)
                _online_update([jnp.where(keep, s, NEG_INF)], [vT], m_ref, l_ref, acc_ref, hh)
        return carry

    _init_lazy(m_ref, l_ref, acc_ref, top_ref, bad_ref)

    @pl.when(jnp.logical_not(excess_ties))
    def _():
        attend_all_groups(True)

    @pl.when(jnp.logical_not(excess_ties) & _lazy_failed(top_ref, bad_ref, N_HEADS_C))
    def _():
        _init_stats(m_ref, l_ref, acc_ref)
        attend_all_groups(False)

    @pl.when(excess_ties)
    def _():
        _init_stats(m_ref, l_ref, acc_ref)
        ties_ref[...] = jnp.zeros(ties_ref.shape, F32)
        lax.fori_loop(0, n_groups, attend_group_exact_ties, 0)

    o_ref[0] = _finish(l_ref, acc_ref, N_HEADS_C).astype(o_ref.dtype)


def _dsa(qiT, wT, ki, qcT, kc, vcT, bias, batch, seq):
    nb = seq // TQ
    ng = seq // GK
    n_keep = min(DSA_TOPK, seq // 4)
    tri = jnp.asarray(np.tril(np.ones((TQ, TQ), np.float32), -1), BF16)
    bias_blk = (N_HEADS_A + N_HEADS_B) // N_HEADS_C
    return pl.pallas_call(
        functools.partial(_dsa_kernel, n_keep=n_keep),
        grid=(batch, nb),
        in_specs=[pl.BlockSpec((1, 1, C_QIDX, TQ), lambda b, i: (b, i, 0, 0)),
                  pl.BlockSpec((1, 1, 16, TQ), lambda b, i: (b, i, 0, 0)),
                  pl.BlockSpec((1, seq, IDX_DIM), lambda b, i: (b, 0, 0)),
                  pl.BlockSpec((1, 1, C_Q, TQ), lambda b, i: (b, i, 0, 0)),
                  pl.BlockSpec((1, seq, HEAD_DIM), lambda b, i: (b, 0, 0)),
                  pl.BlockSpec((1, ng, HEAD_DIM, GK), lambda b, i: (b, 0, 0, 0)),
                  pl.BlockSpec((N_HEADS_C, BIAS_ROWS, TQ), lambda b, i: (bias_blk, 0, 0)),
                  pl.BlockSpec((TQ, TQ), lambda b, i: (0, 0))],
        out_specs=pl.BlockSpec((1, TQ, C_Q), lambda b, i: (b, i, 0)),
        out_shape=jax.ShapeDtypeStruct((batch, seq, C_Q), BF16),
        scratch_shapes=[pltpu.VMEM((seq, TQ), F32),
                        pltpu.VMEM((seq, TQ), jnp.int16),
                        pltpu.VMEM((seq, TQ), jnp.int16),
                        pltpu.VMEM((N_HEADS_C, 1, TQ), F32),
                        pltpu.VMEM((N_HEADS_C, 1, TQ), F32),
                        pltpu.VMEM((N_HEADS_C, HEAD_DIM, TQ), F32),
                        pltpu.VMEM((1, TQ), F32),
                        pltpu.VMEM((N_HEADS_C, 1, TQ), F32),
                        pltpu.VMEM((1, TQ), F32)],
        compiler_params=_cparams(2),
        name="dsa_attn",
    )(qiT, wT, ki, qcT, kc, vcT, bias, tri)


def _oproj_kernel(x_ref, a_ref, b_ref, c_ref, wa_ref, wb_ref, wc_ref, o_ref):
    o_ref[...] = (x_ref[...] + _dot(a_ref[...], wa_ref[...]) + _dot(b_ref[...], wb_ref[...])
                  + _dot(c_ref[...], wc_ref[...]))


def _oproj(x, a, b, c, wa, wb, wc):
    n = x.shape[0]
    tm = min(512, n)
    assert n % tm == 0

    def rows(cols):
        return pl.BlockSpec((tm, cols), lambda i: (i, 0))

    def full(w):
        return pl.BlockSpec(w.shape, lambda i: (0, 0))

    return pl.pallas_call(
        _oproj_kernel,
        grid=(n // tm,),
        in_specs=[rows(D_MODEL), rows(A_Q), rows(B_QKV), rows(C_Q), full(wa), full(wb), full(wc)],
        out_specs=rows(D_MODEL),
        out_shape=jax.ShapeDtypeStruct((n, D_MODEL), F32),
        compiler_params=_cparams(1),
        name="mix_out_proj",
    )(x, a, b, c, wa, wb, wc)


def _pad_heads(w, n_heads):
    w = w.reshape(D_MODEL, n_heads, HEAD_DIM)
    return jnp.pad(w, ((0, 0), (0, 0), (0, LANES - HEAD_DIM))).reshape(D_MODEL, n_heads * LANES)


def _split_w_in(w_in):
    cuts = np.cumsum([A_Q, A_KV, A_KV, B_QKV, B_QKV, B_QKV, C_Q, C_KV_LATENT, C_QIDX, IDX_DIM])
    qa, ka, va, qb, kb, vb, qc, ckv, qidx, kidx, widx = jnp.split(w_in, cuts, axis=1)
    wfm = jnp.concatenate([qa, va, qb, vb, qc, qidx, jnp.pad(widx, ((0, 0), (0, 16 - N_IDX_HEADS)))], axis=1)
    wtm = jnp.concatenate([_pad_heads(ka, N_KV_A), _pad_heads(kb, N_HEADS_B), ckv, _pad_heads(kidx, 1)], axis=1)
    assert wfm.shape[1] == _FM_ROWS and wtm.shape[1] == _TM_COLS
    return wfm.T.astype(BF16), wtm.astype(BF16)


def kernel(x, rel_bias_table, ffn1_norm, ffn1_w_gate, ffn1_w_up, ffn1_w_down, mix_norm, w_in, attn_sinks, kv_norm_c, w_kv_up_c, w_out, ffn2_norm, ffn2_w_gate, ffn2_w_up, ffn2_w_down, final_norm):
    batch, seq = x.shape[0], x.shape[1]
    depth = w_in.shape[0]
    assert seq % GK == 0 and x.shape[2] == D_MODEL
    nb = seq // TQ
    bias = _bias_tiles(rel_bias_table)
    gf = final_norm.reshape(1, D_MODEL)
    xf = x.reshape(batch * seq, D_MODEL)
    for l in range(depth):
        xf = _ffn(xf, ffn1_norm[l].reshape(1, D_MODEL), ffn1_w_gate[l].astype(BF16), ffn1_w_up[l].astype(BF16),
                  ffn1_w_down[l].astype(BF16), gf, False)
        wfm, wtm = _split_w_in(w_in[l])
        wkv = w_kv_up_c[l].astype(BF16)
        (qaT, vaT, qbT, vbT, qcT, qiT, wT, ka, kb, kmean, kc, vcT, ki) = _proj(
            xf, mix_norm[l].reshape(1, D_MODEL), wfm, wtm, kv_norm_c[l].reshape(1, C_KV_LATENT),
            wkv, wkv[:, HEAD_DIM:].T, batch, seq)
        kmean = kmean.reshape(batch, nb, N_HEADS_B, LANES).transpose(0, 2, 1, 3)
        out_a = _swa(attn_sinks[l], qaT, ka, vaT, bias, batch, seq)
        out_b = _moba(qbT, kb, vbT, kmean, bias, batch, seq)
        out_c = _dsa(qiT, wT, ki, qcT, kc, vcT, bias, batch, seq)
        wo = w_out[l].astype(BF16)
        xf = _oproj(xf, out_a.reshape(batch * seq, A_Q), out_b.reshape(batch * seq, B_QKV),
                    out_c.reshape(batch * seq, C_Q), wo[:A_Q], wo[A_Q:A_Q + B_QKV], wo[A_Q + B_QKV:])
        xf = _ffn(xf, ffn2_norm[l].reshape(1, D_MODEL), ffn2_w_gate[l].astype(BF16), ffn2_w_up[l].astype(BF16),
                  ffn2_w_down[l].astype(BF16), gf, l == depth - 1)
    return xf.reshape(batch, seq, D_MODEL)
```

```python
import functools
import math

import jax
import jax.numpy as jnp
import numpy as np
from jax import lax
from jax.experimental import pallas as pl
from jax.experimental.pallas import tpu as pltpu

D_MODEL = 1024
HEAD_DIM = 64
N_HEADS = 16
N_HEADS_A = 8
N_KV_A = 2
WINDOW = 128
N_HEADS_B = 4
MOBA_BLOCK = 256
MOBA_TOPK = 3
N_HEADS_C = 4
C_KV_LATENT = 128
N_IDX_HEADS = 4
IDX_DIM = 64
DSA_TOPK = 256
D_FF = 2816
N_BUCKETS = 32
MAX_DISTANCE = 128
RMS_EPS = 1e-6

A_Q = N_HEADS_A * HEAD_DIM
A_KV = N_KV_A * HEAD_DIM
B_QKV = N_HEADS_B * HEAD_DIM
C_Q = N_HEADS_C * HEAD_DIM
C_QIDX = N_IDX_HEADS * IDX_DIM

TQ = 256
GB = 4
GK = GB * TQ
SUBLANES = 8
LANES = 128
FFN_TM = 1024
FFN_TF = 256
PROJ_TM = GK
VMEM_LIMIT = 56 * 1024 * 1024

F32 = jnp.float32
BF16 = jnp.bfloat16
NEG_INF = float("-inf")
LOG2E = math.log2(math.e)
QK_SCALE = HEAD_DIM ** -0.5 * LOG2E

BIAS_ROWS = 4 * TQ

_NT = (((1,), (1,)), ((), ()))


def _cparams(n_axes):
    return pltpu.CompilerParams(dimension_semantics=("arbitrary",) * n_axes,
                                vmem_limit_bytes=VMEM_LIMIT)


def _dot(a, b):
    return jnp.dot(a, b, preferred_element_type=F32)


def _dot_nt(a, b):
    return lax.dot_general(a, b, _NT, preferred_element_type=F32)


def _rms(x, g):
    return x * lax.rsqrt(jnp.mean(x * x, axis=-1, keepdims=True) + RMS_EPS) * g


def _t5_bucket_np(dist):
    n = np.maximum(dist, 0)
    max_exact = N_BUCKETS // 2
    nf = np.maximum(n, 1).astype(np.float32)
    large = max_exact + (np.log(nf / np.float32(max_exact)) / np.float32(math.log(MAX_DISTANCE / max_exact))
                         * np.float32(N_BUCKETS - max_exact)).astype(np.int32)
    large = np.minimum(large, N_BUCKETS - 1)
    return np.where(n < max_exact, n, large).astype(np.int32)


_IDX_MASKED = -1
_IDX_ZERO = -2


def _bias_index_tiles():
    col = np.arange(TQ)[None, :]
    row = np.arange(TQ)[:, None]
    d_own = col - row
    d_prev = TQ + col - row
    d_prev_w = WINDOW + col - np.arange(WINDOW)[:, None]
    a = np.full((BIAS_ROWS, TQ), _IDX_MASKED, np.int32)
    a[:TQ] = np.where((d_own >= 0) & (d_own < WINDOW), _t5_bucket_np(d_own), _IDX_MASKED)
    a[TQ:TQ + WINDOW] = np.where((d_prev_w >= 0) & (d_prev_w < WINDOW), _t5_bucket_np(d_prev_w), _IDX_MASKED)
    b = np.full((BIAS_ROWS, TQ), _IDX_MASKED, np.int32)
    b[:TQ] = _IDX_ZERO
    b[TQ:2 * TQ] = _t5_bucket_np(d_prev)
    b[2 * TQ:3 * TQ] = np.where(d_own >= 0, _t5_bucket_np(d_own), _IDX_MASKED)
    assert (_t5_bucket_np(np.arange(2 * TQ, 64 * TQ)) == N_BUCKETS - 1).all()
    return np.stack([a, b])


def _bias_kernel(tab_ref, idx_ref, o_ref):
    h = pl.program_id(0)
    idx = idx_ref[0]
    shift = jnp.where(h >= N_HEADS_A, tab_ref[N_BUCKETS - 1, h], 0.0)
    out = jnp.where(idx == _IDX_ZERO, 0.0, NEG_INF)
    for b in range(N_BUCKETS):
        out = jnp.where(idx == b, (tab_ref[b, h] - shift) * LOG2E, out)
    o_ref[0] = out


def _bias_tiles(rel_bias_table):
    idx = jnp.asarray(_bias_index_tiles())
    return pl.pallas_call(
        _bias_kernel,
        grid=(N_HEADS,),
        in_specs=[pl.BlockSpec(memory_space=pltpu.SMEM),
                  pl.BlockSpec((1, BIAS_ROWS, TQ), lambda h: (h // N_HEADS_A, 0, 0))],
        out_specs=pl.BlockSpec((1, BIAS_ROWS, TQ), lambda h: (h, 0, 0)),
        out_shape=jax.ShapeDtypeStruct((N_HEADS, BIAS_ROWS, TQ), F32),
        compiler_params=_cparams(1),
        name="t5_bias_tiles",
    )(rel_bias_table, idx)


def _ffn_kernel(x_ref, g_ref, wg_ref, wu_ref, wd_ref, gf_ref, o_ref, h_ref, acc_ref, *, final_norm):
    k = pl.program_id(1)

    @pl.when(k == 0)
    def _():
        h_ref[...] = _rms(x_ref[...], g_ref[...]).astype(BF16)
        acc_ref[...] = jnp.zeros_like(acc_ref)

    h = h_ref[...]
    gate = _dot(h, wg_ref[...])
    up = _dot(h, wu_ref[...])
    act = (gate * jax.nn.sigmoid(gate) * up).astype(BF16)
    acc_ref[...] += _dot(act, wd_ref[...])

    @pl.when(k == pl.num_programs(1) - 1)
    def _():
        y = x_ref[...] + 0.5 * acc_ref[...]
        if final_norm:
            y = _rms(y, gf_ref[...])
        o_ref[...] = y


def _ffn(x, g, wg, wu, wd, gf, final_norm):
    n = x.shape[0]
    tm = min(FFN_TM, n)
    assert n % tm == 0 and D_FF % FFN_TF == 0
    return pl.pallas_call(
        functools.partial(_ffn_kernel, final_norm=final_norm),
        grid=(n // tm, D_FF // FFN_TF),
        in_specs=[pl.BlockSpec((tm, D_MODEL), lambda i, k: (i, 0)),
                  pl.BlockSpec((1, D_MODEL), lambda i, k: (0, 0)),
                  pl.BlockSpec((D_MODEL, FFN_TF), lambda i, k: (0, k)),
                  pl.BlockSpec((D_MODEL, FFN_TF), lambda i, k: (0, k)),
                  pl.BlockSpec((FFN_TF, D_MODEL), lambda i, k: (k, 0)),
                  pl.BlockSpec((1, D_MODEL), lambda i, k: (0, 0))],
        out_specs=pl.BlockSpec((tm, D_MODEL), lambda i, k: (i, 0)),
        out_shape=jax.ShapeDtypeStruct((n, D_MODEL), F32),
        scratch_shapes=[pltpu.VMEM((tm, D_MODEL), BF16), pltpu.VMEM((tm, D_MODEL), F32)],
        compiler_params=_cparams(2),
        name="swiglu_ffn",
    )(x, g, wg, wu, wd, gf)


_FM_QA, _FM_VA, _FM_QB, _FM_VB, _FM_QC, _FM_QI, _FM_W = 0, 512, 640, 896, 1152, 1408, 1664
_FM_ROWS = 1680
_TM_KA, _TM_KB, _TM_CKV, _TM_KI = 0, 256, 768, 896
_TM_COLS = 1024


def _proj_kernel(x_ref, g_ref, wfm_ref, wtm_ref, gc_ref, wkv_ref, wvT_ref,
                 qa_ref, va_ref, qb_ref, vb_ref, qc_ref, qi_ref, w_ref,
                 ka_ref, kb_ref, kmean_ref, kc_ref, vc_ref, ki_ref):
    tm = x_ref.shape[0]
    nsub = tm // TQ
    h = _rms(x_ref[...], g_ref[...]).astype(BF16)

    def fm(lo, hi, scale=None):
        y = _dot_nt(wfm_ref[lo:hi, :], h)
        return y if scale is None else y * scale

    def put_fm(ref, y):
        for r in range(nsub):
            ref[0, r] = y[:, r * TQ:(r + 1) * TQ].astype(ref.dtype)

    put_fm(qa_ref, fm(_FM_QA, _FM_VA, QK_SCALE))
    put_fm(va_ref, fm(_FM_VA, _FM_QB))
    put_fm(qb_ref, fm(_FM_QB, _FM_VB, QK_SCALE))
    vb_ref[0, 0] = fm(_FM_VB, _FM_QC).astype(BF16)
    put_fm(qc_ref, fm(_FM_QC, _FM_QI, QK_SCALE))
    put_fm(qi_ref, fm(_FM_QI, _FM_W))
    put_fm(w_ref, fm(_FM_W, _FM_ROWS, (N_IDX_HEADS ** -0.5) * (IDX_DIM ** -0.5)))

    ptm = _dot(h, wtm_ref[...])
    for kv in range(N_KV_A):
        lo = _TM_KA + kv * LANES
        ka_ref[0, kv] = ptm[:, lo:lo + HEAD_DIM].astype(BF16)
    for hb in range(N_HEADS_B):
        lo = _TM_KB + hb * LANES
        kb_ref[0, hb] = ptm[:, lo:lo + HEAD_DIM].astype(BF16)
    for r in range(nsub):
        kmean_ref[0, r] = jnp.mean(ptm[r * TQ:(r + 1) * TQ, _TM_KB:_TM_CKV], axis=0, keepdims=True)
    ki_ref[0] = ptm[:, _TM_KI:_TM_KI + IDX_DIM].astype(BF16)

    ckv = _rms(ptm[:, _TM_CKV:_TM_KI], gc_ref[...]).astype(BF16)
    kc_ref[0] = _dot(ckv, wkv_ref[...])[:, :HEAD_DIM].astype(BF16)
    vc_ref[0, 0] = _dot_nt(wvT_ref[...], ckv).astype(BF16)


def _proj(x, g, wfm, wtm, gc, wkv, wvT, batch, seq):
    tm = PROJ_TM
    assert seq % tm == 0 and tm == GK
    tpb = seq // tm
    nsub = tm // TQ
    nb = seq // TQ

    def fm_spec(rows):
        return pl.BlockSpec((1, nsub, rows, TQ), lambda i: (i // tpb, i % tpb, 0, 0))

    def fm_shape(rows, dtype=BF16):
        return jax.ShapeDtypeStruct((batch, nb, rows, TQ), dtype)

    def grp_spec(rows):
        return pl.BlockSpec((1, 1, rows, GK), lambda i: (i // tpb, i % tpb, 0, 0))

    def grp_shape(rows):
        return jax.ShapeDtypeStruct((batch, tpb, rows, GK), BF16)

    def full(a):
        return pl.BlockSpec(a.shape, lambda i: (0,) * a.ndim)

    out_specs = [fm_spec(A_Q), fm_spec(A_KV), fm_spec(B_QKV), grp_spec(B_QKV), fm_spec(C_Q), fm_spec(C_QIDX),
                 fm_spec(16),
                 pl.BlockSpec((1, N_KV_A, tm, HEAD_DIM), lambda i: (i // tpb, 0, i % tpb, 0)),
                 pl.BlockSpec((1, N_HEADS_B, tm, HEAD_DIM), lambda i: (i // tpb, 0, i % tpb, 0)),
                 pl.BlockSpec((1, nsub, 1, N_HEADS_B * LANES), lambda i: (i // tpb, i % tpb, 0, 0)),
                 pl.BlockSpec((1, tm, HEAD_DIM), lambda i: (i // tpb, i % tpb, 0)),
                 grp_spec(HEAD_DIM),
                 pl.BlockSpec((1, tm, IDX_DIM), lambda i: (i // tpb, i % tpb, 0))]
    out_shape = [fm_shape(A_Q), fm_shape(A_KV), fm_shape(B_QKV), grp_shape(B_QKV), fm_shape(C_Q), fm_shape(C_QIDX),
                 fm_shape(16, F32),
                 jax.ShapeDtypeStruct((batch, N_KV_A, seq, HEAD_DIM), BF16),
                 jax.ShapeDtypeStruct((batch, N_HEADS_B, seq, HEAD_DIM), BF16),
                 jax.ShapeDtypeStruct((batch, nb, 1, N_HEADS_B * LANES), F32),
                 jax.ShapeDtypeStruct((batch, seq, HEAD_DIM), BF16),
                 grp_shape(HEAD_DIM),
                 jax.ShapeDtypeStruct((batch, seq, IDX_DIM), BF16)]
    return pl.pallas_call(
        _proj_kernel,
        grid=(batch * tpb,),
        in_specs=[pl.BlockSpec((tm, D_MODEL), lambda i: (i, 0)),
                  full(g), full(wfm), full(wtm), full(gc), full(wkv), full(wvT)],
        out_specs=out_specs,
        out_shape=out_shape,
        compiler_params=_cparams(1),
        name="mix_in_proj",
    )(x, g, wfm, wtm, gc, wkv, wvT)


def _online_update(parts, v_parts, m_ref, l_ref, acc_ref, hh):
    m_prev = m_ref[hh]
    m_new = jnp.maximum(m_prev, jnp.max(functools.reduce(jnp.maximum, parts), axis=0, keepdims=True))
    m_safe = jnp.where(m_new == NEG_INF, 0.0, m_new)
    alpha = jnp.exp2(m_prev - m_safe)
    ps = [jnp.exp2(s - m_safe) for s in parts]
    pv = functools.reduce(lambda a, b: a + b, [_dot(v, p.astype(BF16)) for v, p in zip(v_parts, ps)])
    l_ref[hh] = alpha * l_ref[hh] + jnp.sum(functools.reduce(lambda a, b: a + b, ps), axis=0, keepdims=True)
    acc_ref[hh] = alpha * acc_ref[hh] + pv
    m_ref[hh] = m_new


def _init_stats(m_ref, l_ref, acc_ref, m_init=NEG_INF):
    m_ref[...] = jnp.full(m_ref.shape, m_init, F32)
    l_ref[...] = jnp.zeros(l_ref.shape, F32)
    acc_ref[...] = jnp.zeros(acc_ref.shape, F32)


LAZY_GUARD = 64.0


def _lazy_update(parts, v_parts, m_ref, l_ref, acc_ref, top_ref, bad_ref, hh):
    m_stab = m_ref[hh]
    ps = [jnp.exp2(s - m_stab) for s in parts]
    top = jnp.max(functools.reduce(jnp.maximum, parts), axis=0, keepdims=True)
    pv = functools.reduce(lambda a, b: a + b, [_dot(v, p.astype(BF16)) for v, p in zip(v_parts, ps)])
    m_new = jnp.maximum(m_stab, top)
    beta = jnp.exp2(m_stab - m_new)
    l_ref[hh] = (l_ref[hh] + jnp.sum(functools.reduce(lambda a, b: a + b, ps), axis=0, keepdims=True)) * beta
    acc_ref[hh] = (acc_ref[hh] + pv) * beta
    m_ref[hh] = m_new
    top_ref[hh] = jnp.maximum(top_ref[hh], top)
    bad_ref[...] = jnp.maximum(bad_ref[...], jnp.where(top - m_stab > LAZY_GUARD, 1.0, 0.0))


def _init_lazy(m_ref, l_ref, acc_ref, top_ref, bad_ref):
    _init_stats(m_ref, l_ref, acc_ref, 0.0)
    top_ref[...] = jnp.full(top_ref.shape, NEG_INF, F32)
    bad_ref[...] = jnp.zeros(bad_ref.shape, F32)


def _lazy_failed(top_ref, bad_ref, n_heads):
    low = functools.reduce(jnp.maximum, [jnp.where(top_ref[hh] < -LAZY_GUARD, 1.0, 0.0) for hh in range(n_heads)])
    return jnp.max(jnp.maximum(bad_ref[...], low)) > 0.5


def _finish(l_ref, acc_ref, n_heads):
    oT = jnp.concatenate([acc_ref[hh] * (1.0 / l_ref[hh]) for hh in range(n_heads)], axis=0)
    return oT.T


def _bias_tile(bias_ref, hh, j, i):
    kind = jnp.clip(j - i + 2, 0, 3)
    return bias_ref[hh, pl.ds(pl.multiple_of(kind * TQ, TQ), TQ), :]


def _group_bounds(i):
    n_groups = i // GB + 1
    n_far = jnp.maximum((i - 1) // GB, 0)
    return n_far, n_groups


def _swa_kernel(sink_ref, q_ref, k_ref, v_ref, bias_ref, o_ref):
    g = pl.program_id(1)
    i = pl.program_id(2)
    g_heads = N_HEADS_A // N_KV_A
    start = pl.multiple_of(i * TQ, TQ)
    prev_start = pl.multiple_of(jnp.maximum(i * TQ - WINDOW, 0), WINDOW)
    k_main = k_ref[0, 0, pl.ds(start, TQ), :]
    k_prev = k_ref[0, 0, pl.ds(prev_start, WINDOW), :]
    v_main = v_ref[0, i]
    v_prev = v_ref[0, jnp.maximum(i - 1, 0)][:, TQ - WINDOW:]
    prev_off = jnp.where(i > 0, 0.0, NEG_INF)
    qs = [q_ref[0, 0, hh * HEAD_DIM:(hh + 1) * HEAD_DIM, :] for hh in range(g_heads)]
    s_mains = [_dot(k_main, q) for q in qs]
    s_prevs = [_dot(k_prev, q) for q in qs]
    outs = []
    for hh in range(g_heads):
        s_main = s_mains[hh] + bias_ref[hh, 0:TQ, :]
        s_prev = s_prevs[hh] + bias_ref[hh, TQ:TQ + WINDOW, :] + prev_off
        sink = sink_ref[g * g_heads + hh] * LOG2E
        m = jnp.maximum(jnp.maximum(jnp.max(s_main, axis=0, keepdims=True),
                                    jnp.max(s_prev, axis=0, keepdims=True)), sink)
        p_main = jnp.exp2(s_main - m)
        p_prev = jnp.exp2(s_prev - m)
        denom = (jnp.sum(p_main, axis=0, keepdims=True) + jnp.sum(p_prev, axis=0, keepdims=True)
                 + jnp.exp2(sink - m))
        oT = _dot(v_main, p_main.astype(BF16)) + _dot(v_prev, p_prev.astype(BF16))
        outs.append(oT * (1.0 / denom))
    o_ref[0] = jnp.concatenate(outs, axis=0).T.astype(o_ref.dtype)


def _swa(sinks, qaT, ka, vaT, bias, batch, seq):
    nb = seq // TQ
    g_heads = N_HEADS_A // N_KV_A
    return pl.pallas_call(
        _swa_kernel,
        grid=(batch, N_KV_A, nb),
        in_specs=[pl.BlockSpec(memory_space=pltpu.SMEM),
                  pl.BlockSpec((1, 1, g_heads * HEAD_DIM, TQ), lambda b, g, i: (b, i, g, 0)),
                  pl.BlockSpec((1, 1, seq, HEAD_DIM), lambda b, g, i: (b, g, 0, 0)),
                  pl.BlockSpec((1, nb, HEAD_DIM, TQ), lambda b, g, i: (b, 0, g, 0)),
                  pl.BlockSpec((g_heads, BIAS_ROWS, TQ), lambda b, g, i: (g, 0, 0))],
        out_specs=pl.BlockSpec((1, TQ, g_heads * HEAD_DIM), lambda b, g, i: (b, i, g)),
        out_shape=jax.ShapeDtypeStruct((batch, seq, A_Q), BF16),
        compiler_params=_cparams(3),
        name="swa_attn",
    )(sinks, qaT, ka, vaT, bias)


_MOBA_HPS = 2


def _moba_kernel(q_ref, k_ref, v_ref, kmean_ref, bias_ref, o_ref,
                 sel_ref, m_ref, l_ref, acc_ref, top_ref, bad_ref, *, n_sel):
    i = pl.program_id(2)
    nb = kmean_ref.shape[2]
    row = lax.broadcasted_iota(jnp.int32, (nb, TQ), 0)
    qs = [q_ref[0, 0, hh * HEAD_DIM:(hh + 1) * HEAD_DIM, :] for hh in range(_MOBA_HPS)]

    for hh in range(_MOBA_HPS):
        gate = _dot(kmean_ref[0, hh][:, :HEAD_DIM], qs[hh].astype(F32))
        gate = jnp.where(row < i, gate, NEG_INF)
        sel = row == i
        for _ in range(n_sel):
            best = jnp.max(gate, axis=0, keepdims=True)
            first = jnp.min(jnp.where(gate == best, row, nb), axis=0, keepdims=True)
            pick = (row == first) & (best > NEG_INF)
            sel = sel | pick
            gate = jnp.where(pick, NEG_INF, gate)
        sel_ref[hh] = jnp.where(sel, 0.0, NEG_INF)

    def group(g, mixed, lazy):
        kbase = pl.multiple_of(g * GK, GK)
        s_all = [_dot(k_ref[0, hh, pl.ds(kbase, GK), :], qs[hh]) for hh in range(_MOBA_HPS)]
        for hh in range(_MOBA_HPS):
            parts = []
            for b in range(GB):
                j = g * GB + b
                sb = s_all[hh][b * TQ:(b + 1) * TQ] + sel_ref[hh, pl.ds(j, 1), :]
                if mixed:
                    sb = sb + _bias_tile(bias_ref, hh, j, i)
                parts.append(sb)
            v_parts = [v_ref[0, g, hh * HEAD_DIM:(hh + 1) * HEAD_DIM, b * TQ:(b + 1) * TQ] for b in range(GB)]
            if lazy:
                _lazy_update(parts, v_parts, m_ref, l_ref, acc_ref, top_ref, bad_ref, hh)
            else:
                _online_update(parts, v_parts, m_ref, l_ref, acc_ref, hh)

    n_far, n_groups = _group_bounds(i)

    def all_groups(lazy):
        def far_body(g, carry):
            group(g, False, lazy)
            return carry

        def mixed_body(g, carry):
            group(g, True, lazy)
            return carry

        lax.fori_loop(0, n_far, far_body, 0)
        lax.fori_loop(n_far, n_groups, mixed_body, 0)

    _init_lazy(m_ref, l_ref, acc_ref, top_ref, bad_ref)
    all_groups(True)

    @pl.when(_lazy_failed(top_ref, bad_ref, _MOBA_HPS))
    def _():
        _init_stats(m_ref, l_ref, acc_ref)
        all_groups(False)

    o_ref[0] = _finish(l_ref, acc_ref, _MOBA_HPS).astype(o_ref.dtype)


def _moba(qbT, kb, vbT, kmean, bias, batch, seq):
    nb = seq // TQ
    ng = seq // GK
    hps = _MOBA_HPS
    n_sel = min(MOBA_TOPK, nb - 1)
    bias_blk0 = N_HEADS_A // hps
    return pl.pallas_call(
        functools.partial(_moba_kernel, n_sel=n_sel),
        grid=(batch, N_HEADS_B // hps, nb),
        in_specs=[pl.BlockSpec((1, 1, hps * HEAD_DIM, TQ), lambda b, hp, i: (b, i, hp, 0)),
                  pl.BlockSpec((1, hps, seq, HEAD_DIM), lambda b, hp, i: (b, hp, 0, 0)),
                  pl.BlockSpec((1, ng, hps * HEAD_DIM, GK), lambda b, hp, i: (b, 0, hp, 0)),
                  pl.BlockSpec((1, hps, nb, LANES), lambda b, hp, i: (b, hp, 0, 0)),
                  pl.BlockSpec((hps, BIAS_ROWS, TQ), lambda b, hp, i: (bias_blk0 + hp, 0, 0))],
        out_specs=pl.BlockSpec((1, TQ, hps * HEAD_DIM), lambda b, hp, i: (b, i, hp)),
        out_shape=jax.ShapeDtypeStruct((batch, seq, B_QKV), BF16),
        scratch_shapes=[pltpu.VMEM((hps, nb, TQ), F32),
                        pltpu.VMEM((hps, 1, TQ), F32),
                        pltpu.VMEM((hps, 1, TQ), F32),
                        pltpu.VMEM((hps, HEAD_DIM, TQ), F32),
                        pltpu.VMEM((hps, 1, TQ), F32),
                        pltpu.VMEM((1, TQ), F32)],
        compiler_params=_cparams(3),
        name="moba_attn",
    )(qbT, kb, vbT, kmean, bias)


_PACK16 = 16
_I16_MIN, _I16_MAX = -32768, 32767
_N_COUNT_ACC = 4


def _f32_to_key(x):
    bits = lax.bitcast_convert_type(x, jnp.int32)
    key = bits ^ ((bits >> 31) & jnp.int32(0x7FFFFFFF))
    return jnp.where(key == -1, 0, key)


def _key_to_f32(key):
    bits = key ^ ((key >> 31) & jnp.int32(0x7FFFFFFF))
    return lax.bitcast_convert_type(bits, F32)


_MIN_NORMAL_KEY = 0x00800000


def _next_key(key):
    nxt = key + 1
    nxt = jnp.where((nxt >= -_MIN_NORMAL_KEY) & (nxt < 0), 0, nxt)
    return jnp.where((nxt > 0) & (nxt < _MIN_NORMAL_KEY), _MIN_NORMAL_KEY, nxt)


def _dsa_kernel(qi_ref, w_ref, ki_ref, q_ref, k_ref, v_ref, bias_ref, tri_ref, o_ref,
                score_ref, hi_ref, lo_ref, m_ref, l_ref, acc_ref, ties_ref, top_ref, bad_ref, *, n_keep):
    i = pl.program_id(1)
    n_far, n_groups = _group_bounds(i)

    def rows(g, b=0):
        return pl.ds(pl.multiple_of(g * GK + b * TQ, TQ), TQ)

    w = w_ref[0, 0]
    qis = [qi_ref[0, 0, hi * IDX_DIM:(hi + 1) * IDX_DIM, :] for hi in range(N_IDX_HEADS)]
    krow = lax.broadcasted_iota(jnp.int32, (TQ, TQ), 0)
    qcol = lax.broadcasted_iota(jnp.int32, (TQ, TQ), 1)

    def score_group(g, mixed):
        for b in range(GB):
            ki = ki_ref[0, rows(g, b), :]
            ds = [_dot(ki, qi) for qi in qis]
            sc = functools.reduce(lambda x, y: x + y,
                                  [jnp.maximum(d, 0.0) * w[hi:hi + 1, :] for hi, d in enumerate(ds)])
            if mixed:
                sc = jnp.where(krow + (g * GB + b) * TQ <= qcol + i * TQ, sc, NEG_INF)
            score_ref[rows(g, b), :] = sc
            key = _f32_to_key(sc)
            hi_ref[rows(g, b), :] = (key >> 16).astype(jnp.int16)
            lo_ref[rows(g, b), :] = ((key & 0xFFFF) + _I16_MIN).astype(jnp.int16)

    def score_far(g, carry):
        score_group(g, False)
        return carry

    def score_mixed(g, carry):
        score_group(g, True)
        return carry

    lax.fori_loop(0, n_far, score_far, 0)
    lax.fori_loop(n_far, n_groups, score_mixed, 0)

    def group16(ref, g):
        return ref[pl.ds(pl.multiple_of(g * GK, GK), GK), :]

    def count_ge16(ref, cand):
        cand16 = jnp.broadcast_to(cand.astype(jnp.int16), (_PACK16, TQ))

        def body(g, accs):
            accs = list(accs)
            grp = group16(ref, g)
            for r in range(GK // _PACK16):
                a = r % _N_COUNT_ACC
                hit = grp[r * _PACK16:(r + 1) * _PACK16] >= cand16
                accs[a] = accs[a] + jnp.where(hit, jnp.int16(1), jnp.int16(0))
            return tuple(accs)

        zero = jnp.zeros((_PACK16, TQ), jnp.int16)
        accs = lax.fori_loop(0, n_groups, body, (zero,) * _N_COUNT_ACC)
        total = functools.reduce(lambda x, y: x + y, accs).astype(jnp.int32)
        return jnp.sum(total, axis=0, keepdims=True)

    def count_gt16(ref, cand):
        return jnp.where(cand >= _I16_MAX, 0, count_ge16(ref, jnp.minimum(cand + 1, _I16_MAX)))

    def kth_largest16(ref, k):
        def bit_body(b, c):
            c_try = c + lax.shift_left(jnp.int32(1), 15 - b)
            return jnp.where(count_ge16(ref, c_try) >= k, c_try, c)
        return lax.fori_loop(0, 16, bit_body, jnp.full((1, TQ), _I16_MIN, jnp.int32))

    c_hi = kth_largest16(hi_ref, n_keep)
    n_gt_hi = count_gt16(hi_ref, c_hi)
    c_hi16 = c_hi.astype(jnp.int16)

    def mask_lo(g, carry):
        sl = pl.ds(pl.multiple_of(g * GK, GK), GK)
        lo_ref[sl, :] = jnp.where(hi_ref[sl, :] == c_hi16, lo_ref[sl, :], jnp.int16(_I16_MIN))
        return carry

    lax.fori_loop(0, n_groups, mask_lo, 0)
    c_lo = kth_largest16(lo_ref, n_keep - n_gt_hi)
    key_thr = lax.shift_left(c_hi, 16) | (c_lo - _I16_MIN)
    thr = _key_to_f32(key_thr)
    thr_next = _key_to_f32(_next_key(key_thr))
    tie_budget = (n_keep - (n_gt_hi + count_gt16(lo_ref, c_lo))).astype(F32)

    qs = [q_ref[0, 0, hh * HEAD_DIM:(hh + 1) * HEAD_DIM, :] for hh in range(N_HEADS_C)]

    def selection_masks(g):
        blks = [score_ref[rows(g, b), :] for b in range(GB)]
        ties = [jnp.where(blk == thr, 1.0, 0.0) for blk in blks]
        prefix = [_dot(tri_ref[...], tie.astype(BF16)) for tie in ties]
        masks = []
        seen = ties_ref[...]
        for blk, tie, pre in zip(blks, ties, prefix):
            bar = jnp.where(seen + pre < tie_budget, thr, thr_next)
            masks.append(jnp.where(blk >= bar, 0.0, NEG_INF))
            seen = seen + pre[TQ - 1:TQ, :] + tie[TQ - 1:TQ, :]
        ties_ref[...] = seen
        return masks

    def attend_group(g, mixed, lazy):
        masks = selection_masks(g)
        kc = k_ref[0, pl.ds(pl.multiple_of(g * GK, GK), GK), :]
        s_all = [_dot(kc, q) for q in qs]
        v_parts = [v_ref[0, g, :, b * TQ:(b + 1) * TQ] for b in range(GB)]
        for hh in range(N_HEADS_C):
            parts = []
            for b in range(GB):
                sb = s_all[hh][b * TQ:(b + 1) * TQ] + masks[b]
                if mixed:
                    sb = sb + _bias_tile(bias_ref, hh, g * GB + b, i)
                parts.append(sb)
            if lazy:
                _lazy_update(parts, v_parts, m_ref, l_ref, acc_ref, top_ref, bad_ref, hh)
            else:
                _online_update(parts, v_parts, m_ref, l_ref, acc_ref, hh)

    def attend_all_groups(lazy):
        def far_body(g, carry):
            attend_group(g, False, lazy)
            return carry

        def mixed_body(g, carry):
            attend_group(g, True, lazy)
            return carry

        ties_ref[...] = jnp.zeros(ties_ref.shape, F32)
        lax.fori_loop(0, n_far, far_body, 0)
        lax.fori_loop(n_far, n_groups, mixed_body, 0)

    _init_lazy(m_ref, l_ref, acc_ref, top_ref, bad_ref)
    attend_all_groups(True)

    @pl.when(_lazy_failed(top_ref, bad_ref, N_HEADS_C))
    def _():
        _init_stats(m_ref, l_ref, acc_ref)
        attend_all_groups(False)

    o_ref[0] = _finish(l_ref, acc_ref, N_HEADS_C).astype(o_ref.dtype)


def _dsa(qiT, wT, ki, qcT, kc, vcT, bias, batch, seq):
    nb = seq // TQ
    ng = seq // GK
    n_keep = min(DSA_TOPK, seq // 4)
    tri = jnp.asarray(np.tril(np.ones((TQ, TQ), np.float32), -1), BF16)
    bias_blk = (N_HEADS_A + N_HEADS_B) // N_HEADS_C
    return pl.pallas_call(
        functools.partial(_dsa_kernel, n_keep=n_keep),
        grid=(batch, nb),
        in_specs=[pl.BlockSpec((1, 1, C_QIDX, TQ), lambda b, i: (b, i, 0, 0)),
                  pl.BlockSpec((1, 1, 16, TQ), lambda b, i: (b, i, 0, 0)),
                  pl.BlockSpec((1, seq, IDX_DIM), lambda b, i: (b, 0, 0)),
                  pl.BlockSpec((1, 1, C_Q, TQ), lambda b, i: (b, i, 0, 0)),
                  pl.BlockSpec((1, seq, HEAD_DIM), lambda b, i: (b, 0, 0)),
                  pl.BlockSpec((1, ng, HEAD_DIM, GK), lambda b, i: (b, 0, 0, 0)),
                  pl.BlockSpec((N_HEADS_C, BIAS_ROWS, TQ), lambda b, i: (bias_blk, 0, 0)),
                  pl.BlockSpec((TQ, TQ), lambda b, i: (0, 0))],
        out_specs=pl.BlockSpec((1, TQ, C_Q), lambda b, i: (b, i, 0)),
        out_shape=jax.ShapeDtypeStruct((batch, seq, C_Q), BF16),
        scratch_shapes=[pltpu.VMEM((seq, TQ), F32),
                        pltpu.VMEM((seq, TQ), jnp.int16),
                        pltpu.VMEM((seq, TQ), jnp.int16),
                        pltpu.VMEM((N_HEADS_C, 1, TQ), F32),
                        pltpu.VMEM((N_HEADS_C, 1, TQ), F32),
                        pltpu.VMEM((N_HEADS_C, HEAD_DIM, TQ), F32),
                        pltpu.VMEM((1, TQ), F32),
                        pltpu.VMEM((N_HEADS_C, 1, TQ), F32),
                        pltpu.VMEM((1, TQ), F32)],
        compiler_params=_cparams(2),
        name="dsa_attn",
    )(qiT, wT, ki, qcT, kc, vcT, bias, tri)


def _oproj_kernel(x_ref, a_ref, b_ref, c_ref, wa_ref, wb_ref, wc_ref, o_ref):
    o_ref[...] = (x_ref[...] + _dot(a_ref[...], wa_ref[...]) + _dot(b_ref[...], wb_ref[...])
                  + _dot(c_ref[...], wc_ref[...]))


def _oproj(x, a, b, c, wa, wb, wc):
    n = x.shape[0]
    tm = min(512, n)
    assert n % tm == 0

    def rows(cols):
        return pl.BlockSpec((tm, cols), lambda i: (i, 0))

    def full(w):
        return pl.BlockSpec(w.shape, lambda i: (0, 0))

    return pl.pallas_call(
        _oproj_kernel,
        grid=(n // tm,),
        in_specs=[rows(D_MODEL), rows(A_Q), rows(B_QKV), rows(C_Q), full(wa), full(wb), full(wc)],
        out_specs=rows(D_MODEL),
        out_shape=jax.ShapeDtypeStruct((n, D_MODEL), F32),
        compiler_params=_cparams(1),
        name="mix_out_proj",
    )(x, a, b, c, wa, wb, wc)


def _pad_heads(w, n_heads):
    w = w.reshape(D_MODEL, n_heads, HEAD_DIM)
    return jnp.pad(w, ((0, 0), (0, 0), (0, LANES - HEAD_DIM))).reshape(D_MODEL, n_heads * LANES)


def _split_w_in(w_in):
    cuts = np.cumsum([A_Q, A_KV, A_KV, B_QKV, B_QKV, B_QKV, C_Q, C_KV_LATENT, C_QIDX, IDX_DIM])
    qa, ka, va, qb, kb, vb, qc, ckv, qidx, kidx, widx = jnp.split(w_in, cuts, axis=1)
    wfm = jnp.concatenate([qa, va, qb, vb, qc, qidx, jnp.pad(widx, ((0, 0), (0, 16 - N_IDX_HEADS)))], axis=1)
    wtm = jnp.concatenate([_pad_heads(ka, N_KV_A), _pad_heads(kb, N_HEADS_B), ckv, _pad_heads(kidx, 1)], axis=1)
    assert wfm.shape[1] == _FM_ROWS and wtm.shape[1] == _TM_COLS
    return wfm.T.astype(BF16), wtm.astype(BF16)


def kernel(x, rel_bias_table, ffn1_norm, ffn1_w_gate, ffn1_w_up, ffn1_w_down, mix_norm, w_in, attn_sinks, kv_norm_c, w_kv_up_c, w_out, ffn2_norm, ffn2_w_gate, ffn2_w_up, ffn2_w_down, final_norm):
    batch, seq = x.shape[0], x.shape[1]
    depth = w_in.shape[0]
    assert seq % GK == 0 and x.shape[2] == D_MODEL
    nb = seq // TQ
    bias = _bias_tiles(rel_bias_table)
    gf = final_norm.reshape(1, D_MODEL)
    xf = x.reshape(batch * seq, D_MODEL)
    for l in range(depth):
        xf = _ffn(xf, ffn1_norm[l].reshape(1, D_MODEL), ffn1_w_gate[l].astype(BF16), ffn1_w_up[l].astype(BF16),
                  ffn1_w_down[l].astype(BF16), gf, False)
        wfm, wtm = _split_w_in(w_in[l])
        wkv = w_kv_up_c[l].astype(BF16)
        (qaT, vaT, qbT, vbT, qcT, qiT, wT, ka, kb, kmean, kc, vcT, ki) = _proj(
            xf, mix_norm[l].reshape(1, D_MODEL), wfm, wtm, kv_norm_c[l].reshape(1, C_KV_LATENT),
            wkv, wkv[:, HEAD_DIM:].T, batch, seq)
        kmean = kmean.reshape(batch, nb, N_HEADS_B, LANES).transpose(0, 2, 1, 3)
        out_a = _swa(attn_sinks[l], qaT, ka, vaT, bias, batch, seq)
        out_b = _moba(qbT, kb, vbT, kmean, bias, batch, seq)
        out_c = _dsa(qiT, wT, ki, qcT, kc, vcT, bias, batch, seq)
        wo = w_out[l].astype(BF16)
        xf = _oproj(xf, out_a.reshape(batch * seq, A_Q), out_b.reshape(batch * seq, B_QKV),
                    out_c.reshape(batch * seq, C_Q), wo[:A_Q], wo[A_Q:A_Q + B_QKV], wo[A_Q + B_QKV:])
        xf = _ffn(xf, ffn2_norm[l].reshape(1, D_MODEL), ffn2_w_gate[l].astype(BF16), ffn2_w_up[l].astype(BF16),
                  ffn2_w_down[l].astype(BF16), gf, l == depth - 1)
    return xf.reshape(batch, seq, D_MODEL)
```

```python
import functools
import math

import jax
import jax.numpy as jnp
import numpy as np
from jax import lax
from jax.experimental import pallas as pl
from jax.experimental.pallas import tpu as pltpu

D_MODEL = 1024
HEAD_DIM = 64
N_HEADS = 16
N_HEADS_A = 8
N_KV_A = 2
WINDOW = 128
N_HEADS_B = 4
MOBA_BLOCK = 256
MOBA_TOPK = 3
N_HEADS_C = 4
C_KV_LATENT = 128
N_IDX_HEADS = 4
IDX_DIM = 64
DSA_TOPK = 256
D_FF = 2816
N_BUCKETS = 32
MAX_DISTANCE = 128
RMS_EPS = 1e-6

A_Q = N_HEADS_A * HEAD_DIM
A_KV = N_KV_A * HEAD_DIM
B_QKV = N_HEADS_B * HEAD_DIM
C_Q = N_HEADS_C * HEAD_DIM
C_QIDX = N_IDX_HEADS * IDX_DIM

TQ = 256
GB = 4
GK = GB * TQ
SUBLANES = 8
LANES = 128
FFN_TM = 1024
FFN_TF = 256
PROJ_TM = GK
VMEM_LIMIT = 56 * 1024 * 1024

F32 = jnp.float32
BF16 = jnp.bfloat16
NEG_INF = float("-inf")
LOG2E = math.log2(math.e)
QK_SCALE = HEAD_DIM ** -0.5 * LOG2E

BIAS_ROWS = 4 * TQ

_NT = (((1,), (1,)), ((), ()))


def _cparams(n_axes):
    return pltpu.CompilerParams(dimension_semantics=("arbitrary",) * n_axes,
                                vmem_limit_bytes=VMEM_LIMIT)


def _dot(a, b):
    return jnp.dot(a, b, preferred_element_type=F32)


def _dot_nt(a, b):
    return lax.dot_general(a, b, _NT, preferred_element_type=F32)


def _rms(x, g):
    return x * lax.rsqrt(jnp.mean(x * x, axis=-1, keepdims=True) + RMS_EPS) * g


def _t5_bucket_np(dist):
    n = np.maximum(dist, 0)
    max_exact = N_BUCKETS // 2
    nf = np.maximum(n, 1).astype(np.float32)
    large = max_exact + (np.log(nf / np.float32(max_exact)) / np.float32(math.log(MAX_DISTANCE / max_exact))
                         * np.float32(N_BUCKETS - max_exact)).astype(np.int32)
    large = np.minimum(large, N_BUCKETS - 1)
    return np.where(n < max_exact, n, large).astype(np.int32)


_IDX_MASKED = -1
_IDX_ZERO = -2


def _bias_index_tiles():
    col = np.arange(TQ)[None, :]
    row = np.arange(TQ)[:, None]
    d_own = col - row
    d_prev = TQ + col - row
    d_prev_w = WINDOW + col - np.arange(WINDOW)[:, None]
    a = np.full((BIAS_ROWS, TQ), _IDX_MASKED, np.int32)
    a[:TQ] = np.where((d_own >= 0) & (d_own < WINDOW), _t5_bucket_np(d_own), _IDX_MASKED)
    a[TQ:TQ + WINDOW] = np.where((d_prev_w >= 0) & (d_prev_w < WINDOW), _t5_bucket_np(d_prev_w), _IDX_MASKED)
    b = np.full((BIAS_ROWS, TQ), _IDX_MASKED, np.int32)
    b[:TQ] = _IDX_ZERO
    b[TQ:2 * TQ] = _t5_bucket_np(d_prev)
    b[2 * TQ:3 * TQ] = np.where(d_own >= 0, _t5_bucket_np(d_own), _IDX_MASKED)
    assert (_t5_bucket_np(np.arange(2 * TQ, 64 * TQ)) == N_BUCKETS - 1).all()
    return np.stack([a, b])


def _bias_kernel(tab_ref, idx_ref, o_ref):
    h = pl.program_id(0)
    idx = idx_ref[0]
    shift = jnp.where(h >= N_HEADS_A, tab_ref[N_BUCKETS - 1, h], 0.0)
    out = jnp.where(idx == _IDX_ZERO, 0.0, NEG_INF)
    for b in range(N_BUCKETS):
        out = jnp.where(idx == b, (tab_ref[b, h] - shift) * LOG2E, out)
    o_ref[0] = out


def _bias_tiles(rel_bias_table):
    idx = jnp.asarray(_bias_index_tiles())
    return pl.pallas_call(
        _bias_kernel,
        grid=(N_HEADS,),
        in_specs=[pl.BlockSpec(memory_space=pltpu.SMEM),
                  pl.BlockSpec((1, BIAS_ROWS, TQ), lambda h: (h // N_HEADS_A, 0, 0))],
        out_specs=pl.BlockSpec((1, BIAS_ROWS, TQ), lambda h: (h, 0, 0)),
        out_shape=jax.ShapeDtypeStruct((N_HEADS, BIAS_ROWS, TQ), F32),
        compiler_params=_cparams(1),
        name="t5_bias_tiles",
    )(rel_bias_table, idx)


def _ffn_kernel(x_ref, g_ref, wg_ref, wu_ref, wd_ref, gf_ref, o_ref, h_ref, act_ref, *, final_norm):
    k = pl.program_id(1)
    n_chunks = act_ref.shape[0]

    @pl.when(k == 0)
    def _():
        h_ref[...] = _rms(x_ref[...], g_ref[...]).astype(BF16)

    h = h_ref[...]
    gate = _dot(h, wg_ref[...])
    up = _dot(h, wu_ref[...])
    act_ref[k] = (gate * jax.nn.sigmoid(gate) * up).astype(BF16)

    @pl.when(k == n_chunks - 1)
    def _():
        down = functools.reduce(lambda a, b: a + b,
                                [_dot(act_ref[c], wd_ref[c * FFN_TF:(c + 1) * FFN_TF, :]) for c in range(n_chunks)])
        y = x_ref[...] + 0.5 * down
        if final_norm:
            y = _rms(y, gf_ref[...])
        o_ref[...] = y


def _ffn(x, g, wg, wu, wd, gf, final_norm):
    n = x.shape[0]
    tm = min(FFN_TM, n)
    assert n % tm == 0 and D_FF % FFN_TF == 0
    return pl.pallas_call(
        functools.partial(_ffn_kernel, final_norm=final_norm),
        grid=(n // tm, D_FF // FFN_TF),
        in_specs=[pl.BlockSpec((tm, D_MODEL), lambda i, k: (i, 0)),
                  pl.BlockSpec((1, D_MODEL), lambda i, k: (0, 0)),
                  pl.BlockSpec((D_MODEL, FFN_TF), lambda i, k: (0, k)),
                  pl.BlockSpec((D_MODEL, FFN_TF), lambda i, k: (0, k)),
                  pl.BlockSpec((D_FF, D_MODEL), lambda i, k: (0, 0)),
                  pl.BlockSpec((1, D_MODEL), lambda i, k: (0, 0))],
        out_specs=pl.BlockSpec((tm, D_MODEL), lambda i, k: (i, 0)),
        out_shape=jax.ShapeDtypeStruct((n, D_MODEL), F32),
        scratch_shapes=[pltpu.VMEM((tm, D_MODEL), BF16), pltpu.VMEM((D_FF // FFN_TF, tm, FFN_TF), BF16)],
        compiler_params=_cparams(2),
        name="swiglu_ffn",
    )(x, g, wg, wu, wd, gf)


_FM_QA, _FM_VA, _FM_QB, _FM_VB, _FM_QC, _FM_QI, _FM_W = 0, 512, 640, 896, 1152, 1408, 1664
_FM_ROWS = 1680
_TM_KA, _TM_KB, _TM_CKV, _TM_KI = 0, 256, 768, 896
_TM_COLS = 1024


def _proj_kernel(x_ref, g_ref, wfm_ref, wtm_ref, gc_ref, wkv_ref, wvT_ref,
                 qa_ref, va_ref, qb_ref, vb_ref, qc_ref, qi_ref, w_ref,
                 ka_ref, kb_ref, kmean_ref, kc_ref, vc_ref, ki_ref):
    tm = x_ref.shape[0]
    nsub = tm // TQ
    h = _rms(x_ref[...], g_ref[...]).astype(BF16)

    def fm(lo, hi, scale=None):
        y = _dot_nt(wfm_ref[lo:hi, :], h)
        return y if scale is None else y * scale

    def put_fm(ref, y):
        for r in range(nsub):
            ref[0, r] = y[:, r * TQ:(r + 1) * TQ].astype(ref.dtype)

    put_fm(qa_ref, fm(_FM_QA, _FM_VA, QK_SCALE))
    put_fm(va_ref, fm(_FM_VA, _FM_QB))
    put_fm(qb_ref, fm(_FM_QB, _FM_VB, QK_SCALE))
    vb_ref[0, 0] = fm(_FM_VB, _FM_QC).astype(BF16)
    put_fm(qc_ref, fm(_FM_QC, _FM_QI, QK_SCALE))
    put_fm(qi_ref, fm(_FM_QI, _FM_W))
    put_fm(w_ref, fm(_FM_W, _FM_ROWS, (N_IDX_HEADS ** -0.5) * (IDX_DIM ** -0.5)))

    ptm = _dot(h, wtm_ref[...])
    for kv in range(N_KV_A):
        lo = _TM_KA + kv * LANES
        ka_ref[0, kv] = ptm[:, lo:lo + HEAD_DIM].astype(BF16)
    for hb in range(N_HEADS_B):
        lo = _TM_KB + hb * LANES
        kb_ref[0, hb] = ptm[:, lo:lo + HEAD_DIM].astype(BF16)
    for r in range(nsub):
        kmean_ref[0, r] = jnp.mean(ptm[r * TQ:(r + 1) * TQ, _TM_KB:_TM_CKV], axis=0, keepdims=True)
    ki_ref[0] = ptm[:, _TM_KI:_TM_KI + IDX_DIM].astype(BF16)

    ckv = _rms(ptm[:, _TM_CKV:_TM_KI], gc_ref[...]).astype(BF16)
    kc_ref[0] = _dot(ckv, wkv_ref[...])[:, :HEAD_DIM].astype(BF16)
    vc_ref[0, 0] = _dot_nt(wvT_ref[...], ckv).astype(BF16)


def _proj(x, g, wfm, wtm, gc, wkv, wvT, batch, seq):
    tm = PROJ_TM
    assert seq % tm == 0 and tm == GK
    tpb = seq // tm
    nsub = tm // TQ
    nb = seq // TQ

    def fm_spec(rows):
        return pl.BlockSpec((1, nsub, rows, TQ), lambda i: (i // tpb, i % tpb, 0, 0))

    def fm_shape(rows, dtype=BF16):
        return jax.ShapeDtypeStruct((batch, nb, rows, TQ), dtype)

    def grp_spec(rows):
        return pl.BlockSpec((1, 1, rows, GK), lambda i: (i // tpb, i % tpb, 0, 0))

    def grp_shape(rows):
        return jax.ShapeDtypeStruct((batch, tpb, rows, GK), BF16)

    def full(a):
        return pl.BlockSpec(a.shape, lambda i: (0,) * a.ndim)

    out_specs = [fm_spec(A_Q), fm_spec(A_KV), fm_spec(B_QKV), grp_spec(B_QKV), fm_spec(C_Q), fm_spec(C_QIDX),
                 fm_spec(16),
                 pl.BlockSpec((1, N_KV_A, tm, HEAD_DIM), lambda i: (i // tpb, 0, i % tpb, 0)),
                 pl.BlockSpec((1, N_HEADS_B, tm, HEAD_DIM), lambda i: (i // tpb, 0, i % tpb, 0)),
                 pl.BlockSpec((1, nsub, 1, N_HEADS_B * LANES), lambda i: (i // tpb, i % tpb, 0, 0)),
                 pl.BlockSpec((1, tm, HEAD_DIM), lambda i: (i // tpb, i % tpb, 0)),
                 grp_spec(HEAD_DIM),
                 pl.BlockSpec((1, tm, IDX_DIM), lambda i: (i // tpb, i % tpb, 0))]
    out_shape = [fm_shape(A_Q), fm_shape(A_KV), fm_shape(B_QKV), grp_shape(B_QKV), fm_shape(C_Q), fm_shape(C_QIDX),
                 fm_shape(16, F32),
                 jax.ShapeDtypeStruct((batch, N_KV_A, seq, HEAD_DIM), BF16),
                 jax.ShapeDtypeStruct((batch, N_HEADS_B, seq, HEAD_DIM), BF16),
                 jax.ShapeDtypeStruct((batch, nb, 1, N_HEADS_B * LANES), F32),
                 jax.ShapeDtypeStruct((batch, seq, HEAD_DIM), BF16),
                 grp_shape(HEAD_DIM),
                 jax.ShapeDtypeStruct((batch, seq, IDX_DIM), BF16)]
    return pl.pallas_call(
        _proj_kernel,
        grid=(batch * tpb,),
        in_specs=[pl.BlockSpec((tm, D_MODEL), lambda i: (i, 0)),
                  full(g), full(wfm), full(wtm), full(gc), full(wkv), full(wvT)],
        out_specs=out_specs,
        out_shape=out_shape,
        compiler_params=_cparams(1),
        name="mix_in_proj",
    )(x, g, wfm, wtm, gc, wkv, wvT)


def _online_update(parts, v_parts, m_ref, l_ref, acc_ref, hh):
    m_prev = m_ref[hh]
    m_new = jnp.maximum(m_prev, jnp.max(functools.reduce(jnp.maximum, parts), axis=0, keepdims=True))
    m_safe = jnp.where(m_new == NEG_INF, 0.0, m_new)
    alpha = jnp.exp2(m_prev - m_safe)
    ps = [jnp.exp2(s - m_safe) for s in parts]
    pv = functools.reduce(lambda a, b: a + b, [_dot(v, p.astype(BF16)) for v, p in zip(v_parts, ps)])
    l_ref[hh] = alpha * l_ref[hh] + jnp.sum(functools.reduce(lambda a, b: a + b, ps), axis=0, keepdims=True)
    acc_ref[hh] = alpha * acc_ref[hh] + pv
    m_ref[hh] = m_new


def _init_stats(m_ref, l_ref, acc_ref, m_init=NEG_INF):
    m_ref[...] = jnp.full(m_ref.shape, m_init, F32)
    l_ref[...] = jnp.zeros(l_ref.shape, F32)
    acc_ref[...] = jnp.zeros(acc_ref.shape, F32)


LAZY_GUARD = 64.0


def _lazy_update(parts, v_parts, m_ref, l_ref, acc_ref, top_ref, bad_ref, hh):
    m_stab = m_ref[hh]
    ps = [jnp.exp2(s - m_stab) for s in parts]
    top = jnp.max(functools.reduce(jnp.maximum, parts), axis=0, keepdims=True)
    pv = functools.reduce(lambda a, b: a + b, [_dot(v, p.astype(BF16)) for v, p in zip(v_parts, ps)])
    m_new = jnp.maximum(m_stab, top)
    beta = jnp.exp2(m_stab - m_new)
    l_ref[hh] = (l_ref[hh] + jnp.sum(functools.reduce(lambda a, b: a + b, ps), axis=0, keepdims=True)) * beta
    acc_ref[hh] = (acc_ref[hh] + pv) * beta
    m_ref[hh] = m_new
    top_ref[hh] = jnp.maximum(top_ref[hh], top)
    bad_ref[...] = jnp.maximum(bad_ref[...], jnp.where(top - m_stab > LAZY_GUARD, 1.0, 0.0))


def _init_lazy(m_ref, l_ref, acc_ref, top_ref, bad_ref):
    _init_stats(m_ref, l_ref, acc_ref, 0.0)
    top_ref[...] = jnp.full(top_ref.shape, NEG_INF, F32)
    bad_ref[...] = jnp.zeros(bad_ref.shape, F32)


def _lazy_failed(top_ref, bad_ref, n_heads):
    low = functools.reduce(jnp.maximum, [jnp.where(top_ref[hh] < -LAZY_GUARD, 1.0, 0.0) for hh in range(n_heads)])
    return jnp.max(jnp.maximum(bad_ref[...], low)) > 0.5


def _finish(l_ref, acc_ref, n_heads):
    oT = jnp.concatenate([acc_ref[hh] * (1.0 / l_ref[hh]) for hh in range(n_heads)], axis=0)
    return oT.T


def _bias_tile(bias_ref, hh, j, i):
    kind = jnp.clip(j - i + 2, 0, 3)
    return bias_ref[hh, pl.ds(pl.multiple_of(kind * TQ, TQ), TQ), :]


def _group_bounds(i):
    n_groups = i // GB + 1
    n_far = jnp.maximum((i - 1) // GB, 0)
    return n_far, n_groups


def _swa_kernel(sink_ref, q_ref, k_ref, v_ref, bias_ref, o_ref):
    g = pl.program_id(1)
    i = pl.program_id(2)
    g_heads = N_HEADS_A // N_KV_A
    start = pl.multiple_of(i * TQ, TQ)
    prev_start = pl.multiple_of(jnp.maximum(i * TQ - WINDOW, 0), WINDOW)
    k_main = k_ref[0, 0, pl.ds(start, TQ), :]
    k_prev = k_ref[0, 0, pl.ds(prev_start, WINDOW), :]
    v_main = v_ref[0, i]
    v_prev = v_ref[0, jnp.maximum(i - 1, 0)][:, TQ - WINDOW:]
    prev_off = jnp.where(i > 0, 0.0, NEG_INF)
    qs = [q_ref[0, 0, hh * HEAD_DIM:(hh + 1) * HEAD_DIM, :] for hh in range(g_heads)]

    def scores(hh):
        s_main = _dot(k_main, qs[hh]) + bias_ref[hh, 0:TQ, :]
        s_prev = _dot(k_prev, qs[hh]) + bias_ref[hh, TQ:TQ + WINDOW, :] + prev_off
        return s_main, s_prev, sink_ref[g * g_heads + hh] * LOG2E

    def head_out(s_main, s_prev, sink, m):
        p_main = jnp.exp2(s_main - m)
        p_prev = jnp.exp2(s_prev - m)
        denom = (jnp.sum(p_main, axis=0, keepdims=True) + jnp.sum(p_prev, axis=0, keepdims=True)
                 + jnp.exp2(sink - m))
        oT = _dot(v_main, p_main.astype(BF16)) + _dot(v_prev, p_prev.astype(BF16))
        return oT * (1.0 / denom)

    def tile_max(s_main, s_prev):
        return jnp.maximum(jnp.max(s_main, axis=0, keepdims=True), jnp.max(s_prev, axis=0, keepdims=True))

    outs, over = [], None
    for hh in range(g_heads):
        s_main, s_prev, sink = scores(hh)
        outs.append(head_out(s_main, s_prev, sink, sink))
        gap = tile_max(s_main, s_prev) - sink
        over = gap if over is None else jnp.maximum(over, gap)
    o_ref[0] = jnp.concatenate(outs, axis=0).T.astype(o_ref.dtype)

    @pl.when(jnp.max(over) > LAZY_GUARD)
    def _():
        outs = []
        for hh in range(g_heads):
            s_main, s_prev, sink = scores(hh)
            outs.append(head_out(s_main, s_prev, sink, jnp.maximum(tile_max(s_main, s_prev), sink)))
        o_ref[0] = jnp.concatenate(outs, axis=0).T.astype(o_ref.dtype)


def _swa(sinks, qaT, ka, vaT, bias, batch, seq):
    nb = seq // TQ
    g_heads = N_HEADS_A // N_KV_A
    return pl.pallas_call(
        _swa_kernel,
        grid=(batch, N_KV_A, nb),
        in_specs=[pl.BlockSpec(memory_space=pltpu.SMEM),
                  pl.BlockSpec((1, 1, g_heads * HEAD_DIM, TQ), lambda b, g, i: (b, i, g, 0)),
                  pl.BlockSpec((1, 1, seq, HEAD_DIM), lambda b, g, i: (b, g, 0, 0)),
                  pl.BlockSpec((1, nb, HEAD_DIM, TQ), lambda b, g, i: (b, 0, g, 0)),
                  pl.BlockSpec((g_heads, BIAS_ROWS, TQ), lambda b, g, i: (g, 0, 0))],
        out_specs=pl.BlockSpec((1, TQ, g_heads * HEAD_DIM), lambda b, g, i: (b, i, g)),
        out_shape=jax.ShapeDtypeStruct((batch, seq, A_Q), BF16),
        compiler_params=_cparams(3),
        name="swa_attn",
    )(sinks, qaT, ka, vaT, bias)


_MOBA_HPS = 2


def _moba_kernel(q_ref, k_ref, v_ref, kmean_ref, bias_ref, o_ref,
                 sel_ref, m_ref, l_ref, acc_ref, top_ref, bad_ref, *, n_sel):
    i = pl.program_id(2)
    nb = kmean_ref.shape[2]
    row = lax.broadcasted_iota(jnp.int32, (nb, TQ), 0)
    qs = [q_ref[0, 0, hh * HEAD_DIM:(hh + 1) * HEAD_DIM, :] for hh in range(_MOBA_HPS)]

    for hh in range(_MOBA_HPS):
        gate = _dot(kmean_ref[0, hh][:, :HEAD_DIM], qs[hh].astype(F32))
        gate = jnp.where(row < i, gate, NEG_INF)
        sel = row == i
        for _ in range(n_sel):
            best = jnp.max(gate, axis=0, keepdims=True)
            first = jnp.min(jnp.where(gate == best, row, nb), axis=0, keepdims=True)
            pick = (row == first) & (best > NEG_INF)
            sel = sel | pick
            gate = jnp.where(pick, NEG_INF, gate)
        sel_ref[hh] = jnp.where(sel, 0.0, NEG_INF)

    def group(g, mixed, lazy):
        kbase = pl.multiple_of(g * GK, GK)
        s_all = [_dot(k_ref[0, hh, pl.ds(kbase, GK), :], qs[hh]) for hh in range(_MOBA_HPS)]
        for hh in range(_MOBA_HPS):
            parts = []
            for b in range(GB):
                j = g * GB + b
                sb = s_all[hh][b * TQ:(b + 1) * TQ] + sel_ref[hh, pl.ds(j, 1), :]
                if mixed:
                    sb = sb + _bias_tile(bias_ref, hh, j, i)
                parts.append(sb)
            v_parts = [v_ref[0, g, hh * HEAD_DIM:(hh + 1) * HEAD_DIM, b * TQ:(b + 1) * TQ] for b in range(GB)]
            if lazy:
                _lazy_update(parts, v_parts, m_ref, l_ref, acc_ref, top_ref, bad_ref, hh)
            else:
                _online_update(parts, v_parts, m_ref, l_ref, acc_ref, hh)

    n_far, n_groups = _group_bounds(i)

    def all_groups(lazy):
        def far_body(g, carry):
            group(g, False, lazy)
            return carry

        def mixed_body(g, carry):
            group(g, True, lazy)
            return carry

        lax.fori_loop(0, n_far, far_body, 0)
        lax.fori_loop(n_far, n_groups, mixed_body, 0)

    _init_lazy(m_ref, l_ref, acc_ref, top_ref, bad_ref)
    all_groups(True)

    @pl.when(_lazy_failed(top_ref, bad_ref, _MOBA_HPS))
    def _():
        _init_stats(m_ref, l_ref, acc_ref)
        all_groups(False)

    o_ref[0] = _finish(l_ref, acc_ref, _MOBA_HPS).astype(o_ref.dtype)


def _moba(qbT, kb, vbT, kmean, bias, batch, seq):
    nb = seq // TQ
    ng = seq // GK
    hps = _MOBA_HPS
    n_sel = min(MOBA_TOPK, nb - 1)
    bias_blk0 = N_HEADS_A // hps
    return pl.pallas_call(
        functools.partial(_moba_kernel, n_sel=n_sel),
        grid=(batch, N_HEADS_B // hps, nb),
        in_specs=[pl.BlockSpec((1, 1, hps * HEAD_DIM, TQ), lambda b, hp, i: (b, i, hp, 0)),
                  pl.BlockSpec((1, hps, seq, HEAD_DIM), lambda b, hp, i: (b, hp, 0, 0)),
                  pl.BlockSpec((1, ng, hps * HEAD_DIM, GK), lambda b, hp, i: (b, 0, hp, 0)),
                  pl.BlockSpec((1, hps, nb, LANES), lambda b, hp, i: (b, hp, 0, 0)),
                  pl.BlockSpec((hps, BIAS_ROWS, TQ), lambda b, hp, i: (bias_blk0 + hp, 0, 0))],
        out_specs=pl.BlockSpec((1, TQ, hps * HEAD_DIM), lambda b, hp, i: (b, i, hp)),
        out_shape=jax.ShapeDtypeStruct((batch, seq, B_QKV), BF16),
        scratch_shapes=[pltpu.VMEM((hps, nb, TQ), F32),
                        pltpu.VMEM((hps, 1, TQ), F32),
                        pltpu.VMEM((hps, 1, TQ), F32),
                        pltpu.VMEM((hps, HEAD_DIM, TQ), F32),
                        pltpu.VMEM((hps, 1, TQ), F32),
                        pltpu.VMEM((1, TQ), F32)],
        compiler_params=_cparams(3),
        name="moba_attn",
    )(qbT, kb, vbT, kmean, bias)


_PACK16 = 16
_I16_MIN, _I16_MAX = -32768, 32767
_N_COUNT_ACC = 4


def _f32_to_key(x):
    bits = lax.bitcast_convert_type(x, jnp.int32)
    key = bits ^ ((bits >> 31) & jnp.int32(0x7FFFFFFF))
    return jnp.where(key == -1, 0, key)


def _key_to_f32(key):
    bits = key ^ ((key >> 31) & jnp.int32(0x7FFFFFFF))
    return lax.bitcast_convert_type(bits, F32)


_MIN_NORMAL_KEY = 0x00800000


def _next_key(key):
    nxt = key + 1
    nxt = jnp.where((nxt >= -_MIN_NORMAL_KEY) & (nxt < 0), 0, nxt)
    return jnp.where((nxt > 0) & (nxt < _MIN_NORMAL_KEY), _MIN_NORMAL_KEY, nxt)


def _dsa_kernel(qi_ref, w_ref, ki_ref, q_ref, k_ref, v_ref, bias_ref, tri_ref, o_ref,
                score_ref, hi_ref, lo_ref, m_ref, l_ref, acc_ref, ties_ref, top_ref, bad_ref, *, n_keep):
    i = pl.program_id(1)
    n_far, n_groups = _group_bounds(i)

    def rows(g, b=0):
        return pl.ds(pl.multiple_of(g * GK + b * TQ, TQ), TQ)

    w = w_ref[0, 0]
    qis = [qi_ref[0, 0, hi * IDX_DIM:(hi + 1) * IDX_DIM, :] for hi in range(N_IDX_HEADS)]
    krow = lax.broadcasted_iota(jnp.int32, (TQ, TQ), 0)
    qcol = lax.broadcasted_iota(jnp.int32, (TQ, TQ), 1)

    def score_group(g, mixed):
        for b in range(GB):
            ki = ki_ref[0, rows(g, b), :]
            ds = [_dot(ki, qi) for qi in qis]
            sc = functools.reduce(lambda x, y: x + y,
                                  [jnp.maximum(d, 0.0) * w[hi:hi + 1, :] for hi, d in enumerate(ds)])
            if mixed:
                sc = jnp.where(krow + (g * GB + b) * TQ <= qcol + i * TQ, sc, NEG_INF)
            score_ref[rows(g, b), :] = sc
            key = _f32_to_key(sc)
            hi_ref[rows(g, b), :] = (key >> 16).astype(jnp.int16)
            lo_ref[rows(g, b), :] = ((key & 0xFFFF) + _I16_MIN).astype(jnp.int16)

    def score_far(g, carry):
        score_group(g, False)
        return carry

    def score_mixed(g, carry):
        score_group(g, True)
        return carry

    lax.fori_loop(0, n_far, score_far, 0)
    lax.fori_loop(n_far, n_groups, score_mixed, 0)

    def group16(ref, g):
        return ref[pl.ds(pl.multiple_of(g * GK, GK), GK), :]

    def count_ge16(ref, cand):
        cand16 = jnp.broadcast_to(cand.astype(jnp.int16), (_PACK16, TQ))

        def body(g, accs):
            accs = list(accs)
            grp = group16(ref, g)
            for r in range(GK // _PACK16):
                a = r % _N_COUNT_ACC
                hit = grp[r * _PACK16:(r + 1) * _PACK16] >= cand16
                accs[a] = accs[a] + jnp.where(hit, jnp.int16(1), jnp.int16(0))
            return tuple(accs)

        zero = jnp.zeros((_PACK16, TQ), jnp.int16)
        accs = lax.fori_loop(0, n_groups, body, (zero,) * _N_COUNT_ACC)
        total = functools.reduce(lambda x, y: x + y, accs).astype(jnp.int32)
        return jnp.sum(total, axis=0, keepdims=True)

    def kth_largest16(ref, k):
        def bit_body(b, carry):
            c, n_gt = carry
            c_try = c + lax.shift_left(jnp.int32(1), 15 - b)
            n = count_ge16(ref, c_try)
            ok = n >= k
            return jnp.where(ok, c_try, c), jnp.where(ok, n_gt, n)
        init = (jnp.full((1, TQ), _I16_MIN, jnp.int32), jnp.zeros((1, TQ), jnp.int32))
        return lax.fori_loop(0, 16, bit_body, init)

    c_hi, n_gt_hi = kth_largest16(hi_ref, n_keep)
    c_hi16 = c_hi.astype(jnp.int16)

    def mask_lo(g, carry):
        sl = pl.ds(pl.multiple_of(g * GK, GK), GK)
        lo_ref[sl, :] = jnp.where(hi_ref[sl, :] == c_hi16, lo_ref[sl, :], jnp.int16(_I16_MIN))
        return carry

    lax.fori_loop(0, n_groups, mask_lo, 0)
    c_lo, n_gt_lo = kth_largest16(lo_ref, n_keep - n_gt_hi)
    key_thr = lax.shift_left(c_hi, 16) | (c_lo - _I16_MIN)
    thr = _key_to_f32(key_thr)
    thr_next = _key_to_f32(_next_key(key_thr))
    tie_budget = (n_keep - (n_gt_hi + n_gt_lo)).astype(F32)

    qs = [q_ref[0, 0, hh * HEAD_DIM:(hh + 1) * HEAD_DIM, :] for hh in range(N_HEADS_C)]

    def selection_masks(g):
        blks = [score_ref[rows(g, b), :] for b in range(GB)]
        ties = [jnp.where(blk == thr, 1.0, 0.0) for blk in blks]
        prefix = [_dot(tri_ref[...], tie.astype(BF16)) for tie in ties]
        masks = []
        seen = ties_ref[...]
        for blk, tie, pre in zip(blks, ties, prefix):
            bar = jnp.where(seen + pre < tie_budget, thr, thr_next)
            masks.append(jnp.where(blk >= bar, 0.0, NEG_INF))
            seen = seen + pre[TQ - 1:TQ, :] + tie[TQ - 1:TQ, :]
        ties_ref[...] = seen
        return masks

    def attend_group(g, mixed, lazy):
        masks = selection_masks(g)
        kc = k_ref[0, pl.ds(pl.multiple_of(g * GK, GK), GK), :]
        s_all = [_dot(kc, q) for q in qs]
        v_parts = [v_ref[0, g, :, b * TQ:(b + 1) * TQ] for b in range(GB)]
        for hh in range(N_HEADS_C):
            parts = []
            for b in range(GB):
                sb = s_all[hh][b * TQ:(b + 1) * TQ] + masks[b]
                if mixed:
                    sb = sb + _bias_tile(bias_ref, hh, g * GB + b, i)
                parts.append(sb)
            if lazy:
                _lazy_update(parts, v_parts, m_ref, l_ref, acc_ref, top_ref, bad_ref, hh)
            else:
                _online_update(parts, v_parts, m_ref, l_ref, acc_ref, hh)

    def attend_all_groups(lazy):
        def far_body(g, carry):
            attend_group(g, False, lazy)
            return carry

        def mixed_body(g, carry):
            attend_group(g, True, lazy)
            return carry

        ties_ref[...] = jnp.zeros(ties_ref.shape, F32)
        lax.fori_loop(0, n_far, far_body, 0)
        lax.fori_loop(n_far, n_groups, mixed_body, 0)

    _init_lazy(m_ref, l_ref, acc_ref, top_ref, bad_ref)
    attend_all_groups(True)

    @pl.when(_lazy_failed(top_ref, bad_ref, N_HEADS_C))
    def _():
        _init_stats(m_ref, l_ref, acc_ref)
        attend_all_groups(False)

    o_ref[0] = _finish(l_ref, acc_ref, N_HEADS_C).astype(o_ref.dtype)


def _dsa(qiT, wT, ki, qcT, kc, vcT, bias, batch, seq):
    nb = seq // TQ
    ng = seq // GK
    n_keep = min(DSA_TOPK, seq // 4)
    tri = jnp.asarray(np.tril(np.ones((TQ, TQ), np.float32), -1), BF16)
    bias_blk = (N_HEADS_A + N_HEADS_B) // N_HEADS_C
    return pl.pallas_call(
        functools.partial(_dsa_kernel, n_keep=n_keep),
        grid=(batch, nb),
        in_specs=[pl.BlockSpec((1, 1, C_QIDX, TQ), lambda b, i: (b, i, 0, 0)),
                  pl.BlockSpec((1, 1, 16, TQ), lambda b, i: (b, i, 0, 0)),
                  pl.BlockSpec((1, seq, IDX_DIM), lambda b, i: (b, 0, 0)),
                  pl.BlockSpec((1, 1, C_Q, TQ), lambda b, i: (b, i, 0, 0)),
                  pl.BlockSpec((1, seq, HEAD_DIM), lambda b, i: (b, 0, 0)),
                  pl.BlockSpec((1, ng, HEAD_DIM, GK), lambda b, i: (b, 0, 0, 0)),
                  pl.BlockSpec((N_HEADS_C, BIAS_ROWS, TQ), lambda b, i: (bias_blk, 0, 0)),
                  pl.BlockSpec((TQ, TQ), lambda b, i: (0, 0))],
        out_specs=pl.BlockSpec((1, TQ, C_Q), lambda b, i: (b, i, 0)),
        out_shape=jax.ShapeDtypeStruct((batch, seq, C_Q), BF16),
        scratch_shapes=[pltpu.VMEM((seq, TQ), F32),
                        pltpu.VMEM((seq, TQ), jnp.int16),
                        pltpu.VMEM((seq, TQ), jnp.int16),
                        pltpu.VMEM((N_HEADS_C, 1, TQ), F32),
                        pltpu.VMEM((N_HEADS_C, 1, TQ), F32),
                        pltpu.VMEM((N_HEADS_C, HEAD_DIM, TQ), F32),
                        pltpu.VMEM((1, TQ), F32),
                        pltpu.VMEM((N_HEADS_C, 1, TQ), F32),
                        pltpu.VMEM((1, TQ), F32)],
        compiler_params=_cparams(2),
        name="dsa_attn",
    )(qiT, wT, ki, qcT, kc, vcT, bias, tri)


def _oproj_kernel(x_ref, a_ref, b_ref, c_ref, wa_ref, wb_ref, wc_ref, o_ref):
    o_ref[...] = (x_ref[...] + _dot(a_ref[...], wa_ref[...]) + _dot(b_ref[...], wb_ref[...])
                  + _dot(c_ref[...], wc_ref[...]))


def _oproj(x, a, b, c, wa, wb, wc):
    n = x.shape[0]
    tm = min(512, n)
    assert n % tm == 0

    def rows(cols):
        return pl.BlockSpec((tm, cols), lambda i: (i, 0))

    def full(w):
        return pl.BlockSpec(w.shape, lambda i: (0, 0))

    return pl.pallas_call(
        _oproj_kernel,
        grid=(n // tm,),
        in_specs=[rows(D_MODEL), rows(A_Q), rows(B_QKV), rows(C_Q), full(wa), full(wb), full(wc)],
        out_specs=rows(D_MODEL),
        out_shape=jax.ShapeDtypeStruct((n, D_MODEL), F32),
        compiler_params=_cparams(1),
        name="mix_out_proj",
    )(x, a, b, c, wa, wb, wc)


def _pad_heads(w, n_heads):
    w = w.reshape(D_MODEL, n_heads, HEAD_DIM)
    return jnp.pad(w, ((0, 0), (0, 0), (0, LANES - HEAD_DIM))).reshape(D_MODEL, n_heads * LANES)


def _split_w_in(w_in):
    cuts = np.cumsum([A_Q, A_KV, A_KV, B_QKV, B_QKV, B_QKV, C_Q, C_KV_LATENT, C_QIDX, IDX_DIM])
    qa, ka, va, qb, kb, vb, qc, ckv, qidx, kidx, widx = jnp.split(w_in, cuts, axis=1)
    wfm = jnp.concatenate([qa, va, qb, vb, qc, qidx, jnp.pad(widx, ((0, 0), (0, 16 - N_IDX_HEADS)))], axis=1)
    wtm = jnp.concatenate([_pad_heads(ka, N_KV_A), _pad_heads(kb, N_HEADS_B), ckv, _pad_heads(kidx, 1)], axis=1)
    assert wfm.shape[1] == _FM_ROWS and wtm.shape[1] == _TM_COLS
    return wfm.T.astype(BF16), wtm.astype(BF16)


def kernel(x, rel_bias_table, ffn1_norm, ffn1_w_gate, ffn1_w_up, ffn1_w_down, mix_norm, w_in, attn_sinks, kv_norm_c, w_kv_up_c, w_out, ffn2_norm, ffn2_w_gate, ffn2_w_up, ffn2_w_down, final_norm):
    batch, seq = x.shape[0], x.shape[1]
    depth = w_in.shape[0]
    assert seq % GK == 0 and x.shape[2] == D_MODEL
    nb = seq // TQ
    bias = _bias_tiles(rel_bias_table)
    gf = final_norm.reshape(1, D_MODEL)
    xf = x.reshape(batch * seq, D_MODEL)
    for l in range(depth):
        xf = _ffn(xf, ffn1_norm[l].reshape(1, D_MODEL), ffn1_w_gate[l].astype(BF16), ffn1_w_up[l].astype(BF16),
                  ffn1_w_down[l].astype(BF16), gf, False)
        wfm, wtm = _split_w_in(w_in[l])
        wkv = w_kv_up_c[l].astype(BF16)
        (qaT, vaT, qbT, vbT, qcT, qiT, wT, ka, kb, kmean, kc, vcT, ki) = _proj(
            xf, mix_norm[l].reshape(1, D_MODEL), wfm, wtm, kv_norm_c[l].reshape(1, C_KV_LATENT),
            wkv, wkv[:, HEAD_DIM:].T, batch, seq)
        kmean = kmean.reshape(batch, nb, N_HEADS_B, LANES).transpose(0, 2, 1, 3)
        out_a = _swa(attn_sinks[l], qaT, ka, vaT, bias, batch, seq)
        out_b = _moba(qbT, kb, vbT, kmean, bias, batch, seq)
        out_c = _dsa(qiT, wT, ki, qcT, kc, vcT, bias, batch, seq)
        wo = w_out[l].astype(BF16)
        xf = _oproj(xf, out_a.reshape(batch * seq, A_Q), out_b.reshape(batch * seq, B_QKV),
                    out_c.reshape(batch * seq, C_Q), wo[:A_Q], wo[A_Q:A_Q + B_QKV], wo[A_Q + B_QKV:])
        xf = _ffn(xf, ffn2_norm[l].reshape(1, D_MODEL), ffn2_w_gate[l].astype(BF16), ffn2_w_up[l].astype(BF16),
                  ffn2_w_down[l].astype(BF16), gf, l == depth - 1)
    return xf.reshape(batch, seq, D_MODEL)
```

```python
import functools
import math

import jax
import jax.numpy as jnp
import numpy as np
from jax import lax
from jax.experimental import pallas as pl
from jax.experimental.pallas import tpu as pltpu

D_MODEL = 1024
HEAD_DIM = 64
N_HEADS = 16
N_HEADS_A = 8
N_KV_A = 2
WINDOW = 128
N_HEADS_B = 4
MOBA_BLOCK = 256
MOBA_TOPK = 3
N_HEADS_C = 4
C_KV_LATENT = 128
N_IDX_HEADS = 4
IDX_DIM = 64
DSA_TOPK = 256
D_FF = 2816
N_BUCKETS = 32
MAX_DISTANCE = 128
RMS_EPS = 1e-6

A_Q = N_HEADS_A * HEAD_DIM
A_KV = N_KV_A * HEAD_DIM
B_QKV = N_HEADS_B * HEAD_DIM
C_Q = N_HEADS_C * HEAD_DIM
C_QIDX = N_IDX_HEADS * IDX_DIM

TQ = 256
GB = 4
GK = GB * TQ
SUBLANES = 8
LANES = 128
FFN_TM = 1024
FFN_TF = 256
PROJ_TM = GK
VMEM_LIMIT = 56 * 1024 * 1024

F32 = jnp.float32
BF16 = jnp.bfloat16
NEG_INF = float("-inf")
LOG2E = math.log2(math.e)
QK_SCALE = HEAD_DIM ** -0.5 * LOG2E

BIAS_ROWS = 4 * TQ

_NT = (((1,), (1,)), ((), ()))


def _cparams(n_axes):
    return pltpu.CompilerParams(dimension_semantics=("arbitrary",) * n_axes,
                                vmem_limit_bytes=VMEM_LIMIT)


def _dot(a, b):
    return jnp.dot(a, b, preferred_element_type=F32)


def _dot_nt(a, b):
    return lax.dot_general(a, b, _NT, preferred_element_type=F32)


def _rms(x, g):
    return x * lax.rsqrt(jnp.mean(x * x, axis=-1, keepdims=True) + RMS_EPS) * g


def _t5_bucket_np(dist):
    n = np.maximum(dist, 0)
    max_exact = N_BUCKETS // 2
    nf = np.maximum(n, 1).astype(np.float32)
    large = max_exact + (np.log(nf / np.float32(max_exact)) / np.float32(math.log(MAX_DISTANCE / max_exact))
                         * np.float32(N_BUCKETS - max_exact)).astype(np.int32)
    large = np.minimum(large, N_BUCKETS - 1)
    return np.where(n < max_exact, n, large).astype(np.int32)


_IDX_MASKED = -1
_IDX_ZERO = -2


def _bias_index_tiles():
    col = np.arange(TQ)[None, :]
    row = np.arange(TQ)[:, None]
    d_own = col - row
    d_prev = TQ + col - row
    d_prev_w = WINDOW + col - np.arange(WINDOW)[:, None]
    a = np.full((BIAS_ROWS, TQ), _IDX_MASKED, np.int32)
    a[:TQ] = np.where((d_own >= 0) & (d_own < WINDOW), _t5_bucket_np(d_own), _IDX_MASKED)
    a[TQ:TQ + WINDOW] = np.where((d_prev_w >= 0) & (d_prev_w < WINDOW), _t5_bucket_np(d_prev_w), _IDX_MASKED)
    b = np.full((BIAS_ROWS, TQ), _IDX_MASKED, np.int32)
    b[:TQ] = _IDX_ZERO
    b[TQ:2 * TQ] = _t5_bucket_np(d_prev)
    b[2 * TQ:3 * TQ] = np.where(d_own >= 0, _t5_bucket_np(d_own), _IDX_MASKED)
    assert (_t5_bucket_np(np.arange(2 * TQ, 64 * TQ)) == N_BUCKETS - 1).all()
    return np.stack([a, b])


def _bias_kernel(tab_ref, idx_ref, o_ref):
    h = pl.program_id(0)
    idx = idx_ref[0]
    shift = jnp.where(h >= N_HEADS_A, tab_ref[N_BUCKETS - 1, h], 0.0)
    out = jnp.where(idx == _IDX_ZERO, 0.0, NEG_INF)
    for b in range(N_BUCKETS):
        out = jnp.where(idx == b, (tab_ref[b, h] - shift) * LOG2E, out)
    o_ref[0] = out


def _bias_tiles(rel_bias_table):
    idx = jnp.asarray(_bias_index_tiles())
    return pl.pallas_call(
        _bias_kernel,
        grid=(N_HEADS,),
        in_specs=[pl.BlockSpec(memory_space=pltpu.SMEM),
                  pl.BlockSpec((1, BIAS_ROWS, TQ), lambda h: (h // N_HEADS_A, 0, 0))],
        out_specs=pl.BlockSpec((1, BIAS_ROWS, TQ), lambda h: (h, 0, 0)),
        out_shape=jax.ShapeDtypeStruct((N_HEADS, BIAS_ROWS, TQ), F32),
        compiler_params=_cparams(1),
        name="t5_bias_tiles",
    )(rel_bias_table, idx)


def _ffn_kernel(x_ref, g_ref, wg_ref, wu_ref, wd_ref, gf_ref, o_ref, *, final_norm):
    x = x_ref[...]
    h = _rms(x, g_ref[...]).astype(BF16)
    down = None
    for c in range(D_FF // FFN_TF):
        cols = slice(c * FFN_TF, (c + 1) * FFN_TF)
        gate = _dot(h, wg_ref[:, cols])
        up = _dot(h, wu_ref[:, cols])
        act = (gate * jax.nn.sigmoid(gate) * up).astype(BF16)
        d = _dot(act, wd_ref[cols, :])
        down = d if down is None else down + d
    y = x + 0.5 * down
    if final_norm:
        y = _rms(y, gf_ref[...])
    o_ref[...] = y


def _ffn(x, g, wg, wu, wd, gf, final_norm):
    n = x.shape[0]
    tm = min(FFN_TM, n)
    assert n % tm == 0 and D_FF % FFN_TF == 0

    def resident(a):
        return pl.BlockSpec(a.shape, lambda i: (0, 0), pipeline_mode=pl.Buffered(1))

    return pl.pallas_call(
        functools.partial(_ffn_kernel, final_norm=final_norm),
        grid=(n // tm,),
        in_specs=[pl.BlockSpec((tm, D_MODEL), lambda i: (i, 0)),
                  resident(g), resident(wg), resident(wu), resident(wd), resident(gf)],
        out_specs=pl.BlockSpec((tm, D_MODEL), lambda i: (i, 0)),
        out_shape=jax.ShapeDtypeStruct((n, D_MODEL), F32),
        compiler_params=_cparams(1),
        name="swiglu_ffn",
    )(x, g, wg, wu, wd, gf)


_FM_QA, _FM_VA, _FM_QB, _FM_VB, _FM_QC, _FM_QI, _FM_W = 0, 512, 640, 896, 1152, 1408, 1664
_FM_ROWS = 1680
_TM_KA, _TM_KB, _TM_CKV, _TM_KI = 0, 256, 768, 896
_TM_COLS = 1024


def _proj_kernel(x_ref, g_ref, wfm_ref, wtm_ref, gc_ref, wkv_ref, wvT_ref,
                 qa_ref, va_ref, qb_ref, vb_ref, qc_ref, qi_ref, w_ref,
                 ka_ref, kb_ref, kmean_ref, kc_ref, vc_ref, ki_ref):
    tm = x_ref.shape[0]
    nsub = tm // TQ
    h = _rms(x_ref[...], g_ref[...]).astype(BF16)

    def fm(lo, hi, scale=None):
        y = _dot_nt(wfm_ref[lo:hi, :], h)
        return y if scale is None else y * scale

    def put_fm(ref, y):
        for r in range(nsub):
            ref[0, r] = y[:, r * TQ:(r + 1) * TQ].astype(ref.dtype)

    put_fm(qa_ref, fm(_FM_QA, _FM_VA, QK_SCALE))
    put_fm(va_ref, fm(_FM_VA, _FM_QB))
    put_fm(qb_ref, fm(_FM_QB, _FM_VB, QK_SCALE))
    vb_ref[0, 0] = fm(_FM_VB, _FM_QC).astype(BF16)
    put_fm(qc_ref, fm(_FM_QC, _FM_QI, QK_SCALE))
    put_fm(qi_ref, fm(_FM_QI, _FM_W))
    put_fm(w_ref, fm(_FM_W, _FM_ROWS, (N_IDX_HEADS ** -0.5) * (IDX_DIM ** -0.5)))

    ptm = _dot(h, wtm_ref[...])
    for kv in range(N_KV_A):
        lo = _TM_KA + kv * LANES
        ka_ref[0, kv] = ptm[:, lo:lo + HEAD_DIM].astype(BF16)
    for hb in range(N_HEADS_B):
        lo = _TM_KB + hb * LANES
        kb_ref[0, hb] = ptm[:, lo:lo + HEAD_DIM].astype(BF16)
    for r in range(nsub):
        kmean_ref[0, r] = jnp.mean(ptm[r * TQ:(r + 1) * TQ, _TM_KB:_TM_CKV], axis=0, keepdims=True)
    ki_ref[0] = ptm[:, _TM_KI:_TM_KI + IDX_DIM].astype(BF16)

    ckv = _rms(ptm[:, _TM_CKV:_TM_KI], gc_ref[...]).astype(BF16)
    kc_ref[0] = _dot(ckv, wkv_ref[...])[:, :HEAD_DIM].astype(BF16)
    vc_ref[0, 0] = _dot_nt(wvT_ref[...], ckv).astype(BF16)


def _proj(x, g, wfm, wtm, gc, wkv, wvT, batch, seq):
    tm = PROJ_TM
    assert seq % tm == 0 and tm == GK
    tpb = seq // tm
    nsub = tm // TQ
    nb = seq // TQ

    def fm_spec(rows):
        return pl.BlockSpec((1, nsub, rows, TQ), lambda i: (i // tpb, i % tpb, 0, 0))

    def fm_shape(rows, dtype=BF16):
        return jax.ShapeDtypeStruct((batch, nb, rows, TQ), dtype)

    def grp_spec(rows):
        return pl.BlockSpec((1, 1, rows, GK), lambda i: (i // tpb, i % tpb, 0, 0))

    def grp_shape(rows):
        return jax.ShapeDtypeStruct((batch, tpb, rows, GK), BF16)

    def full(a):
        return pl.BlockSpec(a.shape, lambda i: (0,) * a.ndim)

    out_specs = [fm_spec(A_Q), fm_spec(A_KV), fm_spec(B_QKV), grp_spec(B_QKV), fm_spec(C_Q), fm_spec(C_QIDX),
                 fm_spec(16),
                 pl.BlockSpec((1, N_KV_A, tm, HEAD_DIM), lambda i: (i // tpb, 0, i % tpb, 0)),
                 pl.BlockSpec((1, N_HEADS_B, tm, HEAD_DIM), lambda i: (i // tpb, 0, i % tpb, 0)),
                 pl.BlockSpec((1, nsub, 1, N_HEADS_B * LANES), lambda i: (i // tpb, i % tpb, 0, 0)),
                 pl.BlockSpec((1, tm, HEAD_DIM), lambda i: (i // tpb, i % tpb, 0)),
                 grp_spec(HEAD_DIM),
                 pl.BlockSpec((1, tm, IDX_DIM), lambda i: (i // tpb, i % tpb, 0))]
    out_shape = [fm_shape(A_Q), fm_shape(A_KV), fm_shape(B_QKV), grp_shape(B_QKV), fm_shape(C_Q), fm_shape(C_QIDX),
                 fm_shape(16, F32),
                 jax.ShapeDtypeStruct((batch, N_KV_A, seq, HEAD_DIM), BF16),
                 jax.ShapeDtypeStruct((batch, N_HEADS_B, seq, HEAD_DIM), BF16),
                 jax.ShapeDtypeStruct((batch, nb, 1, N_HEADS_B * LANES), F32),
                 jax.ShapeDtypeStruct((batch, seq, HEAD_DIM), BF16),
                 grp_shape(HEAD_DIM),
                 jax.ShapeDtypeStruct((batch, seq, IDX_DIM), BF16)]
    return pl.pallas_call(
        _proj_kernel,
        grid=(batch * tpb,),
        in_specs=[pl.BlockSpec((tm, D_MODEL), lambda i: (i, 0)),
                  full(g), full(wfm), full(wtm), full(gc), full(wkv), full(wvT)],
        out_specs=out_specs,
        out_shape=out_shape,
        compiler_params=_cparams(1),
        name="mix_in_proj",
    )(x, g, wfm, wtm, gc, wkv, wvT)


def _online_update(parts, v_parts, m_ref, l_ref, acc_ref, hh):
    m_prev = m_ref[hh]
    m_new = jnp.maximum(m_prev, jnp.max(functools.reduce(jnp.maximum, parts), axis=0, keepdims=True))
    m_safe = jnp.where(m_new == NEG_INF, 0.0, m_new)
    alpha = jnp.exp2(m_prev - m_safe)
    ps = [jnp.exp2(s - m_safe) for s in parts]
    pv = functools.reduce(lambda a, b: a + b, [_dot(v, p.astype(BF16)) for v, p in zip(v_parts, ps)])
    l_ref[hh] = alpha * l_ref[hh] + jnp.sum(functools.reduce(lambda a, b: a + b, ps), axis=0, keepdims=True)
    acc_ref[hh] = alpha * acc_ref[hh] + pv
    m_ref[hh] = m_new


def _init_stats(m_ref, l_ref, acc_ref, m_init=NEG_INF):
    m_ref[...] = jnp.full(m_ref.shape, m_init, F32)
    l_ref[...] = jnp.zeros(l_ref.shape, F32)
    acc_ref[...] = jnp.zeros(acc_ref.shape, F32)


LAZY_GUARD = 64.0


def _lazy_update(parts, v_parts, m_ref, l_ref, acc_ref, top_ref, bad_ref, hh):
    m_stab = m_ref[hh]
    ps = [jnp.exp2(s - m_stab) for s in parts]
    top = jnp.max(functools.reduce(jnp.maximum, parts), axis=0, keepdims=True)
    pv = functools.reduce(lambda a, b: a + b, [_dot(v, p.astype(BF16)) for v, p in zip(v_parts, ps)])
    m_new = jnp.maximum(m_stab, top)
    beta = jnp.exp2(m_stab - m_new)
    l_ref[hh] = (l_ref[hh] + jnp.sum(functools.reduce(lambda a, b: a + b, ps), axis=0, keepdims=True)) * beta
    acc_ref[hh] = (acc_ref[hh] + pv) * beta
    m_ref[hh] = m_new
    top_ref[hh] = jnp.maximum(top_ref[hh], top)
    bad_ref[...] = jnp.maximum(bad_ref[...], jnp.where(top - m_stab > LAZY_GUARD, 1.0, 0.0))


def _init_lazy(m_ref, l_ref, acc_ref, top_ref, bad_ref):
    _init_stats(m_ref, l_ref, acc_ref, 0.0)
    top_ref[...] = jnp.full(top_ref.shape, NEG_INF, F32)
    bad_ref[...] = jnp.zeros(bad_ref.shape, F32)


def _lazy_failed(top_ref, bad_ref, n_heads):
    low = functools.reduce(jnp.maximum, [jnp.where(top_ref[hh] < -LAZY_GUARD, 1.0, 0.0) for hh in range(n_heads)])
    return jnp.max(jnp.maximum(bad_ref[...], low)) > 0.5


def _finish(l_ref, acc_ref, n_heads):
    oT = jnp.concatenate([acc_ref[hh] * (1.0 / l_ref[hh]) for hh in range(n_heads)], axis=0)
    return oT.T


def _bias_tile(bias_ref, hh, j, i):
    kind = jnp.clip(j - i + 2, 0, 3)
    return bias_ref[hh, pl.ds(pl.multiple_of(kind * TQ, TQ), TQ), :]


def _group_bounds(i):
    n_groups = i // GB + 1
    n_far = jnp.maximum((i - 1) // GB, 0)
    return n_far, n_groups


_SWA_BIAS_ROWS = TQ + WINDOW


def _swa_kernel(sink_ref, q_ref, k_ref, v_ref, bias_ref, o_ref):
    i = pl.program_id(1)
    g_heads = N_HEADS_A // N_KV_A
    start = pl.multiple_of(i * TQ, TQ)
    prev_start = pl.multiple_of(jnp.maximum(i * TQ - WINDOW, 0), WINDOW)
    prev_off = jnp.where(i > 0, 0.0, NEG_INF)
    k_main = [k_ref[0, kv, pl.ds(start, TQ), :] for kv in range(N_KV_A)]
    k_prev = [k_ref[0, kv, pl.ds(prev_start, WINDOW), :] for kv in range(N_KV_A)]
    v_main = [v_ref[0, i, kv * HEAD_DIM:(kv + 1) * HEAD_DIM, :] for kv in range(N_KV_A)]
    v_prev = [v_ref[0, jnp.maximum(i - 1, 0), kv * HEAD_DIM:(kv + 1) * HEAD_DIM, TQ - WINDOW:] for kv in range(N_KV_A)]

    def scores(h):
        kv = h // g_heads
        q = q_ref[0, 0, h * HEAD_DIM:(h + 1) * HEAD_DIM, :]
        s_main = _dot(k_main[kv], q) + bias_ref[h, 0:TQ, :]
        s_prev = _dot(k_prev[kv], q) + bias_ref[h, TQ:TQ + WINDOW, :] + prev_off
        return s_main, s_prev, sink_ref[h] * LOG2E

    def head_out(h, s_main, s_prev, sink, m):
        kv = h // g_heads
        p_main = jnp.exp2(s_main - m)
        p_prev = jnp.exp2(s_prev - m)
        denom = (jnp.sum(p_main, axis=0, keepdims=True) + jnp.sum(p_prev, axis=0, keepdims=True)
                 + jnp.exp2(sink - m))
        oT = _dot(v_main[kv], p_main.astype(BF16)) + _dot(v_prev[kv], p_prev.astype(BF16))
        return oT * (1.0 / denom)

    def tile_max(s_main, s_prev):
        return jnp.maximum(jnp.max(s_main, axis=0, keepdims=True), jnp.max(s_prev, axis=0, keepdims=True))

    outs, over = [], None
    for h in range(N_HEADS_A):
        s_main, s_prev, sink = scores(h)
        outs.append(head_out(h, s_main, s_prev, sink, sink))
        gap = tile_max(s_main, s_prev) - sink
        over = gap if over is None else jnp.maximum(over, gap)
    o_ref[0] = jnp.concatenate(outs, axis=0).T.astype(o_ref.dtype)

    @pl.when(jnp.max(over) > LAZY_GUARD)
    def _():
        outs = []
        for h in range(N_HEADS_A):
            s_main, s_prev, sink = scores(h)
            outs.append(head_out(h, s_main, s_prev, sink, jnp.maximum(tile_max(s_main, s_prev), sink)))
        o_ref[0] = jnp.concatenate(outs, axis=0).T.astype(o_ref.dtype)


def _swa(sinks, qaT, ka, vaT, bias, batch, seq):
    nb = seq // TQ
    return pl.pallas_call(
        _swa_kernel,
        grid=(batch, nb),
        in_specs=[pl.BlockSpec(memory_space=pltpu.SMEM),
                  pl.BlockSpec((1, 1, A_Q, TQ), lambda b, i: (b, i, 0, 0)),
                  pl.BlockSpec((1, N_KV_A, seq, HEAD_DIM), lambda b, i: (b, 0, 0, 0)),
                  pl.BlockSpec((1, nb, A_KV, TQ), lambda b, i: (b, 0, 0, 0)),
                  pl.BlockSpec((N_HEADS_A, _SWA_BIAS_ROWS, TQ), lambda b, i: (0, 0, 0))],
        out_specs=pl.BlockSpec((1, TQ, A_Q), lambda b, i: (b, i, 0)),
        out_shape=jax.ShapeDtypeStruct((batch, seq, A_Q), BF16),
        compiler_params=_cparams(2),
        name="swa_attn",
    )(sinks, qaT, ka, vaT, bias)


_MOBA_HPS = 4


def _moba_kernel(q_ref, k_ref, v_ref, kmean_ref, bias_ref, o_ref,
                 sel_ref, m_ref, l_ref, acc_ref, top_ref, bad_ref, *, n_sel):
    i = pl.program_id(2)
    nb = kmean_ref.shape[2]
    row = lax.broadcasted_iota(jnp.int32, (nb, TQ), 0)
    qs = [q_ref[0, 0, hh * HEAD_DIM:(hh + 1) * HEAD_DIM, :] for hh in range(_MOBA_HPS)]

    for hh in range(_MOBA_HPS):
        gate = _dot(kmean_ref[0, hh][:, :HEAD_DIM], qs[hh].astype(F32))
        gate = jnp.where(row < i, gate, NEG_INF)
        sel = row == i
        for _ in range(n_sel):
            best = jnp.max(gate, axis=0, keepdims=True)
            first = jnp.min(jnp.where(gate == best, row, nb), axis=0, keepdims=True)
            pick = (row == first) & (best > NEG_INF)
            sel = sel | pick
            gate = jnp.where(pick, NEG_INF, gate)
        sel_ref[hh] = jnp.where(sel, 0.0, NEG_INF)

    def group(g, mixed, lazy):
        kbase = pl.multiple_of(g * GK, GK)
        s_all = [_dot(k_ref[0, hh, pl.ds(kbase, GK), :], qs[hh]) for hh in range(_MOBA_HPS)]
        for hh in range(_MOBA_HPS):
            parts = []
            for b in range(GB):
                j = g * GB + b
                sb = s_all[hh][b * TQ:(b + 1) * TQ] + sel_ref[hh, pl.ds(j, 1), :]
                if mixed:
                    sb = sb + _bias_tile(bias_ref, hh, j, i)
                parts.append(sb)
            v_parts = [v_ref[0, g, hh * HEAD_DIM:(hh + 1) * HEAD_DIM, b * TQ:(b + 1) * TQ] for b in range(GB)]
            if lazy:
                _lazy_update(parts, v_parts, m_ref, l_ref, acc_ref, top_ref, bad_ref, hh)
            else:
                _online_update(parts, v_parts, m_ref, l_ref, acc_ref, hh)

    n_far, n_groups = _group_bounds(i)

    def all_groups(lazy):
        def far_body(g, carry):
            group(g, False, lazy)
            return carry

        def mixed_body(g, carry):
            group(g, True, lazy)
            return carry

        lax.fori_loop(0, n_far, far_body, 0)
        lax.fori_loop(n_far, n_groups, mixed_body, 0)

    _init_lazy(m_ref, l_ref, acc_ref, top_ref, bad_ref)
    all_groups(True)

    @pl.when(_lazy_failed(top_ref, bad_ref, _MOBA_HPS))
    def _():
        _init_stats(m_ref, l_ref, acc_ref)
        all_groups(False)

    o_ref[0] = _finish(l_ref, acc_ref, _MOBA_HPS).astype(o_ref.dtype)


def _moba(qbT, kb, vbT, kmean, bias, batch, seq):
    nb = seq // TQ
    ng = seq // GK
    hps = _MOBA_HPS
    n_sel = min(MOBA_TOPK, nb - 1)
    bias_blk0 = N_HEADS_A // hps
    return pl.pallas_call(
        functools.partial(_moba_kernel, n_sel=n_sel),
        grid=(batch, N_HEADS_B // hps, nb),
        in_specs=[pl.BlockSpec((1, 1, hps * HEAD_DIM, TQ), lambda b, hp, i: (b, i, hp, 0)),
                  pl.BlockSpec((1, hps, seq, HEAD_DIM), lambda b, hp, i: (b, hp, 0, 0)),
                  pl.BlockSpec((1, ng, hps * HEAD_DIM, GK), lambda b, hp, i: (b, 0, hp, 0)),
                  pl.BlockSpec((1, hps, nb, LANES), lambda b, hp, i: (b, hp, 0, 0)),
                  pl.BlockSpec((hps, BIAS_ROWS, TQ), lambda b, hp, i: (bias_blk0 + hp, 0, 0))],
        out_specs=pl.BlockSpec((1, TQ, hps * HEAD_DIM), lambda b, hp, i: (b, i, hp)),
        out_shape=jax.ShapeDtypeStruct((batch, seq, B_QKV), BF16),
        scratch_shapes=[pltpu.VMEM((hps, nb, TQ), F32),
                        pltpu.VMEM((hps, 1, TQ), F32),
                        pltpu.VMEM((hps, 1, TQ), F32),
                        pltpu.VMEM((hps, HEAD_DIM, TQ), F32),
                        pltpu.VMEM((hps, 1, TQ), F32),
                        pltpu.VMEM((1, TQ), F32)],
        compiler_params=_cparams(3),
        name="moba_attn",
    )(qbT, kb, vbT, kmean, bias)


_PACK16 = 16
_I16_MIN, _I16_MAX = -32768, 32767
_N_COUNT_ACC = 4


def _f32_to_key(x):
    bits = lax.bitcast_convert_type(x, jnp.int32)
    key = bits ^ ((bits >> 31) & jnp.int32(0x7FFFFFFF))
    return jnp.where(key == -1, 0, key)


def _key_to_f32(key):
    bits = key ^ ((key >> 31) & jnp.int32(0x7FFFFFFF))
    return lax.bitcast_convert_type(bits, F32)


_MIN_NORMAL_KEY = 0x00800000


def _next_key(key):
    nxt = key + 1
    nxt = jnp.where((nxt >= -_MIN_NORMAL_KEY) & (nxt < 0), 0, nxt)
    return jnp.where((nxt > 0) & (nxt < _MIN_NORMAL_KEY), _MIN_NORMAL_KEY, nxt)


def _dsa_kernel(qi_ref, w_ref, ki_ref, q_ref, k_ref, v_ref, bias_ref, tri_ref, o_ref,
                score_ref, hi_ref, lo_ref, m_ref, l_ref, acc_ref, ties_ref, top_ref, bad_ref, *, n_keep):
    i = pl.program_id(1)
    n_far, n_groups = _group_bounds(i)

    def rows(g, b=0):
        return pl.ds(pl.multiple_of(g * GK + b * TQ, TQ), TQ)

    w = w_ref[0, 0]
    qis = [qi_ref[0, 0, hi * IDX_DIM:(hi + 1) * IDX_DIM, :] for hi in range(N_IDX_HEADS)]
    krow = lax.broadcasted_iota(jnp.int32, (TQ, TQ), 0)
    qcol = lax.broadcasted_iota(jnp.int32, (TQ, TQ), 1)

    def score_group(g, mixed):
        for b in range(GB):
            ki = ki_ref[0, rows(g, b), :]
            ds = [_dot(ki, qi) for qi in qis]
            sc = functools.reduce(lambda x, y: x + y,
                                  [jnp.maximum(d, 0.0) * w[hi:hi + 1, :] for hi, d in enumerate(ds)])
            if mixed:
                sc = jnp.where(krow + (g * GB + b) * TQ <= qcol + i * TQ, sc, NEG_INF)
            score_ref[rows(g, b), :] = sc
            key = _f32_to_key(sc)
            hi_ref[rows(g, b), :] = (key >> 16).astype(jnp.int16)
            lo_ref[rows(g, b), :] = ((key & 0xFFFF) + _I16_MIN).astype(jnp.int16)

    def score_far(g, carry):
        score_group(g, False)
        return carry

    def score_mixed(g, carry):
        score_group(g, True)
        return carry

    lax.fori_loop(0, n_far, score_far, 0)
    lax.fori_loop(n_far, n_groups, score_mixed, 0)

    def group16(ref, g):
        return ref[pl.ds(pl.multiple_of(g * GK, GK), GK), :]

    def count_ge16(ref, cand):
        cand16 = jnp.broadcast_to(cand.astype(jnp.int16), (_PACK16, TQ))

        def body(g, accs):
            accs = list(accs)
            grp = group16(ref, g)
            for r in range(GK // _PACK16):
                a = r % _N_COUNT_ACC
                hit = grp[r * _PACK16:(r + 1) * _PACK16] >= cand16
                accs[a] = accs[a] + jnp.where(hit, jnp.int16(1), jnp.int16(0))
            return tuple(accs)

        zero = jnp.zeros((_PACK16, TQ), jnp.int16)
        accs = lax.fori_loop(0, n_groups, body, (zero,) * _N_COUNT_ACC)
        total = functools.reduce(lambda x, y: x + y, accs).astype(jnp.int32)
        return jnp.sum(total, axis=0, keepdims=True)

    def kth_largest16(ref, k):
        def bit_body(b, carry):
            c, n_gt = carry
            c_try = c + lax.shift_left(jnp.int32(1), 15 - b)
            n = count_ge16(ref, c_try)
            ok = n >= k
            return jnp.where(ok, c_try, c), jnp.where(ok, n_gt, n)
        init = (jnp.full((1, TQ), _I16_MIN, jnp.int32), jnp.zeros((1, TQ), jnp.int32))
        return lax.fori_loop(0, 16, bit_body, init)

    c_hi, n_gt_hi = kth_largest16(hi_ref, n_keep)
    c_hi16 = c_hi.astype(jnp.int16)

    def mask_lo(g, carry):
        sl = pl.ds(pl.multiple_of(g * GK, GK), GK)
        lo_ref[sl, :] = jnp.where(hi_ref[sl, :] == c_hi16, lo_ref[sl, :], jnp.int16(_I16_MIN))
        return carry

    lax.fori_loop(0, n_groups, mask_lo, 0)
    c_lo, n_gt_lo = kth_largest16(lo_ref, n_keep - n_gt_hi)
    key_thr = lax.shift_left(c_hi, 16) | (c_lo - _I16_MIN)
    thr = _key_to_f32(key_thr)
    thr_next = _key_to_f32(_next_key(key_thr))
    tie_budget = (n_keep - (n_gt_hi + n_gt_lo)).astype(F32)

    qs = [q_ref[0, 0, hh * HEAD_DIM:(hh + 1) * HEAD_DIM, :] for hh in range(N_HEADS_C)]

    def selection_masks(g):
        blks = [score_ref[rows(g, b), :] for b in range(GB)]
        ties = [jnp.where(blk == thr, 1.0, 0.0) for blk in blks]
        prefix = [_dot(tri_ref[...], tie.astype(BF16)) for tie in ties]
        masks = []
        seen = ties_ref[...]
        for blk, tie, pre in zip(blks, ties, prefix):
            bar = jnp.where(pre < tie_budget - seen, thr, thr_next)
            masks.append(jnp.where(blk >= bar, 0.0, NEG_INF))
            seen = seen + pre[TQ - 1:TQ, :] + tie[TQ - 1:TQ, :]
        ties_ref[...] = seen
        return masks

    def attend_group(g, mixed, lazy):
        masks = selection_masks(g)
        kc = k_ref[0, pl.ds(pl.multiple_of(g * GK, GK), GK), :]
        s_all = [_dot(kc, q) for q in qs]
        v_parts = [v_ref[0, g, :, b * TQ:(b + 1) * TQ] for b in range(GB)]
        for hh in range(N_HEADS_C):
            parts = []
            for b in range(GB):
                sb = s_all[hh][b * TQ:(b + 1) * TQ] + masks[b]
                if mixed:
                    sb = sb + _bias_tile(bias_ref, hh, g * GB + b, i)
                parts.append(sb)
            if lazy:
                _lazy_update(parts, v_parts, m_ref, l_ref, acc_ref, top_ref, bad_ref, hh)
            else:
                _online_update(parts, v_parts, m_ref, l_ref, acc_ref, hh)

    def attend_all_groups(lazy):
        def far_body(g, carry):
            attend_group(g, False, lazy)
            return carry

        def mixed_body(g, carry):
            attend_group(g, True, lazy)
            return carry

        ties_ref[...] = jnp.zeros(ties_ref.shape, F32)
        lax.fori_loop(0, n_far, far_body, 0)
        lax.fori_loop(n_far, n_groups, mixed_body, 0)

    _init_lazy(m_ref, l_ref, acc_ref, top_ref, bad_ref)
    attend_all_groups(True)

    @pl.when(_lazy_failed(top_ref, bad_ref, N_HEADS_C))
    def _():
        _init_stats(m_ref, l_ref, acc_ref)
        attend_all_groups(False)

    o_ref[0] = _finish(l_ref, acc_ref, N_HEADS_C).astype(o_ref.dtype)


def _dsa(qiT, wT, ki, qcT, kc, vcT, bias, batch, seq):
    nb = seq // TQ
    ng = seq // GK
    n_keep = min(DSA_TOPK, seq // 4)
    tri = jnp.asarray(np.tril(np.ones((TQ, TQ), np.float32), -1), BF16)
    bias_blk = (N_HEADS_A + N_HEADS_B) // N_HEADS_C
    return pl.pallas_call(
        functools.partial(_dsa_kernel, n_keep=n_keep),
        grid=(batch, nb),
        in_specs=[pl.BlockSpec((1, 1, C_QIDX, TQ), lambda b, i: (b, i, 0, 0)),
                  pl.BlockSpec((1, 1, 16, TQ), lambda b, i: (b, i, 0, 0)),
                  pl.BlockSpec((1, seq, IDX_DIM), lambda b, i: (b, 0, 0)),
                  pl.BlockSpec((1, 1, C_Q, TQ), lambda b, i: (b, i, 0, 0)),
                  pl.BlockSpec((1, seq, HEAD_DIM), lambda b, i: (b, 0, 0)),
                  pl.BlockSpec((1, ng, HEAD_DIM, GK), lambda b, i: (b, 0, 0, 0)),
                  pl.BlockSpec((N_HEADS_C, BIAS_ROWS, TQ), lambda b, i: (bias_blk, 0, 0)),
                  pl.BlockSpec((TQ, TQ), lambda b, i: (0, 0))],
        out_specs=pl.BlockSpec((1, TQ, C_Q), lambda b, i: (b, i, 0)),
        out_shape=jax.ShapeDtypeStruct((batch, seq, C_Q), BF16),
        scratch_shapes=[pltpu.VMEM((seq, TQ), F32),
                        pltpu.VMEM((seq, TQ), jnp.int16),
                        pltpu.VMEM((seq, TQ), jnp.int16),
                        pltpu.VMEM((N_HEADS_C, 1, TQ), F32),
                        pltpu.VMEM((N_HEADS_C, 1, TQ), F32),
                        pltpu.VMEM((N_HEADS_C, HEAD_DIM, TQ), F32),
                        pltpu.VMEM((1, TQ), F32),
                        pltpu.VMEM((N_HEADS_C, 1, TQ), F32),
                        pltpu.VMEM((1, TQ), F32)],
        compiler_params=_cparams(2),
        name="dsa_attn",
    )(qiT, wT, ki, qcT, kc, vcT, bias, tri)


def _oproj_kernel(x_ref, a_ref, b_ref, c_ref, wa_ref, wb_ref, wc_ref, o_ref):
    o_ref[...] = (x_ref[...] + _dot(a_ref[...], wa_ref[...]) + _dot(b_ref[...], wb_ref[...])
                  + _dot(c_ref[...], wc_ref[...]))


def _oproj(x, a, b, c, wa, wb, wc):
    n = x.shape[0]
    tm = min(512, n)
    assert n % tm == 0

    def rows(cols):
        return pl.BlockSpec((tm, cols), lambda i: (i, 0))

    def full(w):
        return pl.BlockSpec(w.shape, lambda i: (0, 0))

    return pl.pallas_call(
        _oproj_kernel,
        grid=(n // tm,),
        in_specs=[rows(D_MODEL), rows(A_Q), rows(B_QKV), rows(C_Q), full(wa), full(wb), full(wc)],
        out_specs=rows(D_MODEL),
        out_shape=jax.ShapeDtypeStruct((n, D_MODEL), F32),
        compiler_params=_cparams(1),
        name="mix_out_proj",
    )(x, a, b, c, wa, wb, wc)


def _pad_heads(w, n_heads):
    w = w.reshape(D_MODEL, n_heads, HEAD_DIM)
    return jnp.pad(w, ((0, 0), (0, 0), (0, LANES - HEAD_DIM))).reshape(D_MODEL, n_heads * LANES)


def _split_w_in(w_in):
    cuts = np.cumsum([A_Q, A_KV, A_KV, B_QKV, B_QKV, B_QKV, C_Q, C_KV_LATENT, C_QIDX, IDX_DIM])
    qa, ka, va, qb, kb, vb, qc, ckv, qidx, kidx, widx = jnp.split(w_in, cuts, axis=1)
    wfm = jnp.concatenate([qa, va, qb, vb, qc, qidx, jnp.pad(widx, ((0, 0), (0, 16 - N_IDX_HEADS)))], axis=1)
    wtm = jnp.concatenate([_pad_heads(ka, N_KV_A), _pad_heads(kb, N_HEADS_B), ckv, _pad_heads(kidx, 1)], axis=1)
    assert wfm.shape[1] == _FM_ROWS and wtm.shape[1] == _TM_COLS
    return wfm.T.astype(BF16), wtm.astype(BF16)


def kernel(x, rel_bias_table, ffn1_norm, ffn1_w_gate, ffn1_w_up, ffn1_w_down, mix_norm, w_in, attn_sinks, kv_norm_c, w_kv_up_c, w_out, ffn2_norm, ffn2_w_gate, ffn2_w_up, ffn2_w_down, final_norm):
    batch, seq = x.shape[0], x.shape[1]
    depth = w_in.shape[0]
    assert seq % GK == 0 and x.shape[2] == D_MODEL
    nb = seq // TQ
    bias = _bias_tiles(rel_bias_table)
    gf = final_norm.reshape(1, D_MODEL)
    xf = x.reshape(batch * seq, D_MODEL)
    for l in range(depth):
        xf = _ffn(xf, ffn1_norm[l].reshape(1, D_MODEL), ffn1_w_gate[l].astype(BF16), ffn1_w_up[l].astype(BF16),
                  ffn1_w_down[l].astype(BF16), gf, False)
        wfm, wtm = _split_w_in(w_in[l])
        wkv = w_kv_up_c[l].astype(BF16)
        (qaT, vaT, qbT, vbT, qcT, qiT, wT, ka, kb, kmean, kc, vcT, ki) = _proj(
            xf, mix_norm[l].reshape(1, D_MODEL), wfm, wtm, kv_norm_c[l].reshape(1, C_KV_LATENT),
            wkv, wkv[:, HEAD_DIM:].T, batch, seq)
        kmean = kmean.reshape(batch, nb, N_HEADS_B, LANES).transpose(0, 2, 1, 3)
        out_a = _swa(attn_sinks[l], qaT, ka, vaT, bias, batch, seq)
        out_b = _moba(qbT, kb, vbT, kmean, bias, batch, seq)
        out_c = _dsa(qiT, wT, ki, qcT, kc, vcT, bias, batch, seq)
        wo = w_out[l].astype(BF16)
        xf = _oproj(xf, out_a.reshape(batch * seq, A_Q), out_b.reshape(batch * seq, B_QKV),
                    out_c.reshape(batch * seq, C_Q), wo[:A_Q], wo[A_Q:A_Q + B_QKV], wo[A_Q + B_QKV:])
        xf = _ffn(xf, ffn2_norm[l].reshape(1, D_MODEL), ffn2_w_gate[l].astype(BF16), ffn2_w_up[l].astype(BF16),
                  ffn2_w_down[l].astype(BF16), gf, l == depth - 1)
    return xf.reshape(batch, seq, D_MODEL)
```

```python
import functools
import math

import jax
import jax.numpy as jnp
import numpy as np
from jax import lax
from jax.experimental import pallas as pl
from jax.experimental.pallas import tpu as pltpu

D_MODEL = 1024
HEAD_DIM = 64
N_HEADS = 16
N_HEADS_A = 8
N_KV_A = 2
WINDOW = 128
N_HEADS_B = 4
MOBA_BLOCK = 256
MOBA_TOPK = 3
N_HEADS_C = 4
C_KV_LATENT = 128
N_IDX_HEADS = 4
IDX_DIM = 64
DSA_TOPK = 256
D_FF = 2816
N_BUCKETS = 32
MAX_DISTANCE = 128
RMS_EPS = 1e-6

A_Q = N_HEADS_A * HEAD_DIM
A_KV = N_KV_A * HEAD_DIM
B_QKV = N_HEADS_B * HEAD_DIM
C_Q = N_HEADS_C * HEAD_DIM
C_QIDX = N_IDX_HEADS * IDX_DIM

TQ = 256
GB = 4
GK = GB * TQ
SUBLANES = 8
LANES = 128
FFN_TM = 1024
FFN_TF = 256
PROJ_TM = GK
VMEM_LIMIT = 56 * 1024 * 1024

F32 = jnp.float32
BF16 = jnp.bfloat16
NEG_INF = float("-inf")
LOG2E = math.log2(math.e)
QK_SCALE = HEAD_DIM ** -0.5 * LOG2E

BIAS_ROWS = 4 * TQ

_NT = (((1,), (1,)), ((), ()))


def _cparams(n_axes):
    return pltpu.CompilerParams(dimension_semantics=("arbitrary",) * n_axes,
                                vmem_limit_bytes=VMEM_LIMIT)


def _dot(a, b):
    return jnp.dot(a, b, preferred_element_type=F32)


def _dot_nt(a, b):
    return lax.dot_general(a, b, _NT, preferred_element_type=F32)


def _rms(x, g):
    return x * lax.rsqrt(jnp.mean(x * x, axis=-1, keepdims=True) + RMS_EPS) * g


def _t5_bucket_np(dist):
    n = np.maximum(dist, 0)
    max_exact = N_BUCKETS // 2
    nf = np.maximum(n, 1).astype(np.float32)
    large = max_exact + (np.log(nf / np.float32(max_exact)) / np.float32(math.log(MAX_DISTANCE / max_exact))
                         * np.float32(N_BUCKETS - max_exact)).astype(np.int32)
    large = np.minimum(large, N_BUCKETS - 1)
    return np.where(n < max_exact, n, large).astype(np.int32)


_IDX_MASKED = -1
_IDX_ZERO = -2


def _bias_index_tiles():
    col = np.arange(TQ)[None, :]
    row = np.arange(TQ)[:, None]
    d_own = col - row
    d_prev = TQ + col - row
    d_prev_w = WINDOW + col - np.arange(WINDOW)[:, None]
    a = np.full((BIAS_ROWS, TQ), _IDX_MASKED, np.int32)
    a[:TQ] = np.where((d_own >= 0) & (d_own < WINDOW), _t5_bucket_np(d_own), _IDX_MASKED)
    a[TQ:TQ + WINDOW] = np.where((d_prev_w >= 0) & (d_prev_w < WINDOW), _t5_bucket_np(d_prev_w), _IDX_MASKED)
    b = np.full((BIAS_ROWS, TQ), _IDX_MASKED, np.int32)
    b[:TQ] = _IDX_ZERO
    b[TQ:2 * TQ] = _t5_bucket_np(d_prev)
    b[2 * TQ:3 * TQ] = np.where(d_own >= 0, _t5_bucket_np(d_own), _IDX_MASKED)
    assert (_t5_bucket_np(np.arange(2 * TQ, 64 * TQ)) == N_BUCKETS - 1).all()
    return np.stack([a, b])


def _bias_kernel(tab_ref, idx_ref, o_ref):
    h = pl.program_id(0)
    idx = idx_ref[0]
    shift = jnp.where(h >= N_HEADS_A, tab_ref[N_BUCKETS - 1, h], 0.0)
    out = jnp.where(idx == _IDX_ZERO, 0.0, NEG_INF)
    for b in range(N_BUCKETS):
        out = jnp.where(idx == b, (tab_ref[b, h] - shift) * LOG2E, out)
    o_ref[0] = out


def _bias_tiles(rel_bias_table):
    idx = jnp.asarray(_bias_index_tiles())
    return pl.pallas_call(
        _bias_kernel,
        grid=(N_HEADS,),
        in_specs=[pl.BlockSpec(memory_space=pltpu.SMEM),
                  pl.BlockSpec((1, BIAS_ROWS, TQ), lambda h: (h // N_HEADS_A, 0, 0))],
        out_specs=pl.BlockSpec((1, BIAS_ROWS, TQ), lambda h: (h, 0, 0)),
        out_shape=jax.ShapeDtypeStruct((N_HEADS, BIAS_ROWS, TQ), F32),
        compiler_params=_cparams(1),
        name="t5_bias_tiles",
    )(rel_bias_table, idx)


def _swiglu_half_step(x, g_ref, wg_ref, wu_ref, wd_ref, gf_ref, final_norm):
    h = _rms(x, g_ref[...]).astype(BF16)
    down = None
    for c in range(D_FF // FFN_TF):
        cols = slice(c * FFN_TF, (c + 1) * FFN_TF)
        gate = _dot(h, wg_ref[:, cols])
        up = _dot(h, wu_ref[:, cols])
        act = (gate * jax.nn.sigmoid(gate) * up).astype(BF16)
        d = _dot(act, wd_ref[cols, :])
        down = d if down is None else down + d
    y = x + 0.5 * down
    return _rms(y, gf_ref[...]) if final_norm else y


def _ffn_kernel(x_ref, g_ref, wg_ref, wu_ref, wd_ref, gf_ref, o_ref, *, final_norm):
    o_ref[...] = _swiglu_half_step(x_ref[...], g_ref, wg_ref, wu_ref, wd_ref, gf_ref, final_norm)


def _mix_out_ffn_kernel(x_ref, a_ref, b_ref, c_ref, wa_ref, wb_ref, wc_ref,
                        g_ref, wg_ref, wu_ref, wd_ref, gf_ref, o_ref, *, final_norm):
    x = (x_ref[...] + _dot(a_ref[...], wa_ref[...]) + _dot(b_ref[...], wb_ref[...])
         + _dot(c_ref[...], wc_ref[...]))
    o_ref[...] = _swiglu_half_step(x, g_ref, wg_ref, wu_ref, wd_ref, gf_ref, final_norm)


def _resident(a):
    return pl.BlockSpec(a.shape, lambda i: (0, 0), pipeline_mode=pl.Buffered(1))


def _ffn(x, g, wg, wu, wd, gf, final_norm, mix=None):
    n = x.shape[0]
    tm = min(FFN_TM, n)
    assert n % tm == 0 and D_FF % FFN_TF == 0

    def rows(cols):
        return pl.BlockSpec((tm, cols), lambda i: (i, 0))

    weights = (g, wg, wu, wd, gf)
    if mix is None:
        body, operands = _ffn_kernel, (x,) + weights
        in_specs = [rows(D_MODEL)] + [_resident(w) for w in weights]
    else:
        body, operands = _mix_out_ffn_kernel, (x,) + tuple(mix) + weights
        in_specs = ([rows(D_MODEL)] + [rows(t.shape[1]) for t in mix[:3]] + [_resident(w) for w in mix[3:]]
                    + [_resident(w) for w in weights])
    return pl.pallas_call(
        functools.partial(body, final_norm=final_norm),
        grid=(n // tm,),
        in_specs=in_specs,
        out_specs=rows(D_MODEL),
        out_shape=jax.ShapeDtypeStruct((n, D_MODEL), F32),
        compiler_params=_cparams(1),
        name="swiglu_ffn" if mix is None else "mix_out_ffn",
    )(*operands)


_FM_QA, _FM_VA, _FM_QB, _FM_VB, _FM_QC, _FM_QI, _FM_W = 0, 512, 640, 896, 1152, 1408, 1664
_FM_ROWS = 1680
_TM_KA, _TM_KB, _TM_CKV, _TM_KI = 0, 256, 768, 896
_TM_COLS = 1024


def _proj_kernel(x_ref, g_ref, wfm_ref, wtm_ref, gc_ref, wkv_ref, wvT_ref,
                 qa_ref, va_ref, qb_ref, vb_ref, qc_ref, qi_ref, w_ref,
                 ka_ref, kb_ref, kmean_ref, kc_ref, vc_ref, ki_ref):
    tm = x_ref.shape[0]
    nsub = tm // TQ
    h = _rms(x_ref[...], g_ref[...]).astype(BF16)

    def fm(lo, hi, scale=None):
        y = _dot_nt(wfm_ref[lo:hi, :], h)
        return y if scale is None else y * scale

    def put_fm(ref, y):
        for r in range(nsub):
            ref[0, r] = y[:, r * TQ:(r + 1) * TQ].astype(ref.dtype)

    put_fm(qa_ref, fm(_FM_QA, _FM_VA, QK_SCALE))
    put_fm(va_ref, fm(_FM_VA, _FM_QB))
    put_fm(qb_ref, fm(_FM_QB, _FM_VB, QK_SCALE))
    vb_ref[0, 0] = fm(_FM_VB, _FM_QC).astype(BF16)
    put_fm(qc_ref, fm(_FM_QC, _FM_QI, QK_SCALE))
    put_fm(qi_ref, fm(_FM_QI, _FM_W))
    put_fm(w_ref, fm(_FM_W, _FM_ROWS, (N_IDX_HEADS ** -0.5) * (IDX_DIM ** -0.5)))

    ptm = _dot(h, wtm_ref[...])
    for kv in range(N_KV_A):
        lo = _TM_KA + kv * LANES
        ka_ref[0, kv] = ptm[:, lo:lo + HEAD_DIM].astype(BF16)
    for hb in range(N_HEADS_B):
        lo = _TM_KB + hb * LANES
        kb_ref[0, hb] = ptm[:, lo:lo + HEAD_DIM].astype(BF16)
    for r in range(nsub):
        kmean_ref[0, r] = jnp.mean(ptm[r * TQ:(r + 1) * TQ, _TM_KB:_TM_CKV], axis=0, keepdims=True)
    ki_ref[0] = ptm[:, _TM_KI:_TM_KI + IDX_DIM].astype(BF16)

    ckv = _rms(ptm[:, _TM_CKV:_TM_KI], gc_ref[...]).astype(BF16)
    kc_ref[0] = _dot(ckv, wkv_ref[...])[:, :HEAD_DIM].astype(BF16)
    vc_ref[0, 0] = _dot_nt(wvT_ref[...], ckv).astype(BF16)


def _proj(x, g, wfm, wtm, gc, wkv, wvT, batch, seq):
    tm = PROJ_TM
    assert seq % tm == 0 and tm == GK
    tpb = seq // tm
    nsub = tm // TQ
    nb = seq // TQ

    def fm_spec(rows):
        return pl.BlockSpec((1, nsub, rows, TQ), lambda i: (i // tpb, i % tpb, 0, 0))

    def fm_shape(rows, dtype=BF16):
        return jax.ShapeDtypeStruct((batch, nb, rows, TQ), dtype)

    def grp_spec(rows):
        return pl.BlockSpec((1, 1, rows, GK), lambda i: (i // tpb, i % tpb, 0, 0))

    def grp_shape(rows):
        return jax.ShapeDtypeStruct((batch, tpb, rows, GK), BF16)

    def full(a):
        return pl.BlockSpec(a.shape, lambda i: (0,) * a.ndim)

    out_specs = [fm_spec(A_Q), fm_spec(A_KV), fm_spec(B_QKV), grp_spec(B_QKV), fm_spec(C_Q), fm_spec(C_QIDX),
                 fm_spec(16),
                 pl.BlockSpec((1, N_KV_A, tm, HEAD_DIM), lambda i: (i // tpb, 0, i % tpb, 0)),
                 pl.BlockSpec((1, N_HEADS_B, tm, HEAD_DIM), lambda i: (i // tpb, 0, i % tpb, 0)),
                 pl.BlockSpec((1, nsub, 1, N_HEADS_B * LANES), lambda i: (i // tpb, i % tpb, 0, 0)),
                 pl.BlockSpec((1, tm, HEAD_DIM), lambda i: (i // tpb, i % tpb, 0)),
                 grp_spec(HEAD_DIM),
                 pl.BlockSpec((1, tm, IDX_DIM), lambda i: (i // tpb, i % tpb, 0))]
    out_shape = [fm_shape(A_Q), fm_shape(A_KV), fm_shape(B_QKV), grp_shape(B_QKV), fm_shape(C_Q), fm_shape(C_QIDX),
                 fm_shape(16, F32),
                 jax.ShapeDtypeStruct((batch, N_KV_A, seq, HEAD_DIM), BF16),
                 jax.ShapeDtypeStruct((batch, N_HEADS_B, seq, HEAD_DIM), BF16),
                 jax.ShapeDtypeStruct((batch, nb, 1, N_HEADS_B * LANES), F32),
                 jax.ShapeDtypeStruct((batch, seq, HEAD_DIM), BF16),
                 grp_shape(HEAD_DIM),
                 jax.ShapeDtypeStruct((batch, seq, IDX_DIM), BF16)]
    return pl.pallas_call(
        _proj_kernel,
        grid=(batch * tpb,),
        in_specs=[pl.BlockSpec((tm, D_MODEL), lambda i: (i, 0)),
                  full(g), full(wfm), full(wtm), full(gc), full(wkv), full(wvT)],
        out_specs=out_specs,
        out_shape=out_shape,
        compiler_params=_cparams(1),
        name="mix_in_proj",
    )(x, g, wfm, wtm, gc, wkv, wvT)


def _online_update(parts, v_parts, m_ref, l_ref, acc_ref, hh):
    m_prev = m_ref[hh]
    m_new = jnp.maximum(m_prev, jnp.max(functools.reduce(jnp.maximum, parts), axis=0, keepdims=True))
    m_safe = jnp.where(m_new == NEG_INF, 0.0, m_new)
    alpha = jnp.exp2(m_prev - m_safe)
    ps = [jnp.exp2(s - m_safe) for s in parts]
    pv = functools.reduce(lambda a, b: a + b, [_dot(v, p.astype(BF16)) for v, p in zip(v_parts, ps)])
    l_ref[hh] = alpha * l_ref[hh] + jnp.sum(functools.reduce(lambda a, b: a + b, ps), axis=0, keepdims=True)
    acc_ref[hh] = alpha * acc_ref[hh] + pv
    m_ref[hh] = m_new


def _init_stats(m_ref, l_ref, acc_ref, m_init=NEG_INF):
    m_ref[...] = jnp.full(m_ref.shape, m_init, F32)
    l_ref[...] = jnp.zeros(l_ref.shape, F32)
    acc_ref[...] = jnp.zeros(acc_ref.shape, F32)


LAZY_GUARD = 64.0


def _lazy_update(parts, v_parts, m_ref, l_ref, acc_ref, top_ref, bad_ref, hh):
    m_stab = m_ref[hh]
    ps = [jnp.exp2(s - m_stab) for s in parts]
    top = jnp.max(functools.reduce(jnp.maximum, parts), axis=0, keepdims=True)
    pv = functools.reduce(lambda a, b: a + b, [_dot(v, p.astype(BF16)) for v, p in zip(v_parts, ps)])
    m_new = jnp.maximum(m_stab, top)
    beta = jnp.exp2(m_stab - m_new)
    l_ref[hh] = (l_ref[hh] + jnp.sum(functools.reduce(lambda a, b: a + b, ps), axis=0, keepdims=True)) * beta
    acc_ref[hh] = (acc_ref[hh] + pv) * beta
    m_ref[hh] = m_new
    top_ref[hh] = jnp.maximum(top_ref[hh], top)
    bad_ref[...] = jnp.maximum(bad_ref[...], jnp.where(top - m_stab > LAZY_GUARD, 1.0, 0.0))


def _init_lazy(m_ref, l_ref, acc_ref, top_ref, bad_ref):
    _init_stats(m_ref, l_ref, acc_ref, 0.0)
    top_ref[...] = jnp.full(top_ref.shape, NEG_INF, F32)
    bad_ref[...] = jnp.zeros(bad_ref.shape, F32)


def _lazy_failed(top_ref, bad_ref, n_heads):
    low = functools.reduce(jnp.maximum, [jnp.where(top_ref[hh] < -LAZY_GUARD, 1.0, 0.0) for hh in range(n_heads)])
    return jnp.max(jnp.maximum(bad_ref[...], low)) > 0.5


def _finish(l_ref, acc_ref, n_heads):
    oT = jnp.concatenate([acc_ref[hh] * (1.0 / l_ref[hh]) for hh in range(n_heads)], axis=0)
    return oT.T


def _bias_tile(bias_ref, hh, j, i):
    kind = jnp.clip(j - i + 2, 0, 3)
    return bias_ref[hh, pl.ds(pl.multiple_of(kind * TQ, TQ), TQ), :]


def _group_bounds(i):
    n_groups = i // GB + 1
    n_far = jnp.maximum((i - 1) // GB, 0)
    return n_far, n_groups


def _for_last_group(i, fn):
    for nblk in range(1, GB + 1):
        @pl.when(i % GB == nblk - 1)
        def _():
            fn(nblk)


_SWA_BIAS_ROWS = TQ + WINDOW


def _swa_kernel(sink_ref, q_ref, k_ref, v_ref, bias_ref, o_ref):
    i = pl.program_id(1)
    g_heads = N_HEADS_A // N_KV_A
    start = pl.multiple_of(i * TQ, TQ)
    prev_start = pl.multiple_of(jnp.maximum(i * TQ - WINDOW, 0), WINDOW)
    prev_off = jnp.where(i > 0, 0.0, NEG_INF)
    k_main = [k_ref[0, kv, pl.ds(start, TQ), :] for kv in range(N_KV_A)]
    k_prev = [k_ref[0, kv, pl.ds(prev_start, WINDOW), :] for kv in range(N_KV_A)]
    v_main = [v_ref[0, i, kv * HEAD_DIM:(kv + 1) * HEAD_DIM, :] for kv in range(N_KV_A)]
    v_prev = [v_ref[0, jnp.maximum(i - 1, 0), kv * HEAD_DIM:(kv + 1) * HEAD_DIM, TQ - WINDOW:] for kv in range(N_KV_A)]

    def scores(h):
        kv = h // g_heads
        q = q_ref[0, 0, h * HEAD_DIM:(h + 1) * HEAD_DIM, :]
        s_main = _dot(k_main[kv], q) + bias_ref[h, 0:TQ, :]
        s_prev = _dot(k_prev[kv], q) + bias_ref[h, TQ:TQ + WINDOW, :] + prev_off
        return s_main, s_prev, sink_ref[h] * LOG2E

    def head_out(h, s_main, s_prev, sink, m):
        kv = h // g_heads
        p_main = jnp.exp2(s_main - m)
        p_prev = jnp.exp2(s_prev - m)
        denom = (jnp.sum(p_main, axis=0, keepdims=True) + jnp.sum(p_prev, axis=0, keepdims=True)
                 + jnp.exp2(sink - m))
        oT = _dot(v_main[kv], p_main.astype(BF16)) + _dot(v_prev[kv], p_prev.astype(BF16))
        return oT * (1.0 / denom)

    def tile_max(s_main, s_prev):
        return jnp.maximum(jnp.max(s_main, axis=0, keepdims=True), jnp.max(s_prev, axis=0, keepdims=True))

    outs, over = [], None
    for h in range(N_HEADS_A):
        s_main, s_prev, sink = scores(h)
        outs.append(head_out(h, s_main, s_prev, sink, sink))
        gap = tile_max(s_main, s_prev) - sink
        over = gap if over is None else jnp.maximum(over, gap)
    o_ref[0] = jnp.concatenate(outs, axis=0).T.astype(o_ref.dtype)

    @pl.when(jnp.max(over) > LAZY_GUARD)
    def _():
        outs = []
        for h in range(N_HEADS_A):
            s_main, s_prev, sink = scores(h)
            outs.append(head_out(h, s_main, s_prev, sink, jnp.maximum(tile_max(s_main, s_prev), sink)))
        o_ref[0] = jnp.concatenate(outs, axis=0).T.astype(o_ref.dtype)


def _swa(sinks, qaT, ka, vaT, bias, batch, seq):
    nb = seq // TQ
    return pl.pallas_call(
        _swa_kernel,
        grid=(batch, nb),
        in_specs=[pl.BlockSpec(memory_space=pltpu.SMEM),
                  pl.BlockSpec((1, 1, A_Q, TQ), lambda b, i: (b, i, 0, 0)),
                  pl.BlockSpec((1, N_KV_A, seq, HEAD_DIM), lambda b, i: (b, 0, 0, 0)),
                  pl.BlockSpec((1, nb, A_KV, TQ), lambda b, i: (b, 0, 0, 0)),
                  pl.BlockSpec((N_HEADS_A, _SWA_BIAS_ROWS, TQ), lambda b, i: (0, 0, 0))],
        out_specs=pl.BlockSpec((1, TQ, A_Q), lambda b, i: (b, i, 0)),
        out_shape=jax.ShapeDtypeStruct((batch, seq, A_Q), BF16),
        compiler_params=_cparams(2),
        name="swa_attn",
    )(sinks, qaT, ka, vaT, bias)


_MOBA_HPS = 4


def _moba_kernel(q_ref, k_ref, v_ref, kmean_ref, bias_ref, o_ref,
                 sel_ref, m_ref, l_ref, acc_ref, top_ref, bad_ref, *, n_sel):
    i = pl.program_id(2)
    nb = kmean_ref.shape[2]
    row = lax.broadcasted_iota(jnp.int32, (nb, TQ), 0)
    qs = [q_ref[0, 0, hh * HEAD_DIM:(hh + 1) * HEAD_DIM, :] for hh in range(_MOBA_HPS)]

    for hh in range(_MOBA_HPS):
        gate = _dot(kmean_ref[0, hh][:, :HEAD_DIM], qs[hh].astype(F32))
        gate = jnp.where(row < i, gate, NEG_INF)
        sel = row == i
        for _ in range(n_sel):
            best = jnp.max(gate, axis=0, keepdims=True)
            first = jnp.min(jnp.where(gate == best, row, nb), axis=0, keepdims=True)
            pick = (row == first) & (best > NEG_INF)
            sel = sel | pick
            gate = jnp.where(pick, NEG_INF, gate)
        sel_ref[hh] = jnp.where(sel, 0.0, NEG_INF)

    def group(g, mixed, lazy, nblk=GB):
        kbase = pl.multiple_of(g * GK, GK)
        s_all = [_dot(k_ref[0, hh, pl.ds(kbase, nblk * TQ), :], qs[hh]) for hh in range(_MOBA_HPS)]
        for hh in range(_MOBA_HPS):
            parts = []
            for b in range(nblk):
                j = g * GB + b
                sb = s_all[hh][b * TQ:(b + 1) * TQ] + sel_ref[hh, pl.ds(j, 1), :]
                if mixed:
                    sb = sb + _bias_tile(bias_ref, hh, j, i)
                parts.append(sb)
            v_parts = [v_ref[0, g, hh * HEAD_DIM:(hh + 1) * HEAD_DIM, b * TQ:(b + 1) * TQ] for b in range(nblk)]
            if lazy:
                _lazy_update(parts, v_parts, m_ref, l_ref, acc_ref, top_ref, bad_ref, hh)
            else:
                _online_update(parts, v_parts, m_ref, l_ref, acc_ref, hh)

    n_far, n_groups = _group_bounds(i)

    def all_groups(lazy):
        def far_body(g, carry):
            group(g, False, lazy)
            return carry

        def mixed_body(g, carry):
            group(g, True, lazy)
            return carry

        lax.fori_loop(0, n_far, far_body, 0)
        lax.fori_loop(n_far, n_groups - 1, mixed_body, 0)
        _for_last_group(i, lambda nblk: group(n_groups - 1, True, lazy, nblk))

    _init_lazy(m_ref, l_ref, acc_ref, top_ref, bad_ref)
    all_groups(True)

    @pl.when(_lazy_failed(top_ref, bad_ref, _MOBA_HPS))
    def _():
        _init_stats(m_ref, l_ref, acc_ref)
        all_groups(False)

    o_ref[0] = _finish(l_ref, acc_ref, _MOBA_HPS).astype(o_ref.dtype)


def _moba(qbT, kb, vbT, kmean, bias, batch, seq):
    nb = seq // TQ
    ng = seq // GK
    hps = _MOBA_HPS
    n_sel = min(MOBA_TOPK, nb - 1)
    bias_blk0 = N_HEADS_A // hps
    return pl.pallas_call(
        functools.partial(_moba_kernel, n_sel=n_sel),
        grid=(batch, N_HEADS_B // hps, nb),
        in_specs=[pl.BlockSpec((1, 1, hps * HEAD_DIM, TQ), lambda b, hp, i: (b, i, hp, 0)),
                  pl.BlockSpec((1, hps, seq, HEAD_DIM), lambda b, hp, i: (b, hp, 0, 0)),
                  pl.BlockSpec((1, ng, hps * HEAD_DIM, GK), lambda b, hp, i: (b, 0, hp, 0)),
                  pl.BlockSpec((1, hps, nb, LANES), lambda b, hp, i: (b, hp, 0, 0)),
                  pl.BlockSpec((hps, BIAS_ROWS, TQ), lambda b, hp, i: (bias_blk0 + hp, 0, 0))],
        out_specs=pl.BlockSpec((1, TQ, hps * HEAD_DIM), lambda b, hp, i: (b, i, hp)),
        out_shape=jax.ShapeDtypeStruct((batch, seq, B_QKV), BF16),
        scratch_shapes=[pltpu.VMEM((hps, nb, TQ), F32),
                        pltpu.VMEM((hps, 1, TQ), F32),
                        pltpu.VMEM((hps, 1, TQ), F32),
                        pltpu.VMEM((hps, HEAD_DIM, TQ), F32),
                        pltpu.VMEM((hps, 1, TQ), F32),
                        pltpu.VMEM((1, TQ), F32)],
        compiler_params=_cparams(3),
        name="moba_attn",
    )(qbT, kb, vbT, kmean, bias)


_PACK16 = 16
_I16_MIN, _I16_MAX = -32768, 32767
_N_COUNT_ACC = 4


def _f32_to_key(x):
    bits = lax.bitcast_convert_type(x, jnp.int32)
    key = bits ^ ((bits >> 31) & jnp.int32(0x7FFFFFFF))
    return jnp.where(key == -1, 0, key)


def _key_to_f32(key):
    bits = key ^ ((key >> 31) & jnp.int32(0x7FFFFFFF))
    return lax.bitcast_convert_type(bits, F32)


_MIN_NORMAL_KEY = 0x00800000


def _next_key(key):
    nxt = key + 1
    nxt = jnp.where((nxt >= -_MIN_NORMAL_KEY) & (nxt < 0), 0, nxt)
    return jnp.where((nxt > 0) & (nxt < _MIN_NORMAL_KEY), _MIN_NORMAL_KEY, nxt)


def _dsa_kernel(qi_ref, w_ref, ki_ref, q_ref, k_ref, v_ref, bias_ref, tri_ref, o_ref,
                score_ref, hi_ref, lo_ref, m_ref, l_ref, acc_ref, ties_ref, top_ref, bad_ref, part_ref, *, n_keep):
    i = pl.program_id(1)
    n_far, n_groups = _group_bounds(i)

    def rows(g, b=0):
        return pl.ds(pl.multiple_of(g * GK + b * TQ, TQ), TQ)

    w = w_ref[0, 0]
    qis = [qi_ref[0, 0, hi * IDX_DIM:(hi + 1) * IDX_DIM, :] for hi in range(N_IDX_HEADS)]
    krow = lax.broadcasted_iota(jnp.int32, (TQ, TQ), 0)
    qcol = lax.broadcasted_iota(jnp.int32, (TQ, TQ), 1)

    last = n_groups - 1

    def score_group(g, mixed, nblk=GB):
        for b in range(nblk):
            ki = ki_ref[0, rows(g, b), :]
            ds = [_dot(ki, qi) for qi in qis]
            sc = functools.reduce(lambda x, y: x + y,
                                  [jnp.maximum(d, 0.0) * w[hi:hi + 1, :] for hi, d in enumerate(ds)])
            if mixed:
                sc = jnp.where(krow + (g * GB + b) * TQ <= qcol + i * TQ, sc, NEG_INF)
            score_ref[rows(g, b), :] = sc
            key = _f32_to_key(sc)
            hi_ref[rows(g, b), :] = (key >> 16).astype(jnp.int16)
            lo_ref[rows(g, b), :] = ((key & 0xFFFF) + _I16_MIN).astype(jnp.int16)

    def score_far(g, carry):
        score_group(g, False)
        return carry

    def score_mixed(g, carry):
        score_group(g, True)
        return carry

    lax.fori_loop(0, n_far, score_far, 0)
    lax.fori_loop(n_far, last, score_mixed, 0)
    _for_last_group(i, lambda nblk: score_group(last, True, nblk))

    def group16(ref, g, nblk=GB):
        return ref[pl.ds(pl.multiple_of(g * GK, GK), nblk * TQ), :]

    def count_ge16(ref, cand):
        cand16 = jnp.broadcast_to(cand.astype(jnp.int16), (_PACK16, TQ))
        zero = jnp.zeros((_PACK16, TQ), jnp.int16)

        def count_group(g, accs, nblk=GB):
            accs = list(accs)
            grp = group16(ref, g, nblk)
            for r in range(nblk * TQ // _PACK16):
                a = r % _N_COUNT_ACC
                hit = grp[r * _PACK16:(r + 1) * _PACK16] >= cand16
                accs[a] = accs[a] + jnp.where(hit, jnp.int16(1), jnp.int16(0))
            return tuple(accs)

        def count_last(nblk):
            part_ref[...] = functools.reduce(lambda x, y: x + y, count_group(last, (zero,) * _N_COUNT_ACC, nblk))

        accs = lax.fori_loop(0, last, count_group, (zero,) * _N_COUNT_ACC)
        _for_last_group(i, count_last)
        total = functools.reduce(lambda x, y: x + y, accs).astype(jnp.int32) + part_ref[...].astype(jnp.int32)
        return jnp.sum(total, axis=0, keepdims=True)

    def kth_largest16(ref, k):
        def bit_body(b, carry):
            c, n_gt = carry
            c_try = c + lax.shift_left(jnp.int32(1), 15 - b)
            n = count_ge16(ref, c_try)
            ok = n >= k
            return jnp.where(ok, c_try, c), jnp.where(ok, n_gt, n)
        init = (jnp.full((1, TQ), _I16_MIN, jnp.int32), jnp.zeros((1, TQ), jnp.int32))
        return lax.fori_loop(0, 16, bit_body, init)

    c_hi, n_gt_hi = kth_largest16(hi_ref, n_keep)
    c_hi16 = c_hi.astype(jnp.int16)

    def mask_lo(g, carry, nblk=GB):
        sl = pl.ds(pl.multiple_of(g * GK, GK), nblk * TQ)
        lo_ref[sl, :] = jnp.where(hi_ref[sl, :] == c_hi16, lo_ref[sl, :], jnp.int16(_I16_MIN))
        return carry

    lax.fori_loop(0, last, mask_lo, 0)
    _for_last_group(i, lambda nblk: mask_lo(last, 0, nblk))
    c_lo, n_gt_lo = kth_largest16(lo_ref, n_keep - n_gt_hi)
    key_thr = lax.shift_left(c_hi, 16) | (c_lo - _I16_MIN)
    thr = _key_to_f32(key_thr)
    thr_next = _key_to_f32(_next_key(key_thr))
    tie_budget = (n_keep - (n_gt_hi + n_gt_lo)).astype(F32)

    qs = [q_ref[0, 0, hh * HEAD_DIM:(hh + 1) * HEAD_DIM, :] for hh in range(N_HEADS_C)]

    def selection_masks(g, nblk):
        blks = [score_ref[rows(g, b), :] for b in range(nblk)]
        ties = [jnp.where(blk == thr, 1.0, 0.0) for blk in blks]
        prefix = [_dot(tri_ref[...], tie.astype(BF16)) for tie in ties]
        masks = []
        seen = ties_ref[...]
        for blk, tie, pre in zip(blks, ties, prefix):
            bar = jnp.where(pre < tie_budget - seen, thr, thr_next)
            masks.append(jnp.where(blk >= bar, 0.0, NEG_INF))
            seen = seen + pre[TQ - 1:TQ, :] + tie[TQ - 1:TQ, :]
        ties_ref[...] = seen
        return masks

    def attend_group(g, mixed, lazy, nblk=GB):
        masks = selection_masks(g, nblk)
        kc = k_ref[0, pl.ds(pl.multiple_of(g * GK, GK), nblk * TQ), :]
        s_all = [_dot(kc, q) for q in qs]
        v_parts = [v_ref[0, g, :, b * TQ:(b + 1) * TQ] for b in range(nblk)]
        for hh in range(N_HEADS_C):
            parts = []
            for b in range(nblk):
                sb = s_all[hh][b * TQ:(b + 1) * TQ] + masks[b]
                if mixed:
                    sb = sb + _bias_tile(bias_ref, hh, g * GB + b, i)
                parts.append(sb)
            if lazy:
                _lazy_update(parts, v_parts, m_ref, l_ref, acc_ref, top_ref, bad_ref, hh)
            else:
                _online_update(parts, v_parts, m_ref, l_ref, acc_ref, hh)

    def attend_all_groups(lazy):
        def far_body(g, carry):
            attend_group(g, False, lazy)
            return carry

        def mixed_body(g, carry):
            attend_group(g, True, lazy)
            return carry

        ties_ref[...] = jnp.zeros(ties_ref.shape, F32)
        lax.fori_loop(0, n_far, far_body, 0)
        lax.fori_loop(n_far, last, mixed_body, 0)
        _for_last_group(i, lambda nblk: attend_group(last, True, lazy, nblk))

    _init_lazy(m_ref, l_ref, acc_ref, top_ref, bad_ref)
    attend_all_groups(True)

    @pl.when(_lazy_failed(top_ref, bad_ref, N_HEADS_C))
    def _():
        _init_stats(m_ref, l_ref, acc_ref)
        attend_all_groups(False)

    o_ref[0] = _finish(l_ref, acc_ref, N_HEADS_C).astype(o_ref.dtype)


def _dsa(qiT, wT, ki, qcT, kc, vcT, bias, batch, seq):
    nb = seq // TQ
    ng = seq // GK
    n_keep = min(DSA_TOPK, seq // 4)
    tri = jnp.asarray(np.tril(np.ones((TQ, TQ), np.float32), -1), BF16)
    bias_blk = (N_HEADS_A + N_HEADS_B) // N_HEADS_C
    return pl.pallas_call(
        functools.partial(_dsa_kernel, n_keep=n_keep),
        grid=(batch, nb),
        in_specs=[pl.BlockSpec((1, 1, C_QIDX, TQ), lambda b, i: (b, i, 0, 0)),
                  pl.BlockSpec((1, 1, 16, TQ), lambda b, i: (b, i, 0, 0)),
                  pl.BlockSpec((1, seq, IDX_DIM), lambda b, i: (b, 0, 0)),
                  pl.BlockSpec((1, 1, C_Q, TQ), lambda b, i: (b, i, 0, 0)),
                  pl.BlockSpec((1, seq, HEAD_DIM), lambda b, i: (b, 0, 0)),
                  pl.BlockSpec((1, ng, HEAD_DIM, GK), lambda b, i: (b, 0, 0, 0)),
                  pl.BlockSpec((N_HEADS_C, BIAS_ROWS, TQ), lambda b, i: (bias_blk, 0, 0)),
                  pl.BlockSpec((TQ, TQ), lambda b, i: (0, 0))],
        out_specs=pl.BlockSpec((1, TQ, C_Q), lambda b, i: (b, i, 0)),
        out_shape=jax.ShapeDtypeStruct((batch, seq, C_Q), BF16),
        scratch_shapes=[pltpu.VMEM((seq, TQ), F32),
                        pltpu.VMEM((seq, TQ), jnp.int16),
                        pltpu.VMEM((seq, TQ), jnp.int16),
                        pltpu.VMEM((N_HEADS_C, 1, TQ), F32),
                        pltpu.VMEM((N_HEADS_C, 1, TQ), F32),
                        pltpu.VMEM((N_HEADS_C, HEAD_DIM, TQ), F32),
                        pltpu.VMEM((1, TQ), F32),
                        pltpu.VMEM((N_HEADS_C, 1, TQ), F32),
                        pltpu.VMEM((1, TQ), F32),
                        pltpu.VMEM((_PACK16, TQ), jnp.int16)],
        compiler_params=_cparams(2),
        name="dsa_attn",
    )(qiT, wT, ki, qcT, kc, vcT, bias, tri)


def _pad_heads(w, n_heads):
    w = w.reshape(D_MODEL, n_heads, HEAD_DIM)
    return jnp.pad(w, ((0, 0), (0, 0), (0, LANES - HEAD_DIM))).reshape(D_MODEL, n_heads * LANES)


def _split_w_in(w_in):
    cuts = np.cumsum([A_Q, A_KV, A_KV, B_QKV, B_QKV, B_QKV, C_Q, C_KV_LATENT, C_QIDX, IDX_DIM])
    qa, ka, va, qb, kb, vb, qc, ckv, qidx, kidx, widx = jnp.split(w_in, cuts, axis=1)
    wfm = jnp.concatenate([qa, va, qb, vb, qc, qidx, jnp.pad(widx, ((0, 0), (0, 16 - N_IDX_HEADS)))], axis=1)
    wtm = jnp.concatenate([_pad_heads(ka, N_KV_A), _pad_heads(kb, N_HEADS_B), ckv, _pad_heads(kidx, 1)], axis=1)
    assert wfm.shape[1] == _FM_ROWS and wtm.shape[1] == _TM_COLS
    return wfm.T.astype(BF16), wtm.astype(BF16)


def kernel(x, rel_bias_table, ffn1_norm, ffn1_w_gate, ffn1_w_up, ffn1_w_down, mix_norm, w_in, attn_sinks, kv_norm_c, w_kv_up_c, w_out, ffn2_norm, ffn2_w_gate, ffn2_w_up, ffn2_w_down, final_norm):
    batch, seq = x.shape[0], x.shape[1]
    depth = w_in.shape[0]
    assert seq % GK == 0 and x.shape[2] == D_MODEL
    nb = seq // TQ
    bias = _bias_tiles(rel_bias_table)
    gf = final_norm.reshape(1, D_MODEL)
    xf = x.reshape(batch * seq, D_MODEL)
    for l in range(depth):
        xf = _ffn(xf, ffn1_norm[l].reshape(1, D_MODEL), ffn1_w_gate[l].astype(BF16), ffn1_w_up[l].astype(BF16),
                  ffn1_w_down[l].astype(BF16), gf, False)
        wfm, wtm = _split_w_in(w_in[l])
        wkv = w_kv_up_c[l].astype(BF16)
        (qaT, vaT, qbT, vbT, qcT, qiT, wT, ka, kb, kmean, kc, vcT, ki) = _proj(
            xf, mix_norm[l].reshape(1, D_MODEL), wfm, wtm, kv_norm_c[l].reshape(1, C_KV_LATENT),
            wkv, wkv[:, HEAD_DIM:].T, batch, seq)
        kmean = kmean.reshape(batch, nb, N_HEADS_B, LANES).transpose(0, 2, 1, 3)
        out_a = _swa(attn_sinks[l], qaT, ka, vaT, bias, batch, seq)
        out_b = _moba(qbT, kb, vbT, kmean, bias, batch, seq)
        out_c = _dsa(qiT, wT, ki, qcT, kc, vcT, bias, batch, seq)
        wo = w_out[l].astype(BF16)
        mix = (out_a.reshape(batch * seq, A_Q), out_b.reshape(batch * seq, B_QKV), out_c.reshape(batch * seq, C_Q),
               wo[:A_Q], wo[A_Q:A_Q + B_QKV], wo[A_Q + B_QKV:])
        xf = _ffn(xf, ffn2_norm[l].reshape(1, D_MODEL), ffn2_w_gate[l].astype(BF16), ffn2_w_up[l].astype(BF16),
                  ffn2_w_down[l].astype(BF16), gf, l == depth - 1, mix)
    return xf.reshape(batch, seq, D_MODEL)
```

```python
import functools
import math

import jax
import jax.numpy as jnp
import numpy as np
from jax import lax
from jax.experimental import pallas as pl
from jax.experimental.pallas import tpu as pltpu

D_MODEL = 1024
HEAD_DIM = 64
N_HEADS = 16
N_HEADS_A = 8
N_KV_A = 2
WINDOW = 128
N_HEADS_B = 4
MOBA_BLOCK = 256
MOBA_TOPK = 3
N_HEADS_C = 4
C_KV_LATENT = 128
N_IDX_HEADS = 4
IDX_DIM = 64
DSA_TOPK = 256
D_FF = 2816
N_BUCKETS = 32
MAX_DISTANCE = 128
RMS_EPS = 1e-6

A_Q = N_HEADS_A * HEAD_DIM
A_KV = N_KV_A * HEAD_DIM
B_QKV = N_HEADS_B * HEAD_DIM
C_Q = N_HEADS_C * HEAD_DIM
C_QIDX = N_IDX_HEADS * IDX_DIM

TQ = 256
GB = 4
GK = GB * TQ
SUBLANES = 8
LANES = 128
FFN_TM = 1024
FFN_TF = 256
PROJ_TM = GK
VMEM_LIMIT = 56 * 1024 * 1024

F32 = jnp.float32
BF16 = jnp.bfloat16
NEG_INF = float("-inf")
LOG2E = math.log2(math.e)
QK_SCALE = HEAD_DIM ** -0.5 * LOG2E

BIAS_ROWS = 4 * TQ

_NT = (((1,), (1,)), ((), ()))


def _cparams(n_axes):
    return pltpu.CompilerParams(dimension_semantics=("arbitrary",) * n_axes,
                                vmem_limit_bytes=VMEM_LIMIT)


def _dot(a, b):
    return jnp.dot(a, b, preferred_element_type=F32)


def _dot_nt(a, b):
    return lax.dot_general(a, b, _NT, preferred_element_type=F32)


def _rms(x, g):
    return x * lax.rsqrt(jnp.mean(x * x, axis=-1, keepdims=True) + RMS_EPS) * g


def _t5_bucket_np(dist):
    n = np.maximum(dist, 0)
    max_exact = N_BUCKETS // 2
    nf = np.maximum(n, 1).astype(np.float32)
    large = max_exact + (np.log(nf / np.float32(max_exact)) / np.float32(math.log(MAX_DISTANCE / max_exact))
                         * np.float32(N_BUCKETS - max_exact)).astype(np.int32)
    large = np.minimum(large, N_BUCKETS - 1)
    return np.where(n < max_exact, n, large).astype(np.int32)


_IDX_MASKED = -1
_IDX_ZERO = -2


def _bias_index_tiles():
    col = np.arange(TQ)[None, :]
    row = np.arange(TQ)[:, None]
    d_own = col - row
    d_prev = TQ + col - row
    d_prev_w = WINDOW + col - np.arange(WINDOW)[:, None]
    a = np.full((BIAS_ROWS, TQ), _IDX_MASKED, np.int32)
    a[:TQ] = np.where((d_own >= 0) & (d_own < WINDOW), _t5_bucket_np(d_own), _IDX_MASKED)
    a[TQ:TQ + WINDOW] = np.where((d_prev_w >= 0) & (d_prev_w < WINDOW), _t5_bucket_np(d_prev_w), _IDX_MASKED)
    b = np.full((BIAS_ROWS, TQ), _IDX_MASKED, np.int32)
    b[:TQ] = _IDX_ZERO
    b[TQ:2 * TQ] = _t5_bucket_np(d_prev)
    b[2 * TQ:3 * TQ] = np.where(d_own >= 0, _t5_bucket_np(d_own), _IDX_MASKED)
    assert (_t5_bucket_np(np.arange(2 * TQ, 64 * TQ)) == N_BUCKETS - 1).all()
    return np.stack([a, b])


def _bias_kernel(tab_ref, idx_ref, o_ref):
    h = pl.program_id(0)
    idx = idx_ref[0]
    shift = jnp.where(h >= N_HEADS_A, tab_ref[N_BUCKETS - 1, h], 0.0)
    out = jnp.where(idx == _IDX_ZERO, 0.0, NEG_INF)
    for b in range(N_BUCKETS):
        out = jnp.where(idx == b, (tab_ref[b, h] - shift) * LOG2E, out)
    o_ref[0] = out


def _bias_tiles(rel_bias_table):
    idx = jnp.asarray(_bias_index_tiles())
    return pl.pallas_call(
        _bias_kernel,
        grid=(N_HEADS,),
        in_specs=[pl.BlockSpec(memory_space=pltpu.SMEM),
                  pl.BlockSpec((1, BIAS_ROWS, TQ), lambda h: (h // N_HEADS_A, 0, 0))],
        out_specs=pl.BlockSpec((1, BIAS_ROWS, TQ), lambda h: (h, 0, 0)),
        out_shape=jax.ShapeDtypeStruct((N_HEADS, BIAS_ROWS, TQ), F32),
        compiler_params=_cparams(1),
        name="t5_bias_tiles",
    )(rel_bias_table, idx)


def _swiglu_half_step(x, g_ref, wg_ref, wu_ref, wd_ref, gf_ref, final_norm):
    h = _rms(x, g_ref[...]).astype(BF16)
    down = None
    for c in range(D_FF // FFN_TF):
        cols = slice(c * FFN_TF, (c + 1) * FFN_TF)
        gate = _dot(h, wg_ref[:, cols])
        up = _dot(h, wu_ref[:, cols])
        act = (gate * jax.nn.sigmoid(gate) * up).astype(BF16)
        d = _dot(act, wd_ref[cols, :])
        down = d if down is None else down + d
    y = x + 0.5 * down
    return _rms(y, gf_ref[...]) if final_norm else y


def _ffn_kernel(x_ref, g_ref, wg_ref, wu_ref, wd_ref, gf_ref, o_ref, *, final_norm):
    o_ref[...] = _swiglu_half_step(x_ref[...], g_ref, wg_ref, wu_ref, wd_ref, gf_ref, final_norm)


def _mix_out_ffn_kernel(x_ref, a_ref, b_ref, c_ref, wa_ref, wb_ref, wc_ref,
                        g_ref, wg_ref, wu_ref, wd_ref, gf_ref, o_ref, *, final_norm):
    x = (x_ref[...] + _dot(a_ref[...], wa_ref[...]) + _dot(b_ref[...], wb_ref[...])
         + _dot(c_ref[...], wc_ref[...]))
    o_ref[...] = _swiglu_half_step(x, g_ref, wg_ref, wu_ref, wd_ref, gf_ref, final_norm)


def _resident(a):
    return pl.BlockSpec(a.shape, lambda i: (0, 0), pipeline_mode=pl.Buffered(1))


def _ffn(x, g, wg, wu, wd, gf, final_norm, mix=None):
    n = x.shape[0]
    tm = min(FFN_TM, n)
    assert n % tm == 0 and D_FF % FFN_TF == 0

    def rows(cols):
        return pl.BlockSpec((tm, cols), lambda i: (i, 0))

    weights = (g, wg, wu, wd, gf)
    if mix is None:
        body, operands = _ffn_kernel, (x,) + weights
        in_specs = [rows(D_MODEL)] + [_resident(w) for w in weights]
    else:
        body, operands = _mix_out_ffn_kernel, (x,) + tuple(mix) + weights
        in_specs = ([rows(D_MODEL)] + [rows(t.shape[1]) for t in mix[:3]] + [_resident(w) for w in mix[3:]]
                    + [_resident(w) for w in weights])
    return pl.pallas_call(
        functools.partial(body, final_norm=final_norm),
        grid=(n // tm,),
        in_specs=in_specs,
        out_specs=rows(D_MODEL),
        out_shape=jax.ShapeDtypeStruct((n, D_MODEL), F32),
        compiler_params=_cparams(1),
        name="swiglu_ffn" if mix is None else "mix_out_ffn",
    )(*operands)


_FM_QA, _FM_VA, _FM_QB, _FM_VB, _FM_QC, _FM_QI, _FM_W = 0, 512, 640, 896, 1152, 1408, 1664
_FM_ROWS = 1680
_TM_KA, _TM_KB, _TM_CKV, _TM_KI = 0, 256, 768, 896
_TM_COLS = 1024


def _proj_kernel(x_ref, g_ref, wfm_ref, wtm_ref, gc_ref, wkv_ref, wvT_ref,
                 qa_ref, va_ref, qb_ref, vb_ref, qc_ref, qi_ref, w_ref,
                 ka_ref, kb_ref, kmean_ref, kc_ref, vc_ref, ki_ref):
    tm = x_ref.shape[0]
    nsub = tm // TQ
    h = _rms(x_ref[...], g_ref[...]).astype(BF16)

    def fm(lo, hi, scale=None):
        y = _dot_nt(wfm_ref[lo:hi, :], h)
        return y if scale is None else y * scale

    def put_fm(ref, y):
        for r in range(nsub):
            ref[0, r] = y[:, r * TQ:(r + 1) * TQ].astype(ref.dtype)

    put_fm(qa_ref, fm(_FM_QA, _FM_VA, QK_SCALE))
    put_fm(va_ref, fm(_FM_VA, _FM_QB))
    put_fm(qb_ref, fm(_FM_QB, _FM_VB, QK_SCALE))
    vb_ref[0, 0] = fm(_FM_VB, _FM_QC).astype(BF16)
    put_fm(qc_ref, fm(_FM_QC, _FM_QI, QK_SCALE))
    put_fm(qi_ref, fm(_FM_QI, _FM_W))
    put_fm(w_ref, fm(_FM_W, _FM_ROWS, (N_IDX_HEADS ** -0.5) * (IDX_DIM ** -0.5)))

    ptm = _dot(h, wtm_ref[...])
    for kv in range(N_KV_A):
        lo = _TM_KA + kv * LANES
        ka_ref[0, kv] = ptm[:, lo:lo + HEAD_DIM].astype(BF16)
    for hb in range(N_HEADS_B):
        lo = _TM_KB + hb * LANES
        kb_ref[0, hb] = ptm[:, lo:lo + HEAD_DIM].astype(BF16)
    for r in range(nsub):
        kmean_ref[0, r] = jnp.mean(ptm[r * TQ:(r + 1) * TQ, _TM_KB:_TM_CKV], axis=0, keepdims=True)
    ki_ref[0] = ptm[:, _TM_KI:_TM_KI + IDX_DIM].astype(BF16)

    ckv = _rms(ptm[:, _TM_CKV:_TM_KI], gc_ref[...]).astype(BF16)
    kc_ref[0] = _dot(ckv, wkv_ref[...])[:, :HEAD_DIM].astype(BF16)
    vc_ref[0, 0] = _dot_nt(wvT_ref[...], ckv).astype(BF16)


def _proj(x, g, wfm, wtm, gc, wkv, wvT, batch, seq):
    tm = PROJ_TM
    assert seq % tm == 0 and tm == GK
    tpb = seq // tm
    nsub = tm // TQ
    nb = seq // TQ

    def fm_spec(rows):
        return pl.BlockSpec((1, nsub, rows, TQ), lambda i: (i // tpb, i % tpb, 0, 0))

    def fm_shape(rows, dtype=BF16):
        return jax.ShapeDtypeStruct((batch, nb, rows, TQ), dtype)

    def grp_spec(rows):
        return pl.BlockSpec((1, 1, rows, GK), lambda i: (i // tpb, i % tpb, 0, 0))

    def grp_shape(rows):
        return jax.ShapeDtypeStruct((batch, tpb, rows, GK), BF16)

    def full(a):
        return pl.BlockSpec(a.shape, lambda i: (0,) * a.ndim)

    out_specs = [fm_spec(A_Q), fm_spec(A_KV), fm_spec(B_QKV), grp_spec(B_QKV), fm_spec(C_Q), fm_spec(C_QIDX),
                 fm_spec(16),
                 pl.BlockSpec((1, N_KV_A, tm, HEAD_DIM), lambda i: (i // tpb, 0, i % tpb, 0)),
                 pl.BlockSpec((1, N_HEADS_B, tm, HEAD_DIM), lambda i: (i // tpb, 0, i % tpb, 0)),
                 pl.BlockSpec((1, nsub, 1, N_HEADS_B * LANES), lambda i: (i // tpb, i % tpb, 0, 0)),
                 pl.BlockSpec((1, tm, HEAD_DIM), lambda i: (i // tpb, i % tpb, 0)),
                 grp_spec(HEAD_DIM),
                 pl.BlockSpec((1, tm, IDX_DIM), lambda i: (i // tpb, i % tpb, 0))]
    out_shape = [fm_shape(A_Q), fm_shape(A_KV), fm_shape(B_QKV), grp_shape(B_QKV), fm_shape(C_Q), fm_shape(C_QIDX),
                 fm_shape(16, F32),
                 jax.ShapeDtypeStruct((batch, N_KV_A, seq, HEAD_DIM), BF16),
                 jax.ShapeDtypeStruct((batch, N_HEADS_B, seq, HEAD_DIM), BF16),
                 jax.ShapeDtypeStruct((batch, nb, 1, N_HEADS_B * LANES), F32),
                 jax.ShapeDtypeStruct((batch, seq, HEAD_DIM), BF16),
                 grp_shape(HEAD_DIM),
                 jax.ShapeDtypeStruct((batch, seq, IDX_DIM), BF16)]
    return pl.pallas_call(
        _proj_kernel,
        grid=(batch * tpb,),
        in_specs=[pl.BlockSpec((tm, D_MODEL), lambda i: (i, 0)),
                  full(g), full(wfm), full(wtm), full(gc), full(wkv), full(wvT)],
        out_specs=out_specs,
        out_shape=out_shape,
        compiler_params=_cparams(1),
        name="mix_in_proj",
    )(x, g, wfm, wtm, gc, wkv, wvT)


def _online_update(parts, v_parts, m_ref, l_ref, acc_ref, hh):
    m_prev = m_ref[hh]
    m_new = jnp.maximum(m_prev, jnp.max(functools.reduce(jnp.maximum, parts), axis=0, keepdims=True))
    m_safe = jnp.where(m_new == NEG_INF, 0.0, m_new)
    alpha = jnp.exp2(m_prev - m_safe)
    ps = [jnp.exp2(s - m_safe) for s in parts]
    pv = functools.reduce(lambda a, b: a + b, [_dot(v, p.astype(BF16)) for v, p in zip(v_parts, ps)])
    l_ref[hh] = alpha * l_ref[hh] + jnp.sum(functools.reduce(lambda a, b: a + b, ps), axis=0, keepdims=True)
    acc_ref[hh] = alpha * acc_ref[hh] + pv
    m_ref[hh] = m_new


def _init_stats(m_ref, l_ref, acc_ref):
    m_ref[...] = jnp.full(m_ref.shape, NEG_INF, F32)
    l_ref[...] = jnp.zeros(l_ref.shape, F32)
    acc_ref[...] = jnp.zeros(acc_ref.shape, F32)


LAZY_SUM_MIN, LAZY_SUM_MAX = 2.0 ** -64, 2.0 ** 64


def _lazy_update(parts, v_parts, l_ref, acc_ref, hh):
    ps = [jnp.exp2(s) for s in parts]
    pv = functools.reduce(lambda a, b: a + b, [_dot(v, p.astype(BF16)) for v, p in zip(v_parts, ps)])
    l_ref[hh] += jnp.sum(functools.reduce(lambda a, b: a + b, ps), axis=0, keepdims=True)
    acc_ref[hh] += pv


def _lazy_sum_ok(total):
    return jnp.where((total > LAZY_SUM_MIN) & (total < LAZY_SUM_MAX), 1.0, 0.0)


def _lazy_failed(l_ref, n_heads):
    ok = functools.reduce(jnp.minimum, [_lazy_sum_ok(l_ref[hh]) for hh in range(n_heads)])
    return jnp.min(ok) < 0.5


def _finish(l_ref, acc_ref, n_heads):
    oT = jnp.concatenate([acc_ref[hh] * (1.0 / l_ref[hh]) for hh in range(n_heads)], axis=0)
    return oT.T


def _bias_tile(bias_ref, hh, j, i):
    kind = jnp.clip(j - i + 2, 0, 3)
    return bias_ref[hh, pl.ds(pl.multiple_of(kind * TQ, TQ), TQ), :]


def _group_bounds(i):
    n_groups = i // GB + 1
    n_far = jnp.maximum((i - 1) // GB, 0)
    return n_far, n_groups


def _for_last_group(i, fn):
    for nblk in range(1, GB + 1):
        @pl.when(i % GB == nblk - 1)
        def _():
            fn(nblk)


_SWA_BIAS_ROWS = TQ + WINDOW


def _swa_kernel(sink_ref, q_ref, k_ref, v_ref, bias_ref, o_ref):
    i = pl.program_id(1)
    g_heads = N_HEADS_A // N_KV_A
    start = pl.multiple_of(i * TQ, TQ)
    prev_start = pl.multiple_of(jnp.maximum(i * TQ - WINDOW, 0), WINDOW)
    prev_off = jnp.where(i > 0, 0.0, NEG_INF)
    k_main = [k_ref[0, kv, pl.ds(start, TQ), :] for kv in range(N_KV_A)]
    k_prev = [k_ref[0, kv, pl.ds(prev_start, WINDOW), :] for kv in range(N_KV_A)]
    v_main = [v_ref[0, i, kv * HEAD_DIM:(kv + 1) * HEAD_DIM, :] for kv in range(N_KV_A)]
    v_prev = [v_ref[0, jnp.maximum(i - 1, 0), kv * HEAD_DIM:(kv + 1) * HEAD_DIM, TQ - WINDOW:] for kv in range(N_KV_A)]

    def scores(h):
        kv = h // g_heads
        q = q_ref[0, 0, h * HEAD_DIM:(h + 1) * HEAD_DIM, :]
        s_main = _dot(k_main[kv], q) + bias_ref[h, 0:TQ, :]
        s_prev = _dot(k_prev[kv], q) + bias_ref[h, TQ:TQ + WINDOW, :] + prev_off
        return s_main, s_prev, sink_ref[h] * LOG2E

    def head_out(h, s_main, s_prev, sink, m):
        kv = h // g_heads
        p_main = jnp.exp2(s_main - m)
        p_prev = jnp.exp2(s_prev - m)
        denom = (jnp.sum(p_main, axis=0, keepdims=True) + jnp.sum(p_prev, axis=0, keepdims=True)
                 + jnp.exp2(sink - m))
        oT = _dot(v_main[kv], p_main.astype(BF16)) + _dot(v_prev[kv], p_prev.astype(BF16))
        return oT * (1.0 / denom), denom

    all_scores = [scores(h) for h in range(N_HEADS_A)]
    outs, ok = [], None
    for h in range(N_HEADS_A):
        s_main, s_prev, sink = all_scores[h]
        out, denom = head_out(h, s_main, s_prev, sink, 0.0)
        outs.append(out)
        ok = _lazy_sum_ok(denom) if ok is None else jnp.minimum(ok, _lazy_sum_ok(denom))
    o_ref[0] = jnp.concatenate(outs, axis=0).T.astype(o_ref.dtype)

    @pl.when(jnp.min(ok) < 0.5)
    def _():
        outs = []
        for h in range(N_HEADS_A):
            s_main, s_prev, sink = scores(h)
            m = jnp.maximum(jnp.maximum(jnp.max(s_main, axis=0, keepdims=True),
                                        jnp.max(s_prev, axis=0, keepdims=True)), sink)
            outs.append(head_out(h, s_main, s_prev, sink, m)[0])
        o_ref[0] = jnp.concatenate(outs, axis=0).T.astype(o_ref.dtype)


def _swa(sinks, qaT, ka, vaT, bias, batch, seq):
    nb = seq // TQ
    return pl.pallas_call(
        _swa_kernel,
        grid=(batch, nb),
        in_specs=[pl.BlockSpec(memory_space=pltpu.SMEM),
                  pl.BlockSpec((1, 1, A_Q, TQ), lambda b, i: (b, i, 0, 0)),
                  pl.BlockSpec((1, N_KV_A, seq, HEAD_DIM), lambda b, i: (b, 0, 0, 0)),
                  pl.BlockSpec((1, nb, A_KV, TQ), lambda b, i: (b, 0, 0, 0)),
                  pl.BlockSpec((N_HEADS_A, _SWA_BIAS_ROWS, TQ), lambda b, i: (0, 0, 0))],
        out_specs=pl.BlockSpec((1, TQ, A_Q), lambda b, i: (b, i, 0)),
        out_shape=jax.ShapeDtypeStruct((batch, seq, A_Q), BF16),
        compiler_params=_cparams(2),
        name="swa_attn",
    )(sinks, qaT, ka, vaT, bias)


_MOBA_HPS = 4


def _moba_kernel(q_ref, k_ref, v_ref, kmean_ref, bias_ref, o_ref,
                 sel_ref, m_ref, l_ref, acc_ref, *, n_sel):
    i = pl.program_id(2)
    nb = kmean_ref.shape[2]
    row = lax.broadcasted_iota(jnp.int32, (nb, TQ), 0)
    qs = [q_ref[0, 0, hh * HEAD_DIM:(hh + 1) * HEAD_DIM, :] for hh in range(_MOBA_HPS)]

    for hh in range(_MOBA_HPS):
        gate = _dot(kmean_ref[0, hh][:, :HEAD_DIM], qs[hh].astype(F32))
        gate = jnp.where(row < i, gate, NEG_INF)
        sel = row == i
        for _ in range(n_sel):
            best = jnp.max(gate, axis=0, keepdims=True)
            first = jnp.min(jnp.where(gate == best, row, nb), axis=0, keepdims=True)
            pick = (row == first) & (best > NEG_INF)
            sel = sel | pick
            gate = jnp.where(pick, NEG_INF, gate)
        sel_ref[hh] = jnp.where(sel, 0.0, NEG_INF)

    def group(g, mixed, lazy, nblk=GB):
        kbase = pl.multiple_of(g * GK, GK)
        s_all = [_dot(k_ref[0, hh, pl.ds(kbase, nblk * TQ), :], qs[hh]) for hh in range(_MOBA_HPS)]
        for hh in range(_MOBA_HPS):
            parts = []
            for b in range(nblk):
                j = g * GB + b
                sb = s_all[hh][b * TQ:(b + 1) * TQ] + sel_ref[hh, pl.ds(j, 1), :]
                if mixed:
                    sb = sb + _bias_tile(bias_ref, hh, j, i)
                parts.append(sb)
            v_parts = [v_ref[0, g, hh * HEAD_DIM:(hh + 1) * HEAD_DIM, b * TQ:(b + 1) * TQ] for b in range(nblk)]
            if lazy:
                _lazy_update(parts, v_parts, l_ref, acc_ref, hh)
            else:
                _online_update(parts, v_parts, m_ref, l_ref, acc_ref, hh)

    n_far, n_groups = _group_bounds(i)

    def all_groups(lazy):
        def far_body(g, carry):
            group(g, False, lazy)
            return carry

        def mixed_body(g, carry):
            group(g, True, lazy)
            return carry

        lax.fori_loop(0, n_far, far_body, 0)
        lax.fori_loop(n_far, n_groups - 1, mixed_body, 0)
        _for_last_group(i, lambda nblk: group(n_groups - 1, True, lazy, nblk))

    _init_stats(m_ref, l_ref, acc_ref)
    all_groups(True)

    @pl.when(_lazy_failed(l_ref, _MOBA_HPS))
    def _():
        _init_stats(m_ref, l_ref, acc_ref)
        all_groups(False)

    o_ref[0] = _finish(l_ref, acc_ref, _MOBA_HPS).astype(o_ref.dtype)


def _moba(qbT, kb, vbT, kmean, bias, batch, seq):
    nb = seq // TQ
    ng = seq // GK
    hps = _MOBA_HPS
    n_sel = min(MOBA_TOPK, nb - 1)
    bias_blk0 = N_HEADS_A // hps
    return pl.pallas_call(
        functools.partial(_moba_kernel, n_sel=n_sel),
        grid=(batch, N_HEADS_B // hps, nb),
        in_specs=[pl.BlockSpec((1, 1, hps * HEAD_DIM, TQ), lambda b, hp, i: (b, i, hp, 0)),
                  pl.BlockSpec((1, hps, seq, HEAD_DIM), lambda b, hp, i: (b, hp, 0, 0)),
                  pl.BlockSpec((1, ng, hps * HEAD_DIM, GK), lambda b, hp, i: (b, 0, hp, 0)),
                  pl.BlockSpec((1, hps, nb, LANES), lambda b, hp, i: (b, hp, 0, 0)),
                  pl.BlockSpec((hps, BIAS_ROWS, TQ), lambda b, hp, i: (bias_blk0 + hp, 0, 0))],
        out_specs=pl.BlockSpec((1, TQ, hps * HEAD_DIM), lambda b, hp, i: (b, i, hp)),
        out_shape=jax.ShapeDtypeStruct((batch, seq, B_QKV), BF16),
        scratch_shapes=[pltpu.VMEM((hps, nb, TQ), F32),
                        pltpu.VMEM((hps, 1, TQ), F32),
                        pltpu.VMEM((hps, 1, TQ), F32),
                        pltpu.VMEM((hps, HEAD_DIM, TQ), F32)],
        compiler_params=_cparams(3),
        name="moba_attn",
    )(qbT, kb, vbT, kmean, bias)


_PACK16 = 16
_I16_MIN, _I16_MAX = -32768, 32767
_N_COUNT_ACC = 4


def _f32_to_key(x):
    bits = lax.bitcast_convert_type(x, jnp.int32)
    key = bits ^ ((bits >> 31) & jnp.int32(0x7FFFFFFF))
    return jnp.where(key == -1, 0, key)


def _key_to_f32(key):
    bits = key ^ ((key >> 31) & jnp.int32(0x7FFFFFFF))
    return lax.bitcast_convert_type(bits, F32)


_MIN_NORMAL_KEY = 0x00800000


def _next_key(key):
    nxt = key + 1
    nxt = jnp.where((nxt >= -_MIN_NORMAL_KEY) & (nxt < 0), 0, nxt)
    return jnp.where((nxt > 0) & (nxt < _MIN_NORMAL_KEY), _MIN_NORMAL_KEY, nxt)


def _dsa_kernel(qi_ref, w_ref, ki_ref, q_ref, k_ref, v_ref, bias_ref, tri_ref, o_ref,
                score_ref, hi_ref, lo_ref, m_ref, l_ref, acc_ref, ties_ref, *, n_keep):
    i = pl.program_id(1)
    n_far, n_groups = _group_bounds(i)

    def rows(g, b=0):
        return pl.ds(pl.multiple_of(g * GK + b * TQ, TQ), TQ)

    w = w_ref[0, 0]
    qis = [qi_ref[0, 0, hi * IDX_DIM:(hi + 1) * IDX_DIM, :] for hi in range(N_IDX_HEADS)]
    krow = lax.broadcasted_iota(jnp.int32, (TQ, TQ), 0)
    qcol = lax.broadcasted_iota(jnp.int32, (TQ, TQ), 1)

    last = n_groups - 1

    def score_group(g, mixed, nblk=GB):
        for b in range(nblk):
            ki = ki_ref[0, rows(g, b), :]
            ds = [_dot(ki, qi) for qi in qis]
            sc = functools.reduce(lambda x, y: x + y,
                                  [jnp.maximum(d, 0.0) * w[hi:hi + 1, :] for hi, d in enumerate(ds)])
            if mixed:
                sc = jnp.where(krow + (g * GB + b) * TQ <= qcol + i * TQ, sc, NEG_INF)
            score_ref[rows(g, b), :] = sc
            key = _f32_to_key(sc)
            hi_ref[rows(g, b), :] = (key >> 16).astype(jnp.int16)
            lo_ref[rows(g, b), :] = ((key & 0xFFFF) + _I16_MIN).astype(jnp.int16)
        for b in range(nblk, GB):
            hi_ref[rows(g, b), :] = jnp.full((TQ, TQ), _I16_MIN, jnp.int16)
            lo_ref[rows(g, b), :] = jnp.full((TQ, TQ), _I16_MIN, jnp.int16)

    def score_far(g, carry):
        score_group(g, False)
        return carry

    def score_mixed(g, carry):
        score_group(g, True)
        return carry

    lax.fori_loop(0, n_far, score_far, 0)
    lax.fori_loop(n_far, last, score_mixed, 0)
    _for_last_group(i, lambda nblk: score_group(last, True, nblk))

    def group16(ref, g):
        return ref[pl.ds(pl.multiple_of(g * GK, GK), GK), :]

    def count_ge16(ref, cand):
        cand16 = jnp.broadcast_to(cand.astype(jnp.int16), (_PACK16, TQ))

        def body(g, accs):
            accs = list(accs)
            grp = group16(ref, g)
            for r in range(GK // _PACK16):
                a = r % _N_COUNT_ACC
                hit = grp[r * _PACK16:(r + 1) * _PACK16] >= cand16
                accs[a] = accs[a] + jnp.where(hit, jnp.int16(1), jnp.int16(0))
            return tuple(accs)

        zero = jnp.zeros((_PACK16, TQ), jnp.int16)
        accs = lax.fori_loop(0, n_groups, body, (zero,) * _N_COUNT_ACC)
        total = functools.reduce(lambda x, y: x + y, accs).astype(jnp.int32)
        return jnp.sum(total, axis=0, keepdims=True)

    def kth_largest16(ref, k):
        def bit_body(b, carry):
            c, n_gt = carry
            c_try = c + lax.shift_left(jnp.int32(1), 15 - b)
            n = count_ge16(ref, c_try)
            ok = n >= k
            return jnp.where(ok, c_try, c), jnp.where(ok, n_gt, n)
        init = (jnp.full((1, TQ), _I16_MIN, jnp.int32), jnp.zeros((1, TQ), jnp.int32))
        return lax.fori_loop(0, 16, bit_body, init)

    c_hi, n_gt_hi = kth_largest16(hi_ref, n_keep)
    c_hi16 = c_hi.astype(jnp.int16)

    def mask_lo(g, carry):
        sl = pl.ds(pl.multiple_of(g * GK, GK), GK)
        lo_ref[sl, :] = jnp.where(hi_ref[sl, :] == c_hi16, lo_ref[sl, :], jnp.int16(_I16_MIN))
        return carry

    lax.fori_loop(0, n_groups, mask_lo, 0)
    c_lo, n_gt_lo = kth_largest16(lo_ref, n_keep - n_gt_hi)
    key_thr = lax.shift_left(c_hi, 16) | (c_lo - _I16_MIN)
    thr = _key_to_f32(key_thr)
    thr_next = _key_to_f32(_next_key(key_thr))
    tie_budget = (n_keep - (n_gt_hi + n_gt_lo)).astype(F32)

    qs = [q_ref[0, 0, hh * HEAD_DIM:(hh + 1) * HEAD_DIM, :] for hh in range(N_HEADS_C)]

    def selection_masks(g, nblk):
        blks = [score_ref[rows(g, b), :] for b in range(nblk)]
        ties = [jnp.where(blk == thr, 1.0, 0.0) for blk in blks]
        prefix = [_dot(tri_ref[...], tie.astype(BF16)) for tie in ties]
        masks = []
        seen = ties_ref[...]
        for blk, tie, pre in zip(blks, ties, prefix):
            bar = jnp.where(pre < tie_budget - seen, thr, thr_next)
            masks.append(jnp.where(blk >= bar, 0.0, NEG_INF))
            seen = seen + pre[TQ - 1:TQ, :] + tie[TQ - 1:TQ, :]
        ties_ref[...] = seen
        return masks

    def attend_group(g, mixed, lazy, nblk=GB):
        masks = selection_masks(g, nblk)
        kc = k_ref[0, pl.ds(pl.multiple_of(g * GK, GK), nblk * TQ), :]
        s_all = [_dot(kc, q) for q in qs]
        v_parts = [v_ref[0, g, :, b * TQ:(b + 1) * TQ] for b in range(nblk)]
        for hh in range(N_HEADS_C):
            parts = []
            for b in range(nblk):
                sb = s_all[hh][b * TQ:(b + 1) * TQ] + masks[b]
                if mixed:
                    sb = sb + _bias_tile(bias_ref, hh, g * GB + b, i)
                parts.append(sb)
            if lazy:
                _lazy_update(parts, v_parts, l_ref, acc_ref, hh)
            else:
                _online_update(parts, v_parts, m_ref, l_ref, acc_ref, hh)

    def attend_all_groups(lazy):
        def far_body(g, carry):
            attend_group(g, False, lazy)
            return carry

        def mixed_body(g, carry):
            attend_group(g, True, lazy)
            return carry

        ties_ref[...] = jnp.zeros(ties_ref.shape, F32)
        lax.fori_loop(0, n_far, far_body, 0)
        lax.fori_loop(n_far, last, mixed_body, 0)
        _for_last_group(i, lambda nblk: attend_group(last, True, lazy, nblk))

    _init_stats(m_ref, l_ref, acc_ref)
    attend_all_groups(True)

    @pl.when(_lazy_failed(l_ref, N_HEADS_C))
    def _():
        _init_stats(m_ref, l_ref, acc_ref)
        attend_all_groups(False)

    o_ref[0] = _finish(l_ref, acc_ref, N_HEADS_C).astype(o_ref.dtype)


def _dsa(qiT, wT, ki, qcT, kc, vcT, bias, batch, seq):
    nb = seq // TQ
    ng = seq // GK
    n_keep = min(DSA_TOPK, seq // 4)
    tri = jnp.asarray(np.tril(np.ones((TQ, TQ), np.float32), -1), BF16)
    bias_blk = (N_HEADS_A + N_HEADS_B) // N_HEADS_C
    return pl.pallas_call(
        functools.partial(_dsa_kernel, n_keep=n_keep),
        grid=(batch, nb),
        in_specs=[pl.BlockSpec((1, 1, C_QIDX, TQ), lambda b, i: (b, i, 0, 0)),
                  pl.BlockSpec((1, 1, 16, TQ), lambda b, i: (b, i, 0, 0)),
                  pl.BlockSpec((1, seq, IDX_DIM), lambda b, i: (b, 0, 0)),
                  pl.BlockSpec((1, 1, C_Q, TQ), lambda b, i: (b, i, 0, 0)),
                  pl.BlockSpec((1, seq, HEAD_DIM), lambda b, i: (b, 0, 0)),
                  pl.BlockSpec((1, ng, HEAD_DIM, GK), lambda b, i: (b, 0, 0, 0)),
                  pl.BlockSpec((N_HEADS_C, BIAS_ROWS, TQ), lambda b, i: (bias_blk, 0, 0)),
                  pl.BlockSpec((TQ, TQ), lambda b, i: (0, 0))],
        out_specs=pl.BlockSpec((1, TQ, C_Q), lambda b, i: (b, i, 0)),
        out_shape=jax.ShapeDtypeStruct((batch, seq, C_Q), BF16),
        scratch_shapes=[pltpu.VMEM((seq, TQ), F32),
                        pltpu.VMEM((seq, TQ), jnp.int16),
                        pltpu.VMEM((seq, TQ), jnp.int16),
                        pltpu.VMEM((N_HEADS_C, 1, TQ), F32),
                        pltpu.VMEM((N_HEADS_C, 1, TQ), F32),
                        pltpu.VMEM((N_HEADS_C, HEAD_DIM, TQ), F32),
                        pltpu.VMEM((1, TQ), F32)],
        compiler_params=_cparams(2),
        name="dsa_attn",
    )(qiT, wT, ki, qcT, kc, vcT, bias, tri)


def _pad_heads(w, n_heads):
    w = w.reshape(D_MODEL, n_heads, HEAD_DIM)
    return jnp.pad(w, ((0, 0), (0, 0), (0, LANES - HEAD_DIM))).reshape(D_MODEL, n_heads * LANES)


def _split_w_in(w_in):
    cuts = np.cumsum([A_Q, A_KV, A_KV, B_QKV, B_QKV, B_QKV, C_Q, C_KV_LATENT, C_QIDX, IDX_DIM])
    qa, ka, va, qb, kb, vb, qc, ckv, qidx, kidx, widx = jnp.split(w_in, cuts, axis=1)
    wfm = jnp.concatenate([qa, va, qb, vb, qc, qidx, jnp.pad(widx, ((0, 0), (0, 16 - N_IDX_HEADS)))], axis=1)
    wtm = jnp.concatenate([_pad_heads(ka, N_KV_A), _pad_heads(kb, N_HEADS_B), ckv, _pad_heads(kidx, 1)], axis=1)
    assert wfm.shape[1] == _FM_ROWS and wtm.shape[1] == _TM_COLS
    return wfm.T.astype(BF16), wtm.astype(BF16)


def kernel(x, rel_bias_table, ffn1_norm, ffn1_w_gate, ffn1_w_up, ffn1_w_down, mix_norm, w_in, attn_sinks, kv_norm_c, w_kv_up_c, w_out, ffn2_norm, ffn2_w_gate, ffn2_w_up, ffn2_w_down, final_norm):
    batch, seq = x.shape[0], x.shape[1]
    depth = w_in.shape[0]
    assert seq % GK == 0 and x.shape[2] == D_MODEL
    nb = seq // TQ
    bias = _bias_tiles(rel_bias_table)
    gf = final_norm.reshape(1, D_MODEL)
    xf = x.reshape(batch * seq, D_MODEL)
    for l in range(depth):
        xf = _ffn(xf, ffn1_norm[l].reshape(1, D_MODEL), ffn1_w_gate[l].astype(BF16), ffn1_w_up[l].astype(BF16),
                  ffn1_w_down[l].astype(BF16), gf, False)
        wfm, wtm = _split_w_in(w_in[l])
        wkv = w_kv_up_c[l].astype(BF16)
        (qaT, vaT, qbT, vbT, qcT, qiT, wT, ka, kb, kmean, kc, vcT, ki) = _proj(
            xf, mix_norm[l].reshape(1, D_MODEL), wfm, wtm, kv_norm_c[l].reshape(1, C_KV_LATENT),
            wkv, wkv[:, HEAD_DIM:].T, batch, seq)
        kmean = kmean.reshape(batch, nb, N_HEADS_B, LANES).transpose(0, 2, 1, 3)
        out_a = _swa(attn_sinks[l], qaT, ka, vaT, bias, batch, seq)
        out_b = _moba(qbT, kb, vbT, kmean, bias, batch, seq)
        out_c = _dsa(qiT, wT, ki, qcT, kc, vcT, bias, batch, seq)
        wo = w_out[l].astype(BF16)
        mix = (out_a.reshape(batch * seq, A_Q), out_b.reshape(batch * seq, B_QKV), out_c.reshape(batch * seq, C_Q),
               wo[:A_Q], wo[A_Q:A_Q + B_QKV], wo[A_Q + B_QKV:])
        xf = _ffn(xf, ffn2_norm[l].reshape(1, D_MODEL), ffn2_w_gate[l].astype(BF16), ffn2_w_up[l].astype(BF16),
                  ffn2_w_down[l].astype(BF16), gf, l == depth - 1, mix)
    return xf.reshape(batch, seq, D_MODEL)
```

```python
import functools
import math

import jax
import jax.numpy as jnp
import numpy as np
from jax import lax
from jax.experimental import pallas as pl
from jax.experimental.pallas import tpu as pltpu

D_MODEL = 1024
HEAD_DIM = 64
N_HEADS = 16
N_HEADS_A = 8
N_KV_A = 2
WINDOW = 128
N_HEADS_B = 4
MOBA_BLOCK = 256
MOBA_TOPK = 3
N_HEADS_C = 4
C_KV_LATENT = 128
N_IDX_HEADS = 4
IDX_DIM = 64
DSA_TOPK = 256
D_FF = 2816
N_BUCKETS = 32
MAX_DISTANCE = 128
RMS_EPS = 1e-6

A_Q = N_HEADS_A * HEAD_DIM
A_KV = N_KV_A * HEAD_DIM
B_QKV = N_HEADS_B * HEAD_DIM
C_Q = N_HEADS_C * HEAD_DIM
C_QIDX = N_IDX_HEADS * IDX_DIM

TQ = MOBA_BLOCK
GB = 4
GK = GB * TQ
LANES = 128
FFN_TM = 1024
FFN_TF = 256
PROJ_TM = GK
VMEM_LIMIT = 56 * 1024 * 1024

F32 = jnp.float32
BF16 = jnp.bfloat16
NEG_INF = float("-inf")
LOG2E = math.log2(math.e)
QK_SCALE = HEAD_DIM ** -0.5 * LOG2E

BIAS_ROWS = 4 * TQ

_NT = (((1,), (1,)), ((), ()))


def _cparams(n_axes):
    return pltpu.CompilerParams(dimension_semantics=("arbitrary",) * n_axes,
                                vmem_limit_bytes=VMEM_LIMIT)


def _dot(a, b):
    return jnp.dot(a, b, preferred_element_type=F32)


def _dot_nt(a, b):
    return lax.dot_general(a, b, _NT, preferred_element_type=F32)


def _rms(x, g):
    return x * lax.rsqrt(jnp.mean(x * x, axis=-1, keepdims=True) + RMS_EPS) * g


def _t5_bucket_np(dist):
    n = np.maximum(dist, 0)
    max_exact = N_BUCKETS // 2
    nf = np.maximum(n, 1).astype(np.float32)
    large = max_exact + (np.log(nf / np.float32(max_exact)) / np.float32(math.log(MAX_DISTANCE / max_exact))
                         * np.float32(N_BUCKETS - max_exact)).astype(np.int32)
    large = np.minimum(large, N_BUCKETS - 1)
    return np.where(n < max_exact, n, large).astype(np.int32)


_IDX_MASKED = -1
_IDX_ZERO = -2


def _bias_index_tiles():
    col = np.arange(TQ)[None, :]
    row = np.arange(TQ)[:, None]
    d_own = col - row
    d_prev = TQ + col - row
    d_prev_w = WINDOW + col - np.arange(WINDOW)[:, None]
    a = np.full((BIAS_ROWS, TQ), _IDX_MASKED, np.int32)
    a[:TQ] = np.where((d_own >= 0) & (d_own < WINDOW), _t5_bucket_np(d_own), _IDX_MASKED)
    a[TQ:TQ + WINDOW] = np.where((d_prev_w >= 0) & (d_prev_w < WINDOW), _t5_bucket_np(d_prev_w), _IDX_MASKED)
    b = np.full((BIAS_ROWS, TQ), _IDX_MASKED, np.int32)
    b[:TQ] = _IDX_ZERO
    b[TQ:2 * TQ] = _t5_bucket_np(d_prev)
    b[2 * TQ:3 * TQ] = np.where(d_own >= 0, _t5_bucket_np(d_own), _IDX_MASKED)
    assert (_t5_bucket_np(np.arange(2 * TQ, 64 * TQ)) == N_BUCKETS - 1).all()
    return np.stack([a, b])


def _bias_kernel(tab_ref, idx_ref, o_ref):
    h = pl.program_id(0)
    idx = idx_ref[0]
    shift = jnp.where(h >= N_HEADS_A, tab_ref[N_BUCKETS - 1, h], 0.0)
    out = jnp.where(idx == _IDX_ZERO, 0.0, NEG_INF)
    for b in range(N_BUCKETS):
        out = jnp.where(idx == b, (tab_ref[b, h] - shift) * LOG2E, out)
    o_ref[0] = out


def _bias_tiles(rel_bias_table):
    idx = jnp.asarray(_bias_index_tiles())
    return pl.pallas_call(
        _bias_kernel,
        grid=(N_HEADS,),
        in_specs=[pl.BlockSpec(memory_space=pltpu.SMEM),
                  pl.BlockSpec((1, BIAS_ROWS, TQ), lambda h: (h // N_HEADS_A, 0, 0))],
        out_specs=pl.BlockSpec((1, BIAS_ROWS, TQ), lambda h: (h, 0, 0)),
        out_shape=jax.ShapeDtypeStruct((N_HEADS, BIAS_ROWS, TQ), F32),
        compiler_params=_cparams(1),
        name="t5_bias_tiles",
    )(rel_bias_table, idx)


def _swiglu_half_step(x, g_ref, wg_ref, wu_ref, wd_ref, gf_ref, final_norm):
    h = _rms(x, g_ref[...]).astype(BF16)
    down = None
    for c in range(D_FF // FFN_TF):
        cols = slice(c * FFN_TF, (c + 1) * FFN_TF)
        gate = _dot(h, wg_ref[:, cols])
        up = _dot(h, wu_ref[:, cols])
        act = (gate * jax.nn.sigmoid(gate) * up).astype(BF16)
        d = _dot(act, wd_ref[cols, :])
        down = d if down is None else down + d
    y = x + 0.5 * down
    return _rms(y, gf_ref[...]) if final_norm else y


def _ffn_kernel(x_ref, g_ref, wg_ref, wu_ref, wd_ref, gf_ref, o_ref, *, final_norm):
    o_ref[...] = _swiglu_half_step(x_ref[...], g_ref, wg_ref, wu_ref, wd_ref, gf_ref, final_norm)


def _mix_out_ffn_kernel(x_ref, a_ref, b_ref, c_ref, wa_ref, wb_ref, wc_ref,
                        g_ref, wg_ref, wu_ref, wd_ref, gf_ref, o_ref, *, final_norm):
    x = (x_ref[...] + _dot(a_ref[...], wa_ref[...]) + _dot(b_ref[...], wb_ref[...])
         + _dot(c_ref[...], wc_ref[...]))
    o_ref[...] = _swiglu_half_step(x, g_ref, wg_ref, wu_ref, wd_ref, gf_ref, final_norm)


def _resident(a):
    return pl.BlockSpec(a.shape, lambda i: (0, 0), pipeline_mode=pl.Buffered(1))


def _ffn(x, g, wg, wu, wd, gf, final_norm, mix=None):
    n = x.shape[0]
    tm = min(FFN_TM, n)
    assert n % tm == 0 and D_FF % FFN_TF == 0

    def rows(cols):
        return pl.BlockSpec((tm, cols), lambda i: (i, 0))

    weights = (g, wg, wu, wd, gf)
    if mix is None:
        body, operands = _ffn_kernel, (x,) + weights
        in_specs = [rows(D_MODEL)] + [_resident(w) for w in weights]
    else:
        body, operands = _mix_out_ffn_kernel, (x,) + tuple(mix) + weights
        in_specs = ([rows(D_MODEL)] + [rows(t.shape[1]) for t in mix[:3]] + [_resident(w) for w in mix[3:]]
                    + [_resident(w) for w in weights])
    return pl.pallas_call(
        functools.partial(body, final_norm=final_norm),
        grid=(n // tm,),
        in_specs=in_specs,
        out_specs=rows(D_MODEL),
        out_shape=jax.ShapeDtypeStruct((n, D_MODEL), F32),
        compiler_params=_cparams(1),
        name="swiglu_ffn" if mix is None else "mix_out_ffn",
    )(*operands)


_FM_QA, _FM_VA, _FM_QB, _FM_VB, _FM_QC, _FM_QI, _FM_W = 0, 512, 640, 896, 1152, 1408, 1664
_FM_ROWS = 1680
_TM_KA, _TM_KB, _TM_CKV, _TM_KI = 0, 256, 768, 896
_TM_COLS = 1024


def _proj_kernel(x_ref, g_ref, wfm_ref, wtm_ref, gc_ref, wkv_ref, wvT_ref,
                 qa_ref, va_ref, qb_ref, vb_ref, qc_ref, qi_ref, w_ref,
                 ka_ref, kb_ref, kmean_ref, kc_ref, vc_ref, ki_ref):
    tm = x_ref.shape[0]
    nsub = tm // TQ
    h = _rms(x_ref[...], g_ref[...]).astype(BF16)

    def fm(lo, hi, scale=None):
        y = _dot_nt(wfm_ref[lo:hi, :], h)
        return y if scale is None else y * scale

    def put_fm(ref, y):
        for r in range(nsub):
            ref[0, r] = y[:, r * TQ:(r + 1) * TQ].astype(ref.dtype)

    put_fm(qa_ref, fm(_FM_QA, _FM_VA, QK_SCALE))
    put_fm(va_ref, fm(_FM_VA, _FM_QB))
    put_fm(qb_ref, fm(_FM_QB, _FM_VB, QK_SCALE))
    vb_ref[0, 0] = fm(_FM_VB, _FM_QC).astype(BF16)
    put_fm(qc_ref, fm(_FM_QC, _FM_QI, QK_SCALE))
    put_fm(qi_ref, fm(_FM_QI, _FM_W))
    put_fm(w_ref, fm(_FM_W, _FM_ROWS, (N_IDX_HEADS ** -0.5) * (IDX_DIM ** -0.5)))

    ptm = _dot(h, wtm_ref[...])
    for kv in range(N_KV_A):
        lo = _TM_KA + kv * LANES
        ka_ref[0, kv] = ptm[:, lo:lo + HEAD_DIM].astype(BF16)
    for hb in range(N_HEADS_B):
        lo = _TM_KB + hb * LANES
        kb_ref[0, hb] = ptm[:, lo:lo + HEAD_DIM].astype(BF16)
    for r in range(nsub):
        kmean_ref[0, r] = jnp.mean(ptm[r * TQ:(r + 1) * TQ, _TM_KB:_TM_CKV], axis=0, keepdims=True)
    ki_ref[0] = ptm[:, _TM_KI:_TM_KI + IDX_DIM].astype(BF16)

    ckv = _rms(ptm[:, _TM_CKV:_TM_KI], gc_ref[...]).astype(BF16)
    kc_ref[0] = _dot(ckv, wkv_ref[...])[:, :HEAD_DIM].astype(BF16)
    vc_ref[0, 0] = _dot_nt(wvT_ref[...], ckv).astype(BF16)


def _proj(x, g, wfm, wtm, gc, wkv, wvT, batch, seq):
    tm = PROJ_TM
    assert seq % tm == 0 and tm == GK
    tpb = seq // tm
    nsub = tm // TQ
    nb = seq // TQ

    def fm_spec(rows):
        return pl.BlockSpec((1, nsub, rows, TQ), lambda i: (i // tpb, i % tpb, 0, 0))

    def fm_shape(rows, dtype=BF16):
        return jax.ShapeDtypeStruct((batch, nb, rows, TQ), dtype)

    def grp_spec(rows):
        return pl.BlockSpec((1, 1, rows, GK), lambda i: (i // tpb, i % tpb, 0, 0))

    def grp_shape(rows):
        return jax.ShapeDtypeStruct((batch, tpb, rows, GK), BF16)

    def full(a):
        return pl.BlockSpec(a.shape, lambda i: (0,) * a.ndim)

    out_specs = [fm_spec(A_Q), fm_spec(A_KV), fm_spec(B_QKV), grp_spec(B_QKV), fm_spec(C_Q), fm_spec(C_QIDX),
                 fm_spec(16),
                 pl.BlockSpec((1, N_KV_A, tm, HEAD_DIM), lambda i: (i // tpb, 0, i % tpb, 0)),
                 pl.BlockSpec((1, N_HEADS_B, tm, HEAD_DIM), lambda i: (i // tpb, 0, i % tpb, 0)),
                 pl.BlockSpec((1, nsub, 1, N_HEADS_B * LANES), lambda i: (i // tpb, i % tpb, 0, 0)),
                 pl.BlockSpec((1, tm, HEAD_DIM), lambda i: (i // tpb, i % tpb, 0)),
                 grp_spec(HEAD_DIM),
                 pl.BlockSpec((1, tm, IDX_DIM), lambda i: (i // tpb, i % tpb, 0))]
    out_shape = [fm_shape(A_Q), fm_shape(A_KV), fm_shape(B_QKV), grp_shape(B_QKV), fm_shape(C_Q), fm_shape(C_QIDX),
                 fm_shape(16, F32),
                 jax.ShapeDtypeStruct((batch, N_KV_A, seq, HEAD_DIM), BF16),
                 jax.ShapeDtypeStruct((batch, N_HEADS_B, seq, HEAD_DIM), BF16),
                 jax.ShapeDtypeStruct((batch, nb, 1, N_HEADS_B * LANES), F32),
                 jax.ShapeDtypeStruct((batch, seq, HEAD_DIM), BF16),
                 grp_shape(HEAD_DIM),
                 jax.ShapeDtypeStruct((batch, seq, IDX_DIM), BF16)]
    return pl.pallas_call(
        _proj_kernel,
        grid=(batch * tpb,),
        in_specs=[pl.BlockSpec((tm, D_MODEL), lambda i: (i, 0)),
                  full(g), full(wfm), full(wtm), full(gc), full(wkv), full(wvT)],
        out_specs=out_specs,
        out_shape=out_shape,
        compiler_params=_cparams(1),
        name="mix_in_proj",
    )(x, g, wfm, wtm, gc, wkv, wvT)


def _online_update(parts, v_parts, m_ref, l_ref, acc_ref, hh):
    m_prev = m_ref[hh]
    m_new = jnp.maximum(m_prev, jnp.max(functools.reduce(jnp.maximum, parts), axis=0, keepdims=True))
    m_safe = jnp.where(m_new == NEG_INF, 0.0, m_new)
    alpha = jnp.exp2(m_prev - m_safe)
    ps = [jnp.exp2(s - m_safe) for s in parts]
    pv = functools.reduce(lambda a, b: a + b, [_dot(v, p.astype(BF16)) for v, p in zip(v_parts, ps)])
    l_ref[hh] = alpha * l_ref[hh] + jnp.sum(functools.reduce(lambda a, b: a + b, ps), axis=0, keepdims=True)
    acc_ref[hh] = alpha * acc_ref[hh] + pv
    m_ref[hh] = m_new


def _init_stats(m_ref, l_ref, acc_ref):
    m_ref[...] = jnp.full(m_ref.shape, NEG_INF, F32)
    l_ref[...] = jnp.zeros(l_ref.shape, F32)
    acc_ref[...] = jnp.zeros(acc_ref.shape, F32)


LAZY_SUM_MIN, LAZY_SUM_MAX = 2.0 ** -64, 2.0 ** 64


def _lazy_update(parts, v_parts, l_ref, acc_ref, hh):
    ps = [jnp.exp2(s) for s in parts]
    pv = functools.reduce(lambda a, b: a + b, [_dot(v, p.astype(BF16)) for v, p in zip(v_parts, ps)])
    l_ref[hh] += jnp.sum(functools.reduce(lambda a, b: a + b, ps), axis=0, keepdims=True)
    acc_ref[hh] += pv


def _lazy_sum_ok(total):
    return jnp.where((total > LAZY_SUM_MIN) & (total < LAZY_SUM_MAX), 1.0, 0.0)


def _lazy_failed(l_ref, n_heads):
    ok = functools.reduce(jnp.minimum, [_lazy_sum_ok(l_ref[hh]) for hh in range(n_heads)])
    return jnp.min(ok) < 0.5


def _finish(l_ref, acc_ref, n_heads):
    oT = jnp.concatenate([acc_ref[hh] * (1.0 / l_ref[hh]) for hh in range(n_heads)], axis=0)
    return oT.T


def _bias_tile(bias_ref, hh, j, i):
    kind = jnp.clip(j - i + 2, 0, 3)
    return bias_ref[hh, pl.ds(pl.multiple_of(kind * TQ, TQ), TQ), :]


def _group_bounds(i):
    n_groups = i // GB + 1
    n_far = jnp.maximum((i - 1) // GB, 0)
    return n_far, n_groups


def _for_last_group(i, fn):
    for nblk in range(1, GB + 1):
        @pl.when(i % GB == nblk - 1)
        def _():
            fn(nblk)


_SWA_BIAS_ROWS = TQ + WINDOW


def _swa_kernel(sink_ref, q_ref, k_ref, v_ref, bias_ref, o_ref):
    i = pl.program_id(1)
    g_heads = N_HEADS_A // N_KV_A
    start = pl.multiple_of(i * TQ, TQ)
    prev_start = pl.multiple_of(jnp.maximum(i * TQ - WINDOW, 0), WINDOW)
    prev_off = jnp.where(i > 0, 0.0, NEG_INF)
    k_main = [k_ref[0, kv, pl.ds(start, TQ), :] for kv in range(N_KV_A)]
    k_prev = [k_ref[0, kv, pl.ds(prev_start, WINDOW), :] for kv in range(N_KV_A)]
    v_main = [v_ref[0, i, kv * HEAD_DIM:(kv + 1) * HEAD_DIM, :] for kv in range(N_KV_A)]
    v_prev = [v_ref[0, jnp.maximum(i - 1, 0), kv * HEAD_DIM:(kv + 1) * HEAD_DIM, TQ - WINDOW:] for kv in range(N_KV_A)]

    def scores(h):
        kv = h // g_heads
        q = q_ref[0, 0, h * HEAD_DIM:(h + 1) * HEAD_DIM, :]
        s_main = _dot(k_main[kv], q) + bias_ref[h, 0:TQ, :]
        s_prev = _dot(k_prev[kv], q) + bias_ref[h, TQ:TQ + WINDOW, :] + prev_off
        return s_main, s_prev, sink_ref[h] * LOG2E

    def head_out(h, s_main, s_prev, sink, m):
        kv = h // g_heads
        p_main = jnp.exp2(s_main - m)
        p_prev = jnp.exp2(s_prev - m)
        denom = (jnp.sum(p_main, axis=0, keepdims=True) + jnp.sum(p_prev, axis=0, keepdims=True)
                 + jnp.exp2(sink - m))
        oT = _dot(v_main[kv], p_main.astype(BF16)) + _dot(v_prev[kv], p_prev.astype(BF16))
        return oT * (1.0 / denom), denom

    all_scores = [scores(h) for h in range(N_HEADS_A)]
    outs, ok = [], None
    for h in range(N_HEADS_A):
        s_main, s_prev, sink = all_scores[h]
        out, denom = head_out(h, s_main, s_prev, sink, 0.0)
        outs.append(out)
        ok = _lazy_sum_ok(denom) if ok is None else jnp.minimum(ok, _lazy_sum_ok(denom))
    o_ref[0] = jnp.concatenate(outs, axis=0).T.astype(o_ref.dtype)

    @pl.when(jnp.min(ok) < 0.5)
    def _():
        outs = []
        for h in range(N_HEADS_A):
            s_main, s_prev, sink = scores(h)
            m = jnp.maximum(jnp.maximum(jnp.max(s_main, axis=0, keepdims=True),
                                        jnp.max(s_prev, axis=0, keepdims=True)), sink)
            outs.append(head_out(h, s_main, s_prev, sink, m)[0])
        o_ref[0] = jnp.concatenate(outs, axis=0).T.astype(o_ref.dtype)


def _swa(sinks, qaT, ka, vaT, bias, batch, seq):
    nb = seq // TQ
    return pl.pallas_call(
        _swa_kernel,
        grid=(batch, nb),
        in_specs=[pl.BlockSpec(memory_space=pltpu.SMEM),
                  pl.BlockSpec((1, 1, A_Q, TQ), lambda b, i: (b, i, 0, 0)),
                  pl.BlockSpec((1, N_KV_A, seq, HEAD_DIM), lambda b, i: (b, 0, 0, 0)),
                  pl.BlockSpec((1, nb, A_KV, TQ), lambda b, i: (b, 0, 0, 0)),
                  pl.BlockSpec((N_HEADS_A, _SWA_BIAS_ROWS, TQ), lambda b, i: (0, 0, 0))],
        out_specs=pl.BlockSpec((1, TQ, A_Q), lambda b, i: (b, i, 0)),
        out_shape=jax.ShapeDtypeStruct((batch, seq, A_Q), BF16),
        compiler_params=_cparams(2),
        name="swa_attn",
    )(sinks, qaT, ka, vaT, bias)


_MOBA_HPS = 4


def _moba_kernel(q_ref, k_ref, v_ref, kmean_ref, bias_ref, o_ref,
                 sel_ref, m_ref, l_ref, acc_ref, *, n_sel):
    i = pl.program_id(2)
    nb = kmean_ref.shape[2]
    row = lax.broadcasted_iota(jnp.int32, (nb, TQ), 0)
    qs = [q_ref[0, 0, hh * HEAD_DIM:(hh + 1) * HEAD_DIM, :] for hh in range(_MOBA_HPS)]

    for hh in range(_MOBA_HPS):
        gate = _dot(kmean_ref[0, hh][:, :HEAD_DIM], qs[hh].astype(F32))
        gate = jnp.where(row < i, gate, NEG_INF)
        sel = row == i
        for _ in range(n_sel):
            best = jnp.max(gate, axis=0, keepdims=True)
            first = jnp.min(jnp.where(gate == best, row, nb), axis=0, keepdims=True)
            pick = (row == first) & (best > NEG_INF)
            sel = sel | pick
            gate = jnp.where(pick, NEG_INF, gate)
        sel_ref[hh] = jnp.where(sel, 0.0, NEG_INF)

    def group(g, mixed, lazy, nblk=GB):
        kbase = pl.multiple_of(g * GK, GK)
        s_all = [_dot(k_ref[0, hh, pl.ds(kbase, nblk * TQ), :], qs[hh]) for hh in range(_MOBA_HPS)]
        for hh in range(_MOBA_HPS):
            parts = []
            for b in range(nblk):
                j = g * GB + b
                sb = s_all[hh][b * TQ:(b + 1) * TQ] + sel_ref[hh, pl.ds(j, 1), :]
                if mixed:
                    sb = sb + _bias_tile(bias_ref, hh, j, i)
                parts.append(sb)
            v_parts = [v_ref[0, g, hh * HEAD_DIM:(hh + 1) * HEAD_DIM, b * TQ:(b + 1) * TQ] for b in range(nblk)]
            if lazy:
                _lazy_update(parts, v_parts, l_ref, acc_ref, hh)
            else:
                _online_update(parts, v_parts, m_ref, l_ref, acc_ref, hh)

    n_far, n_groups = _group_bounds(i)

    def all_groups(lazy):
        def far_body(g, carry):
            group(g, False, lazy)
            return carry

        def mixed_body(g, carry):
            group(g, True, lazy)
            return carry

        lax.fori_loop(0, n_far, far_body, 0)
        lax.fori_loop(n_far, n_groups - 1, mixed_body, 0)
        _for_last_group(i, lambda nblk: group(n_groups - 1, True, lazy, nblk))

    _init_stats(m_ref, l_ref, acc_ref)
    all_groups(True)

    @pl.when(_lazy_failed(l_ref, _MOBA_HPS))
    def _():
        _init_stats(m_ref, l_ref, acc_ref)
        all_groups(False)

    o_ref[0] = _finish(l_ref, acc_ref, _MOBA_HPS).astype(o_ref.dtype)


def _moba(qbT, kb, vbT, kmean, bias, batch, seq):
    nb = seq // TQ
    ng = seq // GK
    hps = _MOBA_HPS
    n_sel = min(MOBA_TOPK, nb - 1)
    bias_blk0 = N_HEADS_A // hps
    return pl.pallas_call(
        functools.partial(_moba_kernel, n_sel=n_sel),
        grid=(batch, N_HEADS_B // hps, nb),
        in_specs=[pl.BlockSpec((1, 1, hps * HEAD_DIM, TQ), lambda b, hp, i: (b, i, hp, 0)),
                  pl.BlockSpec((1, hps, seq, HEAD_DIM), lambda b, hp, i: (b, hp, 0, 0)),
                  pl.BlockSpec((1, ng, hps * HEAD_DIM, GK), lambda b, hp, i: (b, 0, hp, 0)),
                  pl.BlockSpec((1, hps, nb, LANES), lambda b, hp, i: (b, hp, 0, 0)),
                  pl.BlockSpec((hps, BIAS_ROWS, TQ), lambda b, hp, i: (bias_blk0 + hp, 0, 0))],
        out_specs=pl.BlockSpec((1, TQ, hps * HEAD_DIM), lambda b, hp, i: (b, i, hp)),
        out_shape=jax.ShapeDtypeStruct((batch, seq, B_QKV), BF16),
        scratch_shapes=[pltpu.VMEM((hps, nb, TQ), F32),
                        pltpu.VMEM((hps, 1, TQ), F32),
                        pltpu.VMEM((hps, 1, TQ), F32),
                        pltpu.VMEM((hps, HEAD_DIM, TQ), F32)],
        compiler_params=_cparams(3),
        name="moba_attn",
    )(qbT, kb, vbT, kmean, bias)


_PACK16 = 16
_I16_MIN, _I16_MAX = -32768, 32767
_N_COUNT_ACC = 4


def _f32_to_key(x):
    bits = lax.bitcast_convert_type(x, jnp.int32)
    key = bits ^ ((bits >> 31) & jnp.int32(0x7FFFFFFF))
    return jnp.where(key == -1, 0, key)


def _key_to_f32(key):
    bits = key ^ ((key >> 31) & jnp.int32(0x7FFFFFFF))
    return lax.bitcast_convert_type(bits, F32)


_MIN_NORMAL_KEY = 0x00800000


def _next_key(key):
    nxt = key + 1
    nxt = jnp.where((nxt >= -_MIN_NORMAL_KEY) & (nxt < 0), 0, nxt)
    return jnp.where((nxt > 0) & (nxt < _MIN_NORMAL_KEY), _MIN_NORMAL_KEY, nxt)


def _dsa_kernel(qi_ref, w_ref, ki_ref, q_ref, k_ref, v_ref, bias_ref, tri_ref, o_ref,
                score_ref, hi_ref, lo_ref, m_ref, l_ref, acc_ref, ties_ref, *, n_keep):
    i = pl.program_id(1)
    n_far, n_groups = _group_bounds(i)

    def rows(g, b=0):
        return pl.ds(pl.multiple_of(g * GK + b * TQ, TQ), TQ)

    w = w_ref[0, 0]
    qis = [qi_ref[0, 0, hi * IDX_DIM:(hi + 1) * IDX_DIM, :] for hi in range(N_IDX_HEADS)]
    krow = lax.broadcasted_iota(jnp.int32, (TQ, TQ), 0)
    qcol = lax.broadcasted_iota(jnp.int32, (TQ, TQ), 1)

    last = n_groups - 1

    def score_group(g, mixed, nblk=GB):
        for b in range(nblk):
            ki = ki_ref[0, rows(g, b), :]
            ds = [_dot(ki, qi) for qi in qis]
            sc = functools.reduce(lambda x, y: x + y,
                                  [jnp.maximum(d, 0.0) * w[hi:hi + 1, :] for hi, d in enumerate(ds)])
            if mixed:
                sc = jnp.where(krow + (g * GB + b) * TQ <= qcol + i * TQ, sc, NEG_INF)
            score_ref[rows(g, b), :] = sc
            key = _f32_to_key(sc)
            hi_ref[rows(g, b), :] = (key >> 16).astype(jnp.int16)
            lo_ref[rows(g, b), :] = ((key & 0xFFFF) + _I16_MIN).astype(jnp.int16)
        for b in range(nblk, GB):
            hi_ref[rows(g, b), :] = jnp.full((TQ, TQ), _I16_MIN, jnp.int16)
            lo_ref[rows(g, b), :] = jnp.full((TQ, TQ), _I16_MIN, jnp.int16)

    def score_far(g, carry):
        score_group(g, False)
        return carry

    def score_mixed(g, carry):
        score_group(g, True)
        return carry

    lax.fori_loop(0, n_far, score_far, 0)
    lax.fori_loop(n_far, last, score_mixed, 0)
    _for_last_group(i, lambda nblk: score_group(last, True, nblk))

    def group16(ref, g):
        return ref[pl.ds(pl.multiple_of(g * GK, GK), GK), :]

    def count_ge16(ref, cand):
        cand16 = jnp.broadcast_to(cand.astype(jnp.int16), (_PACK16, TQ))

        def body(g, accs):
            accs = list(accs)
            grp = group16(ref, g)
            for r in range(GK // _PACK16):
                a = r % _N_COUNT_ACC
                hit = grp[r * _PACK16:(r + 1) * _PACK16] >= cand16
                accs[a] = accs[a] + jnp.where(hit, jnp.int16(1), jnp.int16(0))
            return tuple(accs)

        zero = jnp.zeros((_PACK16, TQ), jnp.int16)
        accs = lax.fori_loop(0, n_groups, body, (zero,) * _N_COUNT_ACC)
        total = functools.reduce(lambda x, y: x + y, accs).astype(jnp.int32)
        return jnp.sum(total, axis=0, keepdims=True)

    def kth_largest16(ref, k):
        def bit_body(b, carry):
            c, n_gt = carry
            c_try = c + lax.shift_left(jnp.int32(1), 15 - b)
            n = count_ge16(ref, c_try)
            ok = n >= k
            return jnp.where(ok, c_try, c), jnp.where(ok, n_gt, n)
        init = (jnp.full((1, TQ), _I16_MIN, jnp.int32), jnp.zeros((1, TQ), jnp.int32))
        return lax.fori_loop(0, 16, bit_body, init)

    c_hi, n_gt_hi = kth_largest16(hi_ref, n_keep)
    c_hi16 = c_hi.astype(jnp.int16)

    def mask_lo(g, carry):
        sl = pl.ds(pl.multiple_of(g * GK, GK), GK)
        lo_ref[sl, :] = jnp.where(hi_ref[sl, :] == c_hi16, lo_ref[sl, :], jnp.int16(_I16_MIN))
        return carry

    lax.fori_loop(0, n_groups, mask_lo, 0)
    c_lo, n_gt_lo = kth_largest16(lo_ref, n_keep - n_gt_hi)
    key_thr = lax.shift_left(c_hi, 16) | (c_lo - _I16_MIN)
    thr = _key_to_f32(key_thr)
    thr_next = _key_to_f32(_next_key(key_thr))
    tie_budget = (n_keep - (n_gt_hi + n_gt_lo)).astype(F32)

    qs = [q_ref[0, 0, hh * HEAD_DIM:(hh + 1) * HEAD_DIM, :] for hh in range(N_HEADS_C)]

    def selection_masks(g, nblk):
        blks = [score_ref[rows(g, b), :] for b in range(nblk)]
        ties = [jnp.where(blk == thr, 1.0, 0.0) for blk in blks]
        prefix = [_dot(tri_ref[...], tie.astype(BF16)) for tie in ties]
        masks = []
        seen = ties_ref[...]
        for blk, tie, pre in zip(blks, ties, prefix):
            bar = jnp.where(pre < tie_budget - seen, thr, thr_next)
            masks.append(jnp.where(blk >= bar, 0.0, NEG_INF))
            seen = seen + pre[TQ - 1:TQ, :] + tie[TQ - 1:TQ, :]
        ties_ref[...] = seen
        return masks

    def attend_group(g, mixed, lazy, nblk=GB):
        masks = selection_masks(g, nblk)
        kc = k_ref[0, pl.ds(pl.multiple_of(g * GK, GK), nblk * TQ), :]
        s_all = [_dot(kc, q) for q in qs]
        v_parts = [v_ref[0, g, :, b * TQ:(b + 1) * TQ] for b in range(nblk)]
        for hh in range(N_HEADS_C):
            parts = []
            for b in range(nblk):
                sb = s_all[hh][b * TQ:(b + 1) * TQ] + masks[b]
                if mixed:
                    sb = sb + _bias_tile(bias_ref, hh, g * GB + b, i)
                parts.append(sb)
            if lazy:
                _lazy_update(parts, v_parts, l_ref, acc_ref, hh)
            else:
                _online_update(parts, v_parts, m_ref, l_ref, acc_ref, hh)

    def attend_all_groups(lazy):
        def far_body(g, carry):
            attend_group(g, False, lazy)
            return carry

        def mixed_body(g, carry):
            attend_group(g, True, lazy)
            return carry

        ties_ref[...] = jnp.zeros(ties_ref.shape, F32)
        lax.fori_loop(0, n_far, far_body, 0)
        lax.fori_loop(n_far, last, mixed_body, 0)
        _for_last_group(i, lambda nblk: attend_group(last, True, lazy, nblk))

    _init_stats(m_ref, l_ref, acc_ref)
    attend_all_groups(True)

    @pl.when(_lazy_failed(l_ref, N_HEADS_C))
    def _():
        _init_stats(m_ref, l_ref, acc_ref)
        attend_all_groups(False)

    o_ref[0] = _finish(l_ref, acc_ref, N_HEADS_C).astype(o_ref.dtype)


def _dsa(qiT, wT, ki, qcT, kc, vcT, bias, batch, seq):
    nb = seq // TQ
    ng = seq // GK
    n_keep = min(DSA_TOPK, seq // 4)
    assert seq // (_PACK16 * _N_COUNT_ACC) <= _I16_MAX
    tri = jnp.asarray(np.tril(np.ones((TQ, TQ), np.float32), -1), BF16)
    bias_blk = (N_HEADS_A + N_HEADS_B) // N_HEADS_C
    return pl.pallas_call(
        functools.partial(_dsa_kernel, n_keep=n_keep),
        grid=(batch, nb),
        in_specs=[pl.BlockSpec((1, 1, C_QIDX, TQ), lambda b, i: (b, i, 0, 0)),
                  pl.BlockSpec((1, 1, 16, TQ), lambda b, i: (b, i, 0, 0)),
                  pl.BlockSpec((1, seq, IDX_DIM), lambda b, i: (b, 0, 0)),
                  pl.BlockSpec((1, 1, C_Q, TQ), lambda b, i: (b, i, 0, 0)),
                  pl.BlockSpec((1, seq, HEAD_DIM), lambda b, i: (b, 0, 0)),
                  pl.BlockSpec((1, ng, HEAD_DIM, GK), lambda b, i: (b, 0, 0, 0)),
                  pl.BlockSpec((N_HEADS_C, BIAS_ROWS, TQ), lambda b, i: (bias_blk, 0, 0)),
                  pl.BlockSpec((TQ, TQ), lambda b, i: (0, 0))],
        out_specs=pl.BlockSpec((1, TQ, C_Q), lambda b, i: (b, i, 0)),
        out_shape=jax.ShapeDtypeStruct((batch, seq, C_Q), BF16),
        scratch_shapes=[pltpu.VMEM((seq, TQ), F32),
                        pltpu.VMEM((seq, TQ), jnp.int16),
                        pltpu.VMEM((seq, TQ), jnp.int16),
                        pltpu.VMEM((N_HEADS_C, 1, TQ), F32),
                        pltpu.VMEM((N_HEADS_C, 1, TQ), F32),
                        pltpu.VMEM((N_HEADS_C, HEAD_DIM, TQ), F32),
                        pltpu.VMEM((1, TQ), F32)],
        compiler_params=_cparams(2),
        name="dsa_attn",
    )(qiT, wT, ki, qcT, kc, vcT, bias, tri)


def _pad_heads(w, n_heads):
    w = w.reshape(D_MODEL, n_heads, HEAD_DIM)
    return jnp.pad(w, ((0, 0), (0, 0), (0, LANES - HEAD_DIM))).reshape(D_MODEL, n_heads * LANES)


def _split_w_in(w_in):
    cuts = np.cumsum([A_Q, A_KV, A_KV, B_QKV, B_QKV, B_QKV, C_Q, C_KV_LATENT, C_QIDX, IDX_DIM])
    qa, ka, va, qb, kb, vb, qc, ckv, qidx, kidx, widx = jnp.split(w_in, cuts, axis=1)
    wfm = jnp.concatenate([qa, va, qb, vb, qc, qidx, jnp.pad(widx, ((0, 0), (0, 16 - N_IDX_HEADS)))], axis=1)
    wtm = jnp.concatenate([_pad_heads(ka, N_KV_A), _pad_heads(kb, N_HEADS_B), ckv, _pad_heads(kidx, 1)], axis=1)
    assert wfm.shape[1] == _FM_ROWS and wtm.shape[1] == _TM_COLS
    return wfm.T.astype(BF16), wtm.astype(BF16)


def kernel(x, rel_bias_table, ffn1_norm, ffn1_w_gate, ffn1_w_up, ffn1_w_down, mix_norm, w_in, attn_sinks, kv_norm_c, w_kv_up_c, w_out, ffn2_norm, ffn2_w_gate, ffn2_w_up, ffn2_w_down, final_norm):
    batch, seq = x.shape[0], x.shape[1]
    depth = w_in.shape[0]
    assert seq % GK == 0 and x.shape[2] == D_MODEL
    nb = seq // TQ
    bias = _bias_tiles(rel_bias_table)
    gf = final_norm.reshape(1, D_MODEL)
    xf = x.reshape(batch * seq, D_MODEL)
    for l in range(depth):
        xf = _ffn(xf, ffn1_norm[l].reshape(1, D_MODEL), ffn1_w_gate[l].astype(BF16), ffn1_w_up[l].astype(BF16),
                  ffn1_w_down[l].astype(BF16), gf, False)
        wfm, wtm = _split_w_in(w_in[l])
        wkv = w_kv_up_c[l].astype(BF16)
        (qaT, vaT, qbT, vbT, qcT, qiT, wT, ka, kb, kmean, kc, vcT, ki) = _proj(
            xf, mix_norm[l].reshape(1, D_MODEL), wfm, wtm, kv_norm_c[l].reshape(1, C_KV_LATENT),
            wkv, wkv[:, HEAD_DIM:].T, batch, seq)
        kmean = kmean.reshape(batch, nb, N_HEADS_B, LANES).transpose(0, 2, 1, 3)
        out_a = _swa(attn_sinks[l], qaT, ka, vaT, bias, batch, seq)
        out_b = _moba(qbT, kb, vbT, kmean, bias, batch, seq)
        out_c = _dsa(qiT, wT, ki, qcT, kc, vcT, bias, batch, seq)
        wo = w_out[l].astype(BF16)
        mix = (out_a.reshape(batch * seq, A_Q), out_b.reshape(batch * seq, B_QKV), out_c.reshape(batch * seq, C_Q),
               wo[:A_Q], wo[A_Q:A_Q + B_QKV], wo[A_Q + B_QKV:])
        xf = _ffn(xf, ffn2_norm[l].reshape(1, D_MODEL), ffn2_w_gate[l].astype(BF16), ffn2_w_up[l].astype(BF16),
                  ffn2_w_down[l].astype(BF16), gf, l == depth - 1, mix)
    return xf.reshape(batch, seq, D_MODEL)
```
